```python
import jax, jax.numpy as jnp
from jax import lax
import numpy as np

D_MODEL = 2048
BATCH = 4
SEQ = 2048
DEPTH = 4
DEC_BATCH = 8
DEC_SEQ = 32
PAST_LEN = 2048

CHUNK = 64
EPS = 1e-6
D_FF = ((8 * D_MODEL // 3 + 127) // 128) * 128
N_MEM = 256
XA_HEADS = 4
XA_DH = D_MODEL // XA_HEADS
MLSTM_HEADS = 4
MLSTM_DH = D_MODEL // 8
MLSTM_W = MLSTM_HEADS * MLSTM_DH
RG_W = D_MODEL // 4
RG_BLOCKS = 8
RG_BD = RG_W // RG_BLOCKS
RG_C = 8.0
CONV_W = 4
HG_HEADS = 4
HG_DK = D_MODEL // 16
HG_DV = HG_DK
HG_W = HG_HEADS * HG_DK
IN_SPLITS = (MLSTM_W, MLSTM_W, MLSTM_W, MLSTM_W, MLSTM_HEADS, MLSTM_HEADS,
             RG_W, RG_W, HG_W, HG_W, HG_W, HG_W, D_MODEL, D_MODEL, D_MODEL)
IN_W = 4 * MLSTM_W + 2 * MLSTM_HEADS + 2 * RG_W + 4 * HG_W + 3 * D_MODEL

kernel_name = 'hybrid_streaming_encoder_step'


def _in_split_points():
    return [int(v) for v in np.cumsum(IN_SPLITS)[:-1]]


def rmsnorm(x, w):
    xf = x.astype(jnp.float32)
    y = xf * lax.rsqrt(jnp.mean(xf * xf, axis=-1, keepdims=True) + EPS)
    return (y * w.astype(jnp.float32)).astype(x.dtype)


def head_rms(x):
    xf = x.astype(jnp.float32)
    return xf * lax.rsqrt(jnp.mean(xf * xf, axis=-1, keepdims=True) + EPS)


def swiglu(h, w_in, w_out):
    g, u = jnp.split(h @ w_in, 2, axis=-1)
    return (jax.nn.silu(g) * u) @ w_out


def run_chunks(step, carry, seqs):
    T = seqs[0].shape[1]
    L = min(CHUNK, T)
    nC = T // L
    xs = tuple(s.reshape((s.shape[0], nC, L) + s.shape[2:]).swapaxes(0, 1) for s in seqs)
    carry, ys = lax.scan(step, carry, xs)
    ys = ys.swapaxes(0, 1).reshape((ys.shape[1], T) + ys.shape[3:])
    return carry, ys


def mlstm_chunk(carry, inp):
    C0, n0, m0 = carry
    q, k, v, ig, lf = inp
    L = q.shape[1]
    b = jnp.cumsum(lf, axis=1).swapaxes(1, 2)
    ig = ig.swapaxes(1, 2)
    causal = jnp.tril(jnp.ones((L, L), dtype=bool))
    log_w = jnp.where(causal, b[..., :, None] - b[..., None, :] + ig[..., None, :], -jnp.inf)
    log_inter = b + m0[..., None]
    m = jnp.maximum(log_inter, log_w.max(-1))
    w = jnp.exp(log_w - m[..., None])
    w_inter = jnp.exp(log_inter - m)
    wqk = w * jnp.einsum('bthd,bshd->bhts', q, k)
    num = (jnp.einsum('bhts,bshe->bthe', wqk, v)
           + jnp.einsum('bht,bthd,bhde->bthe', w_inter, q, C0))
    den = wqk.sum(-1) + w_inter * jnp.einsum('bthd,bhd->bht', q, n0)
    h = num / jnp.maximum(jnp.abs(den), jnp.exp(-m)).swapaxes(1, 2)[..., None]
    w_end = w[:, :, -1, :]
    s_end = w_inter[..., -1]
    C1 = s_end[..., None, None] * C0 + jnp.einsum('bhs,bshd,bshe->bhde', w_end, k, v)
    n1 = s_end[..., None] * n0 + jnp.einsum('bhs,bshd->bhd', w_end, k)
    return (C1, n1, m[..., -1]), h


def hgrn_chunk(S0, inp):
    q, lf, k, v = inp
    L = q.shape[1]
    cb = jnp.cumsum(lf, axis=1)
    causal = jnp.tril(jnp.ones((L, L), dtype=bool))[None, :, :, None, None]
    decay = jnp.exp(jnp.where(causal, cb[:, :, None] - cb[:, None, :], -jnp.inf))
    a = jnp.einsum('bthd,bshd,btshd->bhts', q, k, decay)
    o = (jnp.einsum('bhts,bshe->bthe', a, v)
         + jnp.einsum('bthd,bhde->bthe', q * jnp.exp(cb), S0))
    cb_end = cb[:, -1]
    S1 = (jnp.exp(cb_end)[..., None] * S0
          + jnp.einsum('bshd,bshe->bhde', k * jnp.exp(cb_end[:, None] - cb), v))
    return S1, o


def causal_conv(u, buf, w, b):
    T = u.shape[1]
    full = jnp.concatenate([buf.astype(u.dtype), u], axis=1)
    y = b
    for j in range(CONV_W):
        y = y + full[:, j:j + T] * w[j]
    return y, full[:, T:]


def rglru(xc, h0, wa, ba, wx, bx, lam):
    B, T, _ = xc.shape
    xf = xc.astype(jnp.float32)
    xb = xf.reshape(B, T, RG_BLOCKS, RG_BD)
    r = jax.nn.sigmoid(jnp.einsum('btnd,nde->btne', xb, wa).reshape(B, T, RG_W) + ba)
    i = jax.nn.sigmoid(jnp.einsum('btnd,nde->btne', xb, wx).reshape(B, T, RG_W) + bx)
    log_a = -RG_C * jax.nn.softplus(-lam.astype(jnp.float32)) * r
    a = jnp.exp(log_a)
    u = jnp.sqrt(-jnp.expm1(2.0 * log_a)) * (i * xf)

    def combine(e1, e2):
        return e1[0] * e2[0], e2[0] * e1[1] + e2[1]

    a_cum, u_cum = lax.associative_scan(combine, (a, u), axis=1)
    h = a_cum * h0[:, None, :] + u_cum
    return h, h[:, -1]


def hgrn_lower_bounds(logits):
    p = jax.nn.softmax(logits.astype(jnp.float32), axis=0)
    c = jnp.cumsum(p, axis=0)
    return c - c[0]


def token_mixing(h, st, P, l, lb):
    C0, n0, m0, hr0, buf0, S0 = st
    f32 = jnp.float32
    B, T, _ = h.shape
    z = h @ P['w_in'][l]
    (qm, km, vm, om, im, fm, rx, rgate, hq, hf, hi, hg, ga, gb, gc) = jnp.split(z, _in_split_points(), axis=-1)
    mshape = (B, T, MLSTM_HEADS, MLSTM_DH)
    qm = qm.reshape(mshape).astype(f32)
    km = km.reshape(mshape).astype(f32) * (MLSTM_DH ** -0.5)
    vm = vm.reshape(mshape).astype(f32)
    ig = (im + P['mlstm_bi'][l]).astype(f32)
    lf = jax.nn.log_sigmoid((fm + P['mlstm_bf'][l]).astype(f32))
    (C1, n1, m1), hm = run_chunks(mlstm_chunk, (C0.astype(f32), n0.astype(f32), m0.astype(f32)), (qm, km, vm, ig, lf))
    hm = (head_rms(hm).reshape(B, T, MLSTM_W) * P['mlstm_norm'][l]).astype(h.dtype)
    y_a = (jax.nn.sigmoid(om) * hm) @ P['w_br_a'][l]
    xc, buf1 = causal_conv(rx, buf0, P['rg_conv_w'][l], P['rg_conv_b'][l])
    hr, hr1 = rglru(xc, hr0.astype(f32), P['rg_wa'][l], P['rg_ba'][l], P['rg_wx'][l], P['rg_bx'][l], P['rg_lambda'][l])
    y_b = (hr.astype(h.dtype) * jax.nn.gelu(rgate)) @ P['w_br_b'][l]
    f = lb + (1.0 - lb) * jax.nn.sigmoid(hf.astype(f32))
    hshape = (B, T, HG_HEADS, HG_DK)
    S1, oh = run_chunks(hgrn_chunk, S0.astype(f32),
                        (hq.astype(f32).reshape(hshape), jnp.log(f).reshape(hshape),
                         (1.0 - f).reshape(hshape), hi.astype(f32).reshape(B, T, HG_HEADS, HG_DV)))
    oh = (head_rms(oh).reshape(B, T, HG_W) * P['hg_norm'][l]).astype(h.dtype)
    y_c = (oh * jax.nn.silu(hg)) @ P['w_br_c'][l]
    merged = jax.nn.sigmoid(ga) * y_a + jax.nn.sigmoid(gb) * y_b + jax.nn.sigmoid(gc) * y_c
    return merged @ P['w_out'][l], (C1, n1, m1, hr1, buf1, S1)


def memory_kv(mem, norm_w, wk, wv):
    B = mem.shape[0]
    mn = rmsnorm(mem, norm_w)
    k = (mn @ wk).reshape(B, N_MEM, XA_HEADS, XA_DH)
    v = (mn @ wv).reshape(B, N_MEM, XA_HEADS, XA_DH)
    return k, v


def cross_attend(h, k, v, wq, wo):
    B, T, _ = h.shape
    q = (h @ wq).reshape(B, T, XA_HEADS, XA_DH)
    s = jnp.einsum('bthd,bmhd->bhtm', q, k, preferred_element_type=jnp.float32) * (XA_DH ** -0.5)
    p = jax.nn.softmax(s, axis=-1).astype(v.dtype)
    o = jnp.einsum('bhtm,bmhd->bthd', p, v).reshape(B, T, D_MODEL)
    return o @ wo


def run_trunk(x, states, mem, mem_k, mem_v, P):
    lower = hgrn_lower_bounds(P['hg_lb_logits'])
    new_states = [[] for _ in range(6)]
    new_k, new_v = [], []
    for l in range(DEPTH):
        x = x + 0.5 * rmsnorm(swiglu(rmsnorm(x, P['ffn1_norm_pre'][l]), P['ffn1_w_in'][l], P['ffn1_w_out'][l]),
                              P['ffn1_norm_post'][l])
        st = tuple(s[l] for s in states)
        mix, st = token_mixing(rmsnorm(x, P['mix_norm_pre'][l]), st, P, l, lower[l])
        x = x + rmsnorm(mix, P['mix_norm_post'][l])
        if mem is not None:
            k_l, v_l = memory_kv(mem, P['xa_mem_norm'][l], P['xa_wk'][l], P['xa_wv'][l])
            new_k.append(k_l)
            new_v.append(v_l)
        else:
            k_l, v_l = mem_k[l], mem_v[l]
        x = x + rmsnorm(cross_attend(rmsnorm(x, P['xa_norm_pre'][l]), k_l, v_l, P['xa_wq'][l], P['xa_wo'][l]),
                        P['xa_norm_post'][l])
        x = x + 0.5 * rmsnorm(swiglu(rmsnorm(x, P['ffn2_norm_pre'][l]), P['ffn2_w_in'][l], P['ffn2_w_out'][l]),
                              P['ffn2_norm_post'][l])
        for lst, s in zip(new_states, st):
            lst.append(s)
    stacked = tuple(jnp.stack(lst) for lst in new_states)
    mem_out = (jnp.stack(new_k), jnp.stack(new_v)) if mem is not None else None
    return x, stacked, mem_out


def setup_inputs(seed: int = 0) -> dict:
    key = jax.random.key(seed)
    ks = iter(jax.random.split(key, 64))
    f32 = jnp.float32
    L = DEPTH
    D = D_MODEL

    def nrm(shape, scale=1.0):
        return jax.random.normal(next(ks), shape, f32) * scale

    def gain(shape):
        return 1.0 + 0.02 * nrm(shape)

    u = jax.random.uniform(next(ks), (L, RG_W), f32, 0.9, 0.999)
    a_base = u ** (1.0 / RG_C)
    rg_lambda = jnp.log(a_base) - jnp.log1p(-a_base)
    m_state = jax.random.uniform(next(ks), (L, DEC_BATCH, MLSTM_HEADS), f32, 0.0, 4.0)
    return {
        'x_prompt': nrm((BATCH, SEQ, D)),
        'x_sample': nrm((DEC_BATCH, DEC_SEQ, D)),
        'mem_prompt': nrm((BATCH, N_MEM, D)),
        'cache_mem_k': nrm((L, DEC_BATCH, N_MEM, XA_HEADS, XA_DH)),
        'cache_mem_v': nrm((L, DEC_BATCH, N_MEM, XA_HEADS, XA_DH)),
        'state_mlstm_C': nrm((L, DEC_BATCH, MLSTM_HEADS, MLSTM_DH, MLSTM_DH), 0.3),
        'state_mlstm_n': nrm((L, DEC_BATCH, MLSTM_HEADS, MLSTM_DH), 0.3),
        'state_mlstm_m': m_state,
        'state_rglru_h': nrm((L, DEC_BATCH, RG_W), 0.5),
        'state_rglru_conv': nrm((L, DEC_BATCH, CONV_W - 1, RG_W)),
        'state_hgrn_S': nrm((L, DEC_BATCH, HG_HEADS, HG_DK, HG_DV), 0.5),
        'ffn1_norm_pre': gain((L, D)),
        'ffn1_w_in': nrm((L, D, 2 * D_FF), D ** -0.5),
        'ffn1_w_out': nrm((L, D_FF, D), D_FF ** -0.5),
        'ffn1_norm_post': gain((L, D)),
        'mix_norm_pre': gain((L, D)),
        'w_in': nrm((L, D, IN_W), D ** -0.5),
        'mlstm_bi': nrm((L, MLSTM_HEADS), 0.1),
        'mlstm_bf': jnp.linspace(3.0, 6.0, MLSTM_HEADS, dtype=f32)[None] + nrm((L, MLSTM_HEADS), 0.1),
        'mlstm_norm': gain((L, MLSTM_W)),
        'rg_conv_w': nrm((L, CONV_W, RG_W), CONV_W ** -0.5),
        'rg_conv_b': nrm((L, RG_W), 0.02),
        'rg_wa': nrm((L, RG_BLOCKS, RG_BD, RG_BD), RG_BD ** -0.5),
        'rg_ba': nrm((L, RG_W), 0.02),
        'rg_wx': nrm((L, RG_BLOCKS, RG_BD, RG_BD), RG_BD ** -0.5),
        'rg_bx': nrm((L, RG_W), 0.02),
        'rg_lambda': rg_lambda,
        'hg_lb_logits': nrm((L, HG_W), 0.1),
        'hg_norm': gain((L, HG_W)),
        'w_br_a': nrm((L, MLSTM_W, D), MLSTM_W ** -0.5),
        'w_br_b': nrm((L, RG_W, D), RG_W ** -0.5),
        'w_br_c': nrm((L, HG_W, D), HG_W ** -0.5),
        'w_out': nrm((L, D, D), D ** -0.5),
        'mix_norm_post': gain((L, D)),
        'xa_norm_pre': gain((L, D)),
        'xa_mem_norm': gain((L, D)),
        'xa_wq': nrm((L, D, D), D ** -0.5),
        'xa_wk': nrm((L, D, D), D ** -0.5),
        'xa_wv': nrm((L, D, D), D ** -0.5),
        'xa_wo': nrm((L, D, D), D ** -0.5),
        'xa_norm_post': gain((L, D)),
        'ffn2_norm_pre': gain((L, D)),
        'ffn2_w_in': nrm((L, D, 2 * D_FF), D ** -0.5),
        'ffn2_w_out': nrm((L, D_FF, D), D_FF ** -0.5),
        'ffn2_norm_post': gain((L, D)),
    }


def reference(x_prompt, x_sample, mem_prompt, cache_mem_k, cache_mem_v,
              state_mlstm_C, state_mlstm_n, state_mlstm_m, state_rglru_h, state_rglru_conv, state_hgrn_S,
              ffn1_norm_pre, ffn1_w_in, ffn1_w_out, ffn1_norm_post,
              mix_norm_pre, w_in, mlstm_bi, mlstm_bf, mlstm_norm,
              rg_conv_w, rg_conv_b, rg_wa, rg_ba, rg_wx, rg_bx, rg_lambda,
              hg_lb_logits, hg_norm, w_br_a, w_br_b, w_br_c, w_out, mix_norm_post,
              xa_norm_pre, xa_mem_norm, xa_wq, xa_wk, xa_wv, xa_wo, xa_norm_post,
              ffn2_norm_pre, ffn2_w_in, ffn2_w_out, ffn2_norm_post):
    P = dict(ffn1_norm_pre=ffn1_norm_pre, ffn1_w_in=ffn1_w_in, ffn1_w_out=ffn1_w_out, ffn1_norm_post=ffn1_norm_post,
             mix_norm_pre=mix_norm_pre, w_in=w_in, mlstm_bi=mlstm_bi, mlstm_bf=mlstm_bf, mlstm_norm=mlstm_norm,
             rg_conv_w=rg_conv_w, rg_conv_b=rg_conv_b, rg_wa=rg_wa, rg_ba=rg_ba, rg_wx=rg_wx, rg_bx=rg_bx,
             rg_lambda=rg_lambda, hg_lb_logits=hg_lb_logits, hg_norm=hg_norm,
             w_br_a=w_br_a, w_br_b=w_br_b, w_br_c=w_br_c, w_out=w_out, mix_norm_post=mix_norm_post,
             xa_norm_pre=xa_norm_pre, xa_mem_norm=xa_mem_norm, xa_wq=xa_wq, xa_wk=xa_wk, xa_wv=xa_wv,
             xa_wo=xa_wo, xa_norm_post=xa_norm_post,
             ffn2_norm_pre=ffn2_norm_pre, ffn2_w_in=ffn2_w_in, ffn2_w_out=ffn2_w_out, ffn2_norm_post=ffn2_norm_post)
    f32 = jnp.float32
    B = x_prompt.shape[0]
    init = (jnp.zeros((DEPTH, B, MLSTM_HEADS, MLSTM_DH, MLSTM_DH), f32),
            jnp.zeros((DEPTH, B, MLSTM_HEADS, MLSTM_DH), f32),
            jnp.zeros((DEPTH, B, MLSTM_HEADS), f32),
            jnp.zeros((DEPTH, B, RG_W), f32),
            jnp.zeros((DEPTH, B, CONV_W - 1, RG_W), x_prompt.dtype),
            jnp.zeros((DEPTH, B, HG_HEADS, HG_DK, HG_DV), f32))
    y_prompt, p_states, p_mem = run_trunk(x_prompt, init, mem_prompt, None, None, P)
    p_C, p_n, p_m, p_h, p_conv, p_S = p_states
    p_mem_k, p_mem_v = p_mem
    s_init = (state_mlstm_C, state_mlstm_n, state_mlstm_m, state_rglru_h, state_rglru_conv, state_hgrn_S)
    y_sample, s_states, _ = run_trunk(x_sample, s_init, None, cache_mem_k, cache_mem_v, P)
    s_C, s_n, s_m, s_h, s_conv, s_S = s_states
    return (y_prompt, y_sample, p_C, p_n, p_m, p_h, p_conv, p_S, p_mem_k, p_mem_v,
            s_C, s_n, s_m, s_h, s_conv, s_S)
```

```python
import functools

import jax
import jax.numpy as jnp
import numpy as np
from jax import lax
from jax.experimental import pallas as pl
from jax.experimental.pallas import tpu as pltpu

F32 = jnp.float32
BF16 = jnp.bfloat16
EPS = 1e-6

V7X_VMEM_LIMIT_BYTES = 56 * 1024 * 1024
LANES = 128

XA_HEADS = 4
MLSTM_HEADS = 4
HG_HEADS = 4
RG_BLOCKS = 8
RG_C = 8.0
CONV_W = 4
D_FF_TILE = 512
MLSTM_CHUNK = 256
RG_CHUNK = 256
HG_CHUNK = 64
HG_TBLOCK = 16
GATE_PAD = LANES


def _params(*sem):
    return pltpu.CompilerParams(dimension_semantics=sem, vmem_limit_bytes=V7X_VMEM_LIMIT_BYTES)


def _rms(x, w):
    return x * lax.rsqrt(jnp.mean(x * x, axis=-1, keepdims=True) + EPS) * w


def _log_sigmoid(x):
    return jnp.minimum(x, 0.0) - jnp.log1p(jnp.exp(-jnp.abs(x)))


def _softplus(x):
    return jnp.maximum(x, 0.0) + jnp.log1p(jnp.exp(-jnp.abs(x)))


def _dot(a, b):
    return jnp.dot(a, b, preferred_element_type=F32)


def _dot_nt(a, b):
    return lax.dot_general(a, b, (((1,), (1,)), ((), ())), preferred_element_type=F32)


def _dot_tn(a, b):
    return lax.dot_general(a, b, (((0,), (0,)), ((), ())), preferred_element_type=F32)


def _shift_rows(x, d, fill, ridx):
    return jnp.where(ridx >= d, pltpu.roll(x, d, 0), fill)


def _ffn_kernel(x_ref, npre_ref, wg_ref, wu_ref, wo_ref, npost_ref, o_ref, hn_ref, acc_ref):
    j = pl.program_id(1)

    @pl.when(j == 0)
    def _():
        hn_ref[...] = _rms(x_ref[...], npre_ref[...]).astype(BF16)
        acc_ref[...] = jnp.zeros_like(acc_ref)

    hn = hn_ref[...]
    g = _dot(hn, wg_ref[...])
    u = _dot(hn, wu_ref[...])
    a = (g * jax.nn.sigmoid(g) * u).astype(BF16)
    acc_ref[...] += _dot(a, wo_ref[...])

    @pl.when(j == pl.num_programs(1) - 1)
    def _():
        o_ref[...] = x_ref[...] + 0.5 * _rms(acc_ref[...], npost_ref[...])


def _ffn(x, npre, w_in, w_out, npost, l, tm):
    M, D = x.shape
    Fp = w_out.shape[1]
    tn = D_FF_TILE
    nj = Fp // tn
    return pl.pallas_call(
        _ffn_kernel,
        grid=(M // tm, nj),
        in_specs=[
            pl.BlockSpec((tm, D), lambda i, j: (i, 0)),
            pl.BlockSpec((None, 1, D), lambda i, j: (l, 0, 0)),
            pl.BlockSpec((None, D, tn), lambda i, j: (l, 0, j)),
            pl.BlockSpec((None, D, tn), lambda i, j: (l, 0, j + nj)),
            pl.BlockSpec((None, tn, D), lambda i, j: (l, j, 0)),
            pl.BlockSpec((None, 1, D), lambda i, j: (l, 0, 0)),
        ],
        out_specs=pl.BlockSpec((tm, D), lambda i, j: (i, 0)),
        out_shape=jax.ShapeDtypeStruct((M, D), F32),
        scratch_shapes=[pltpu.VMEM((tm, D), BF16), pltpu.VMEM((tm, D), F32)],
        compiler_params=_params("parallel", "arbitrary"),
        name="ffn",
    )(x, npre, w_in, w_in, w_out, npost)


def _nmm_kernel(x_ref, nw_ref, w_ref, o_ref, hn_ref):
    @pl.when(pl.program_id(1) == 0)
    def _():
        hn_ref[...] = _rms(x_ref[...], nw_ref[...]).astype(BF16)

    o_ref[...] = _dot(hn_ref[...], w_ref[...]).astype(o_ref.dtype)


def _norm_matmul(x, nw, w, l, out_dtype, tm, tn, name):
    M, D = x.shape
    N = w.shape[-1]
    return pl.pallas_call(
        _nmm_kernel,
        grid=(M // tm, N // tn),
        in_specs=[
            pl.BlockSpec((tm, D), lambda i, j: (i, 0)),
            pl.BlockSpec((None, 1, D), lambda i, j: (l, 0, 0)),
            pl.BlockSpec((None, D, tn), lambda i, j: (l, 0, j)),
        ],
        out_specs=pl.BlockSpec((tm, tn), lambda i, j: (i, j)),
        out_shape=jax.ShapeDtypeStruct((M, N), out_dtype),
        scratch_shapes=[pltpu.VMEM((tm, D), BF16)],
        compiler_params=_params("parallel", "arbitrary"),
        name=name,
    )(x, nw, w)


def _mlstm_kernel(q_ref, k_ref, v_ref, og_ref, g_ref, gb_ref, nw_ref, C0_ref, n0_ref, m0_ref,
                  hm_ref, C_ref, n_ref, m_ref, *, L, dh):
    H = MLSTM_HEADS

    @pl.when(pl.program_id(1) == 0)
    def _():
        C_ref[...] = C0_ref[...]
        n_ref[...] = n0_ref[...]
        m_ref[...] = m0_ref[...]

    ga = g_ref[...] + gb_ref[...]
    m_prev = m_ref[0]
    row = lax.broadcasted_iota(jnp.int32, (L, L), 0)
    col = lax.broadcasted_iota(jnp.int32, (L, L), 1)
    causal = row >= col
    eye = row == col

    def as_row(x_col):
        return jnp.sum(jnp.where(eye, x_col, 0.0), axis=0, keepdims=True)

    for h in range(H):
        hs = slice(h * dh, (h + 1) * dh)
        ig_c = ga[:, h:h + 1]
        lf_c = _log_sigmoid(ga[:, H + h:H + h + 1])
        ig_r = as_row(ig_c)
        lf_r = as_row(lf_c)
        b_c = jnp.sum(jnp.where(causal, lf_r, 0.0), axis=1, keepdims=True)
        b_r = jnp.sum(jnp.where(row <= col, lf_c, 0.0), axis=0, keepdims=True)
        m0 = m_prev[:, h:h + 1]
        log_w = jnp.where(causal, b_c - b_r + ig_r, -jnp.inf)
        log_inter = b_c + m0
        m_c = jnp.maximum(log_inter, jnp.max(log_w, axis=1, keepdims=True))
        w = jnp.exp(log_w - m_c)
        w_inter = jnp.exp(log_inter - m_c)

        q = q_ref[:, hs]
        k = k_ref[:, hs] * (dh ** -0.5)
        v = v_ref[:, hs]
        C0 = C_ref[0, h]
        n0 = n_ref[0, h:h + 1, :]
        wqk = w * _dot_nt(q, k)
        num = _dot(wqk.astype(BF16), v) + w_inter * _dot(q, C0.astype(BF16))
        den = (jnp.sum(wqk, axis=1, keepdims=True)
               + w_inter * jnp.sum(q.astype(F32) * n0, axis=1, keepdims=True))
        hh = num / jnp.maximum(jnp.abs(den), jnp.exp(-m_c))
        hn = hh * lax.rsqrt(jnp.mean(hh * hh, axis=-1, keepdims=True) + EPS) * nw_ref[:, hs]
        hm_ref[:, hs] = (jax.nn.sigmoid(og_ref[:, hs].astype(F32)) * hn).astype(BF16)

        m_end = m_c[L - 1:L, :]
        w_end = jnp.exp(b_c[L - 1:L, :] - b_c + ig_c - m_end)
        s_end = w_inter[L - 1:L, :]
        kw = k.astype(F32) * w_end
        C_ref[0, h] = s_end * C0 + _dot_tn(kw.astype(BF16), v)
        n_ref[0, h:h + 1, :] = s_end * n0 + jnp.sum(kw, axis=0, keepdims=True)
        m_ref[0, :, h:h + 1] = m_end


def _mlstm(z, gates, gate_bias, norm_w, C0, n0, m0, l, ls, B, T, col0):
    H = MLSTM_HEADS
    dh = C0.shape[-1]
    W = H * dh
    L = min(MLSTM_CHUNK, T)
    nC = T // L
    M = B * T
    zspec = lambda cb: pl.BlockSpec((L, W), lambda b, c: (b * nC + c, cb))
    return pl.pallas_call(
        functools.partial(_mlstm_kernel, L=L, dh=dh),
        grid=(B, nC),
        in_specs=[
            zspec(col0), zspec(col0 + 1), zspec(col0 + 2), zspec(col0 + 3),
            pl.BlockSpec((L, GATE_PAD), lambda b, c: (b * nC + c, 0)),
            pl.BlockSpec((None, 1, GATE_PAD), lambda b, c: (l, 0, 0)),
            pl.BlockSpec((None, 1, W), lambda b, c: (l, 0, 0)),
            pl.BlockSpec((None, 1, H, dh, dh), lambda b, c: (ls, b, 0, 0, 0)),
            pl.BlockSpec((None, 1, H, dh), lambda b, c: (ls, b, 0, 0)),
            pl.BlockSpec((None, 1, 1, H), lambda b, c: (ls, b, 0, 0)),
        ],
        out_specs=[
            pl.BlockSpec((L, W), lambda b, c: (b * nC + c, 0)),
            pl.BlockSpec((1, H, dh, dh), lambda b, c: (b, 0, 0, 0)),
            pl.BlockSpec((1, H, dh), lambda b, c: (b, 0, 0)),
            pl.BlockSpec((1, 1, H), lambda b, c: (b, 0, 0)),
        ],
        out_shape=[
            jax.ShapeDtypeStruct((M, W), BF16),
            jax.ShapeDtypeStruct((B, H, dh, dh), F32),
            jax.ShapeDtypeStruct((B, H, dh), F32),
            jax.ShapeDtypeStruct((B, 1, H), F32),
        ],
        compiler_params=_params("parallel", "arbitrary"),
        name="mlstm",
    )(z, z, z, z, gates, gate_bias, norm_w, C0, n0, m0)


def _rglru_kernel(rx_ref, rg_ref, cw_ref, cbias_ref, wa_ref, ba_ref, wx_ref, bx_ref, lam_ref, buf0_ref, h0_ref,
                  y_ref, buf_ref, h_ref, cbuf_ref, *, L):
    c = pl.program_id(1)
    W = rx_ref.shape[-1]
    TAIL = CONV_W - 1

    @pl.when(c == 0)
    def _():
        cbuf_ref[0:8, :] = jnp.zeros((8, W), F32)
        cbuf_ref[8 - TAIL:8, :] = buf0_ref[0]
        h_ref[...] = h0_ref[...]

    cbuf_ref[8:8 + L, :] = rx_ref[...].astype(F32)
    xc = cbias_ref[...]
    for j in range(CONV_W):
        xc = xc + cbuf_ref[8 - TAIL + j:8 - TAIL + j + L, :] * cw_ref[j:j + 1, :]

    xcb = xc.astype(BF16)
    r = jax.nn.sigmoid(_dot(xcb, wa_ref[...]) + ba_ref[...])
    i = jax.nn.sigmoid(_dot(xcb, wx_ref[...]) + bx_ref[...])
    log_a = (-RG_C * _softplus(-lam_ref[...])) * r
    a = jnp.exp(log_a)
    th = jnp.tanh(log_a)
    u = jnp.sqrt(-2.0 * th / (1.0 - th)) * (i * xc)

    ridx = lax.broadcasted_iota(jnp.int32, (L, W), 0)
    d = 1
    while d < L:
        a_sh = _shift_rows(a, d, 1.0, ridx)
        u_sh = _shift_rows(u, d, 0.0, ridx)
        u = a * u_sh + u
        a = a * a_sh
        d *= 2
    h = a * h_ref[0] + u
    h_ref[0] = h[L - 1:L, :]
    y_ref[...] = (h * jax.nn.gelu(rg_ref[...].astype(F32))).astype(BF16)

    cbuf_ref[0:8, :] = cbuf_ref[L:L + 8, :]

    @pl.when(c == pl.num_programs(1) - 1)
    def _():
        buf_ref[0] = cbuf_ref[8 + L - TAIL:8 + L, :]


def _rglru(z, P, buf0, h0, l, ls, B, T, col0):
    W = h0.shape[-1]
    L = min(RG_CHUNK, T)
    nC = T // L
    M = B * T
    vec = lambda: pl.BlockSpec((None, 1, W), lambda b, c: (l, 0, 0))
    mat = lambda: pl.BlockSpec((None, W, W), lambda b, c: (l, 0, 0))
    return pl.pallas_call(
        functools.partial(_rglru_kernel, L=L),
        grid=(B, nC),
        in_specs=[
            pl.BlockSpec((L, W), lambda b, c: (b * nC + c, col0)),
            pl.BlockSpec((L, W), lambda b, c: (b * nC + c, col0 + 1)),
            pl.BlockSpec((None, CONV_W, W), lambda b, c: (l, 0, 0)),
            vec(), mat(), vec(), mat(), vec(), vec(),
            pl.BlockSpec((None, 1, CONV_W - 1, W), lambda b, c: (ls, b, 0, 0)),
            pl.BlockSpec((None, 1, 1, W), lambda b, c: (ls, b, 0, 0)),
        ],
        out_specs=[
            pl.BlockSpec((L, W), lambda b, c: (b * nC + c, 0)),
            pl.BlockSpec((1, CONV_W - 1, W), lambda b, c: (b, 0, 0)),
            pl.BlockSpec((1, 1, W), lambda b, c: (b, 0, 0)),
        ],
        out_shape=[
            jax.ShapeDtypeStruct((M, W), BF16),
            jax.ShapeDtypeStruct((B, CONV_W - 1, W), F32),
            jax.ShapeDtypeStruct((B, 1, W), F32),
        ],
        scratch_shapes=[pltpu.VMEM((L + 8, W), F32)],
        compiler_params=_params("parallel", "arbitrary"),
        name="rglru",
    )(z, z, P['rg_conv_w'], P['rg_conv_b'], P['rg_wa'], P['rg_ba'], P['rg_wx'], P['rg_bx'], P['rg_lambda'],
      buf0, h0)


def _hgrn_kernel(q_ref, f_ref, i_ref, g_ref, lbl_ref, nw_ref, S0_ref, y_ref, S_ref,
                 cb_ref, k_ref, v_ref, *, L, layer):
    H = HG_HEADS
    W = q_ref.shape[-1]
    dk = W // H
    TB = min(HG_TBLOCK, L)

    @pl.when(pl.program_id(1) == 0)
    def _():
        S_ref[...] = S0_ref[...]

    lg = lbl_ref[...]
    e = jnp.exp(lg - jnp.max(lg, axis=0, keepdims=True))
    p = e / jnp.sum(e, axis=0, keepdims=True)
    lb = jnp.zeros((1, W), F32)
    for r in range(1, layer + 1):
        lb = lb + p[r:r + 1, :]

    f = lb + (1.0 - lb) * jax.nn.sigmoid(f_ref[...].astype(F32))
    cb = jnp.log(f)
    ridx = lax.broadcasted_iota(jnp.int32, (L, W), 0)
    d = 1
    while d < L:
        cb = cb + _shift_rows(cb, d, 0.0, ridx)
        d *= 2
    kk = 1.0 - f
    q = q_ref[...].astype(F32)
    v = i_ref[...].astype(F32)
    cb_ref[...] = cb
    k_ref[...] = kk
    v_ref[...] = v

    cb_end = cb[L - 1:L, :]
    qe = (q * jnp.exp(cb)).astype(BF16)
    ke = (kk * jnp.exp(cb_end - cb)).astype(BF16)
    vb = v.astype(BF16)
    o_inter = []
    eye = (lax.broadcasted_iota(jnp.int32, (dk, dk), 0) == lax.broadcasted_iota(jnp.int32, (dk, dk), 1))
    for h in range(H):
        hs = slice(h * dk, (h + 1) * dk)
        S = S_ref[0, h]
        o_inter.append(_dot(qe[:, hs], S.astype(BF16)))
        dec_col = jnp.sum(jnp.where(eye, jnp.exp(cb_end[:, hs]), 0.0), axis=1, keepdims=True)
        S_ref[0, h] = dec_col * S + _dot_tn(ke[:, hs], vb[:, hs])
    o_inter = jnp.concatenate(o_inter, axis=1)

    gate = g_ref[...].astype(F32)
    gate = gate * jax.nn.sigmoid(gate)
    nw = nw_ref[...]
    for tb in range(L // TB):
        rs = slice(tb * TB, (tb + 1) * TB)
        qt = q[rs]
        cbt = cb[rs]
        tid = lax.broadcasted_iota(jnp.int32, (TB, W), 0) + tb * TB

        def body(s, acc, qt=qt, cbt=cbt, tid=tid):
            cbs = cb_ref[pl.ds(s, 1), :]
            ks = k_ref[pl.ds(s, 1), :]
            vs = v_ref[pl.ds(s, 1), :]
            pr = qt * (ks * jnp.exp(jnp.where(tid >= s, cbt - cbs, -jnp.inf)))
            upd = [jnp.sum(pr[:, h * dk:(h + 1) * dk], axis=1, keepdims=True) * vs[:, h * dk:(h + 1) * dk]
                   for h in range(H)]
            return acc + jnp.concatenate(upd, axis=1)

        o = lax.fori_loop(0, (tb + 1) * TB, body, o_inter[rs])
        outs = []
        for h in range(H):
            oh = o[:, h * dk:(h + 1) * dk]
            outs.append(oh * lax.rsqrt(jnp.mean(oh * oh, axis=-1, keepdims=True) + EPS))
        y_ref[rs, :] = (jnp.concatenate(outs, axis=1) * nw * gate[rs]).astype(BF16)


def _hgrn(z, lb_logits, norm_w, S0, l, ls, B, T, col0):
    H, dk, dv = S0.shape[-3:]
    W = H * dk
    L = min(HG_CHUNK, T)
    nC = T // L
    M = B * T
    depth = lb_logits.shape[0]
    zspec = lambda cb: pl.BlockSpec((L, W), lambda b, c: (b * nC + c, cb))
    return pl.pallas_call(
        functools.partial(_hgrn_kernel, L=L, layer=l),
        grid=(B, nC),
        in_specs=[
            zspec(col0), zspec(col0 + 1), zspec(col0 + 2), zspec(col0 + 3),
            pl.BlockSpec((depth, W), lambda b, c: (0, 0)),
            pl.BlockSpec((None, 1, W), lambda b, c: (l, 0, 0)),
            pl.BlockSpec((None, 1, H, dk, dv), lambda b, c: (ls, b, 0, 0, 0)),
        ],
        out_specs=[
            pl.BlockSpec((L, W), lambda b, c: (b * nC + c, 0)),
            pl.BlockSpec((1, H, dk, dv), lambda b, c: (b, 0, 0, 0)),
        ],
        out_shape=[
            jax.ShapeDtypeStruct((M, W), BF16),
            jax.ShapeDtypeStruct((B, H, dk, dv), F32),
        ],
        scratch_shapes=[pltpu.VMEM((L, W), F32), pltpu.VMEM((L, W), F32), pltpu.VMEM((L, W), F32)],
        compiler_params=_params("parallel", "arbitrary"),
        name="hgrn",
    )(z, z, z, z, lb_logits, norm_w, S0)


def _merge_kernel(a_ref, b_ref, c_ref, ga_ref, gb_ref, gc_ref, wa_ref, wb_ref, wc_ref, wo_ref, npost_ref, x_ref,
                  o_ref):
    sig = lambda r: jax.nn.sigmoid(r[...].astype(F32))
    merged = (sig(ga_ref) * _dot(a_ref[...], wa_ref[...])
              + sig(gb_ref) * _dot(b_ref[...], wb_ref[...])
              + sig(gc_ref) * _dot(c_ref[...], wc_ref[...]))
    out = _dot(merged.astype(BF16), wo_ref[...])
    o_ref[...] = x_ref[...] + _rms(out, npost_ref[...])


def _merge(x, ya, yb, yc, z, P, l, tm):
    M, D = x.shape
    rows = lambda w: pl.BlockSpec((tm, w), lambda i: (i, 0))
    wspec = lambda k: pl.BlockSpec((None, k, D), lambda i: (l, 0, 0))
    return pl.pallas_call(
        _merge_kernel,
        grid=(M // tm,),
        in_specs=[
            rows(ya.shape[1]), rows(yb.shape[1]), rows(yc.shape[1]),
            pl.BlockSpec((tm, D), lambda i: (i, 0)),
            pl.BlockSpec((tm, D), lambda i: (i, 1)),
            pl.BlockSpec((tm, D), lambda i: (i, 2)),
            wspec(ya.shape[1]), wspec(yb.shape[1]), wspec(yc.shape[1]), wspec(D),
            pl.BlockSpec((None, 1, D), lambda i: (l, 0, 0)),
            rows(D),
        ],
        out_specs=rows(D),
        out_shape=jax.ShapeDtypeStruct((M, D), F32),
        compiler_params=_params("parallel"),
        name="merge",
    )(ya, yb, yc, z, z, z, P['w_br_a'], P['w_br_b'], P['w_br_c'], P['w_out'], P['mix_norm_post'], x)


def _xattn_kernel(x_ref, npre_ref, wq_ref, k_ref, v_ref, wo_ref, npost_ref, o_ref, kb_ref, vb_ref):
    @pl.when(pl.program_id(1) == 0)
    def _():
        kb_ref[...] = k_ref[...].astype(BF16)
        vb_ref[...] = v_ref[...].astype(BF16)

    D = x_ref.shape[-1]
    dh = D // XA_HEADS
    x = x_ref[...]
    q = _dot(_rms(x, npre_ref[...]).astype(BF16), wq_ref[...]).astype(BF16)
    outs = []
    for h in range(XA_HEADS):
        hs = slice(h * dh, (h + 1) * dh)
        s = _dot_nt(q[:, hs], kb_ref[:, hs]) * (dh ** -0.5)
        e = jnp.exp(s - jnp.max(s, axis=-1, keepdims=True))
        p = e / jnp.sum(e, axis=-1, keepdims=True)
        outs.append(_dot(p.astype(BF16), vb_ref[:, hs]).astype(BF16))
    out = _dot(jnp.concatenate(outs, axis=1), wo_ref[...])
    o_ref[...] = x + _rms(out, npost_ref[...])


def _xattn(x, mem_k, mem_v, P, l, lk, B, T, tq):
    M, D = x.shape
    n_mem = mem_k.shape[2]
    nT = T // tq
    wspec = lambda: pl.BlockSpec((None, D, D), lambda b, t: (l, 0, 0))
    nspec = lambda: pl.BlockSpec((None, 1, D), lambda b, t: (l, 0, 0))
    kvspec = lambda: pl.BlockSpec((None, None, n_mem, D), lambda b, t: (lk, b, 0, 0))
    return pl.pallas_call(
        _xattn_kernel,
        grid=(B, nT),
        in_specs=[
            pl.BlockSpec((tq, D), lambda b, t: (b * nT + t, 0)),
            nspec(), wspec(), kvspec(), kvspec(), wspec(), nspec(),
        ],
        out_specs=pl.BlockSpec((tq, D), lambda b, t: (b * nT + t, 0)),
        out_shape=jax.ShapeDtypeStruct((M, D), F32),
        scratch_shapes=[pltpu.VMEM((n_mem, D), BF16), pltpu.VMEM((n_mem, D), BF16)],
        compiler_params=_params("parallel", "arbitrary"),
        name="xattn",
    )(x, P['xa_norm_pre'], P['xa_wq'], mem_k, mem_v, P['xa_wo'], P['xa_norm_post'])


def _prep_params(R):
    depth, D = R['ffn1_norm_pre'].shape
    P = {}
    row = lambda a: a.astype(F32).reshape(depth, 1, a.shape[-1])
    for name in ('ffn1_norm_pre', 'ffn1_norm_post', 'mix_norm_pre', 'mix_norm_post', 'xa_norm_pre', 'xa_mem_norm',
                 'xa_norm_post', 'ffn2_norm_pre', 'ffn2_norm_post', 'mlstm_norm', 'rg_conv_b', 'rg_ba', 'rg_bx',
                 'rg_lambda', 'hg_norm'):
        P[name] = row(R[name])
    P['rg_conv_w'] = R['rg_conv_w'].astype(F32)
    P['hg_lb_logits'] = R['hg_lb_logits'].astype(F32)

    for name in ('ffn1', 'ffn2'):
        w_in, w_out = R[name + '_w_in'], R[name + '_w_out']
        F = w_out.shape[1]
        Fp = -(-F // D_FF_TILE) * D_FF_TILE
        padc = lambda a: jnp.pad(a, ((0, 0), (0, 0), (0, Fp - F)))
        P[name + '_w_in'] = jnp.concatenate([padc(w_in[..., :F]), padc(w_in[..., F:])], axis=-1).astype(BF16)
        P[name + '_w_out'] = jnp.pad(w_out, ((0, 0), (0, Fp - F), (0, 0))).astype(BF16)

    mw = R['w_br_a'].shape[1]
    rw = R['w_br_b'].shape[1]
    hw = R['w_br_c'].shape[1]
    nh = R['mlstm_bi'].shape[1]
    splits = (mw, mw, mw, mw, nh, nh, rw, rw, hw, hw, hw, hw, D, D, D)
    pts = [0] + [int(v) for v in np.cumsum(splits)]
    seg = lambda n: R['w_in'][..., pts[n]:pts[n + 1]]
    order = (12, 13, 14, 0, 1, 2, 3, 6, 7, 8, 9, 10, 11)
    P['w_in_main'] = jnp.concatenate([seg(n) for n in order], axis=-1).astype(BF16)
    P['w_in_gate'] = jnp.pad(jnp.concatenate([seg(4), seg(5)], axis=-1),
                             ((0, 0), (0, 0), (0, GATE_PAD - 2 * nh))).astype(BF16)
    P['gate_bias'] = jnp.pad(jnp.concatenate([R['mlstm_bi'], R['mlstm_bf']], axis=-1).astype(F32),
                             ((0, 0), (0, GATE_PAD - 2 * nh))).reshape(depth, 1, GATE_PAD)

    eye = jnp.eye(RG_BLOCKS, dtype=F32)
    dense = lambda w: jnp.einsum('lnde,nm->lndme', w, eye).reshape(depth, rw, rw).astype(BF16)
    P['rg_wa'] = dense(R['rg_wa'])
    P['rg_wx'] = dense(R['rg_wx'])

    for name in ('w_br_a', 'w_br_b', 'w_br_c', 'w_out', 'xa_wq', 'xa_wk', 'xa_wv', 'xa_wo'):
        P[name] = R[name].astype(BF16)
    return P


def _run_trunk(x3, states, state_has_layers, mem, mem_k, mem_v, P):
    B, T, D = x3.shape
    M = B * T
    depth = P['ffn1_norm_pre'].shape[0]
    x = x3.reshape(M, D)
    tm_ffn = min(512, M)
    tm_proj = min(1024, M)
    tm_merge = min(256, M)
    tq = min(256, T)
    C0, n0, m0, hr0, buf0, S0 = states
    m0 = m0.reshape(m0.shape[0], m0.shape[1], 1, m0.shape[2])
    hr0 = hr0.reshape(hr0.shape[0], hr0.shape[1], 1, hr0.shape[2])
    mw = P['w_br_a'].shape[1]
    rw = P['w_br_b'].shape[1]
    hw = P['w_br_c'].shape[1]
    col_m = 3 * D // mw
    col_r = (3 * D + 4 * mw) // rw
    col_h = (3 * D + 4 * mw + 2 * rw) // hw

    if mem is not None:
        n_mem = mem.shape[1]
        mem2 = mem.reshape(B * n_mem, D)
        ks, vs = [], []
        for l in range(depth):
            ks.append(_norm_matmul(mem2, P['xa_mem_norm'], P['xa_wk'], l, F32, B * n_mem, 1024, "mem_k"))
            vs.append(_norm_matmul(mem2, P['xa_mem_norm'], P['xa_wv'], l, F32, B * n_mem, 1024, "mem_v"))
        mem_k = jnp.stack(ks).reshape(depth, B, n_mem, D)
        mem_v = jnp.stack(vs).reshape(depth, B, n_mem, D)

    outs = [[] for _ in range(6)]
    for l in range(depth):
        ls = l if state_has_layers else 0
        x = _ffn(x, P['ffn1_norm_pre'], P['ffn1_w_in'], P['ffn1_w_out'], P['ffn1_norm_post'], l, tm_ffn)
        z = _norm_matmul(x, P['mix_norm_pre'], P['w_in_main'], l, BF16, tm_proj, 1024, "mix_in")
        gates = _norm_matmul(x, P['mix_norm_pre'], P['w_in_gate'], l, F32, tm_proj, GATE_PAD, "mix_gates")
        ya, C1, n1, m1 = _mlstm(z, gates, P['gate_bias'], P['mlstm_norm'], C0, n0, m0, l, ls, B, T, col_m)
        yb, buf1, hr1 = _rglru(z, P, buf0, hr0, l, ls, B, T, col_r)
        yc, S1 = _hgrn(z, P['hg_lb_logits'], P['hg_norm'], S0, l, ls, B, T, col_h)
        x = _merge(x, ya, yb, yc, z, P, l, tm_merge)
        x = _xattn(x, mem_k, mem_v, P, l, l, B, T, tq)
        x = _ffn(x, P['ffn2_norm_pre'], P['ffn2_w_in'], P['ffn2_w_out'], P['ffn2_norm_post'], l, tm_ffn)
        for lst, s in zip(outs, (C1, n1, m1.reshape(B, -1), hr1.reshape(B, -1), buf1, S1)):
            lst.append(s)
    stacked = tuple(jnp.stack(lst) for lst in outs)
    return x.reshape(B, T, D), stacked, mem_k, mem_v


def kernel(x_prompt, x_sample, mem_prompt, cache_mem_k, cache_mem_v, state_mlstm_C, state_mlstm_n, state_mlstm_m, state_rglru_h, state_rglru_conv, state_hgrn_S, ffn1_norm_pre, ffn1_w_in, ffn1_w_out, ffn1_norm_post, mix_norm_pre, w_in, mlstm_bi, mlstm_bf, mlstm_norm, rg_conv_w, rg_conv_b, rg_wa, rg_ba, rg_wx, rg_bx, rg_lambda, hg_lb_logits, hg_norm, w_br_a, w_br_b, w_br_c, w_out, mix_norm_post, xa_norm_pre, xa_mem_norm, xa_wq, xa_wk, xa_wv, xa_wo, xa_norm_post, ffn2_norm_pre, ffn2_w_in, ffn2_w_out, ffn2_norm_post):
    R = dict(ffn1_norm_pre=ffn1_norm_pre, ffn1_w_in=ffn1_w_in, ffn1_w_out=ffn1_w_out, ffn1_norm_post=ffn1_norm_post,
             mix_norm_pre=mix_norm_pre, w_in=w_in, mlstm_bi=mlstm_bi, mlstm_bf=mlstm_bf, mlstm_norm=mlstm_norm,
             rg_conv_w=rg_conv_w, rg_conv_b=rg_conv_b, rg_wa=rg_wa, rg_ba=rg_ba, rg_wx=rg_wx, rg_bx=rg_bx,
             rg_lambda=rg_lambda, hg_lb_logits=hg_lb_logits, hg_norm=hg_norm,
             w_br_a=w_br_a, w_br_b=w_br_b, w_br_c=w_br_c, w_out=w_out, mix_norm_post=mix_norm_post,
             xa_norm_pre=xa_norm_pre, xa_mem_norm=xa_mem_norm, xa_wq=xa_wq, xa_wk=xa_wk, xa_wv=xa_wv,
             xa_wo=xa_wo, xa_norm_post=xa_norm_post,
             ffn2_norm_pre=ffn2_norm_pre, ffn2_w_in=ffn2_w_in, ffn2_w_out=ffn2_w_out, ffn2_norm_post=ffn2_norm_post)
    P = _prep_params(R)
    B = x_prompt.shape[0]
    zeros_like_state = lambda s: jnp.zeros((1, B) + s.shape[2:], F32)
    init = tuple(zeros_like_state(s) for s in (state_mlstm_C, state_mlstm_n, state_mlstm_m, state_rglru_h,
                                               state_rglru_conv, state_hgrn_S))
    y_prompt, p_states, p_mem_k, p_mem_v = _run_trunk(x_prompt, init, False, mem_prompt, None, None, P)
    n_mem = mem_prompt.shape[1]
    kv_shape = (p_mem_k.shape[0], B, n_mem) + cache_mem_k.shape[-2:]
    s_init = (state_mlstm_C, state_mlstm_n, state_mlstm_m, state_rglru_h, state_rglru_conv, state_hgrn_S)
    Bs = x_sample.shape[0]
    ck = cache_mem_k.reshape(cache_mem_k.shape[0], Bs, n_mem, -1)
    cv = cache_mem_v.reshape(cache_mem_v.shape[0], Bs, n_mem, -1)
    y_sample, s_states, _, _ = _run_trunk(x_sample, s_init, True, None, ck, cv, P)
    return ((y_prompt, y_sample) + p_states + (p_mem_k.reshape(kv_shape), p_mem_v.reshape(kv_shape)) + s_states)
```

```python
import functools

import jax
import jax.numpy as jnp
import numpy as np
from jax import lax
from jax.experimental import pallas as pl
from jax.experimental.pallas import tpu as pltpu

F32 = jnp.float32
BF16 = jnp.bfloat16
EPS = 1e-6

V7X_VMEM_LIMIT_BYTES = 56 * 1024 * 1024
LANES = 128

XA_HEADS = 4
MLSTM_HEADS = 4
HG_HEADS = 4
RG_BLOCKS = 8
RG_C = 8.0
CONV_W = 4
D_FF_TILE = 512
MLSTM_CHUNK = 256
RG_CHUNK = 256
HG_CHUNK = 128
GATE_PAD = LANES


def _params(*sem):
    return pltpu.CompilerParams(dimension_semantics=sem, vmem_limit_bytes=V7X_VMEM_LIMIT_BYTES)


def _rms(x, w):
    return x * lax.rsqrt(jnp.mean(x * x, axis=-1, keepdims=True) + EPS) * w


def _log_sigmoid(x):
    return jnp.minimum(x, 0.0) - jnp.log1p(jnp.exp(-jnp.abs(x)))


def _softplus(x):
    return jnp.maximum(x, 0.0) + jnp.log1p(jnp.exp(-jnp.abs(x)))


def _dot(a, b):
    return jnp.dot(a, b, preferred_element_type=F32)


def _dot_nt(a, b):
    return lax.dot_general(a, b, (((1,), (1,)), ((), ())), preferred_element_type=F32)


def _dot_tn(a, b):
    return lax.dot_general(a, b, (((0,), (0,)), ((), ())), preferred_element_type=F32)


def _shift_rows(x, d, fill, ridx):
    return jnp.where(ridx >= d, pltpu.roll(x, d, 0), fill)


def _ffn_kernel(x_ref, npre_ref, wg_ref, wu_ref, wo_ref, npost_ref, o_ref, hn_ref, acc_ref):
    j = pl.program_id(1)

    @pl.when(j == 0)
    def _():
        hn_ref[...] = _rms(x_ref[...], npre_ref[...]).astype(BF16)
        acc_ref[...] = jnp.zeros_like(acc_ref)

    hn = hn_ref[...]
    g = _dot(hn, wg_ref[...])
    u = _dot(hn, wu_ref[...])
    a = (g * jax.nn.sigmoid(g) * u).astype(BF16)
    acc_ref[...] += _dot(a, wo_ref[...])

    @pl.when(j == pl.num_programs(1) - 1)
    def _():
        o_ref[...] = x_ref[...] + 0.5 * _rms(acc_ref[...], npost_ref[...])


def _ffn(x, npre, w_in, w_out, npost, l, tm):
    M, D = x.shape
    Fp = w_out.shape[1]
    tn = D_FF_TILE
    nj = Fp // tn
    return pl.pallas_call(
        _ffn_kernel,
        grid=(M // tm, nj),
        in_specs=[
            pl.BlockSpec((tm, D), lambda i, j: (i, 0)),
            pl.BlockSpec((None, 1, D), lambda i, j: (l, 0, 0)),
            pl.BlockSpec((None, D, tn), lambda i, j: (l, 0, j)),
            pl.BlockSpec((None, D, tn), lambda i, j: (l, 0, j + nj)),
            pl.BlockSpec((None, tn, D), lambda i, j: (l, j, 0)),
            pl.BlockSpec((None, 1, D), lambda i, j: (l, 0, 0)),
        ],
        out_specs=pl.BlockSpec((tm, D), lambda i, j: (i, 0)),
        out_shape=jax.ShapeDtypeStruct((M, D), F32),
        scratch_shapes=[pltpu.VMEM((tm, D), BF16), pltpu.VMEM((tm, D), F32)],
        compiler_params=_params("parallel", "arbitrary"),
        name="ffn",
    )(x, npre, w_in, w_in, w_out, npost)


def _nmm_kernel(x_ref, nw_ref, w_ref, o_ref, hn_ref):
    @pl.when(pl.program_id(1) == 0)
    def _():
        hn_ref[...] = _rms(x_ref[...], nw_ref[...]).astype(BF16)

    o_ref[...] = _dot(hn_ref[...], w_ref[...]).astype(o_ref.dtype)


def _norm_matmul(x, nw, w, l, out_dtype, tm, tn, name):
    M, D = x.shape
    N = w.shape[-1]
    return pl.pallas_call(
        _nmm_kernel,
        grid=(M // tm, N // tn),
        in_specs=[
            pl.BlockSpec((tm, D), lambda i, j: (i, 0)),
            pl.BlockSpec((None, 1, D), lambda i, j: (l, 0, 0)),
            pl.BlockSpec((None, D, tn), lambda i, j: (l, 0, j)),
        ],
        out_specs=pl.BlockSpec((tm, tn), lambda i, j: (i, j)),
        out_shape=jax.ShapeDtypeStruct((M, N), out_dtype),
        scratch_shapes=[pltpu.VMEM((tm, D), BF16)],
        compiler_params=_params("parallel", "arbitrary"),
        name=name,
    )(x, nw, w)


def _mix_in_kernel(x_ref, nw_ref, w_ref, wg_ref, z_ref, g_ref, hn_ref):
    @pl.when(pl.program_id(1) == 0)
    def _():
        hn = _rms(x_ref[...], nw_ref[...]).astype(BF16)
        hn_ref[...] = hn
        g_ref[...] = _dot(hn, wg_ref[...])

    z_ref[...] = _dot(hn_ref[...], w_ref[...]).astype(z_ref.dtype)


def _mix_in(x, nw, w, wg, l, tm, tn):
    M, D = x.shape
    N = w.shape[-1]
    G = wg.shape[-1]
    return pl.pallas_call(
        _mix_in_kernel,
        grid=(M // tm, N // tn),
        in_specs=[
            pl.BlockSpec((tm, D), lambda i, j: (i, 0)),
            pl.BlockSpec((None, 1, D), lambda i, j: (l, 0, 0)),
            pl.BlockSpec((None, D, tn), lambda i, j: (l, 0, j)),
            pl.BlockSpec((None, D, G), lambda i, j: (l, 0, 0)),
        ],
        out_specs=[pl.BlockSpec((tm, tn), lambda i, j: (i, j)), pl.BlockSpec((tm, G), lambda i, j: (i, 0))],
        out_shape=[jax.ShapeDtypeStruct((M, N), BF16), jax.ShapeDtypeStruct((M, G), F32)],
        scratch_shapes=[pltpu.VMEM((tm, D), BF16)],
        compiler_params=_params("parallel", "arbitrary"),
        name="mix_in",
    )(x, nw, w, wg)


def _mlstm_kernel(q_ref, k_ref, v_ref, og_ref, g_ref, gb_ref, nw_ref, C0_ref, n0_ref, m0_ref,
                  hm_ref, C_ref, n_ref, m_ref, *, L, dh):
    H = MLSTM_HEADS

    @pl.when(pl.program_id(1) == 0)
    def _():
        C_ref[...] = C0_ref[...]
        n_ref[...] = n0_ref[...]
        m_ref[...] = m0_ref[...]

    ga = g_ref[...] + gb_ref[...]
    m_prev = m_ref[0]
    row = lax.broadcasted_iota(jnp.int32, (L, L), 0)
    col = lax.broadcasted_iota(jnp.int32, (L, L), 1)
    causal = row >= col
    eye = row == col

    def as_row(x_col):
        return jnp.sum(jnp.where(eye, x_col, 0.0), axis=0, keepdims=True)

    for h in range(H):
        hs = slice(h * dh, (h + 1) * dh)
        ig_c = ga[:, h:h + 1]
        lf_c = _log_sigmoid(ga[:, H + h:H + h + 1])
        ig_r = as_row(ig_c)
        lf_r = as_row(lf_c)
        b_c = jnp.sum(jnp.where(causal, lf_r, 0.0), axis=1, keepdims=True)
        b_r = jnp.sum(jnp.where(row <= col, lf_c, 0.0), axis=0, keepdims=True)
        m0 = m_prev[:, h:h + 1]
        log_w = jnp.where(causal, b_c - b_r + ig_r, -jnp.inf)
        log_inter = b_c + m0
        m_c = jnp.maximum(log_inter, jnp.max(log_w, axis=1, keepdims=True))
        w = jnp.exp(log_w - m_c)
        w_inter = jnp.exp(log_inter - m_c)

        q = q_ref[:, hs]
        k = k_ref[:, hs] * (dh ** -0.5)
        v = v_ref[:, hs]
        C0 = C_ref[0, h]
        n0 = n_ref[0, h:h + 1, :]
        wqk = w * _dot_nt(q, k)
        num = _dot(wqk.astype(BF16), v) + w_inter * _dot(q, C0.astype(BF16))
        den = (jnp.sum(wqk, axis=1, keepdims=True)
               + w_inter * jnp.sum(q.astype(F32) * n0, axis=1, keepdims=True))
        hh = num / jnp.maximum(jnp.abs(den), jnp.exp(-m_c))
        hn = hh * lax.rsqrt(jnp.mean(hh * hh, axis=-1, keepdims=True) + EPS) * nw_ref[:, hs]
        hm_ref[:, hs] = (jax.nn.sigmoid(og_ref[:, hs].astype(F32)) * hn).astype(BF16)

        m_end = m_c[L - 1:L, :]
        w_end = jnp.exp(b_c[L - 1:L, :] - b_c + ig_c - m_end)
        s_end = w_inter[L - 1:L, :]
        kw = k.astype(F32) * w_end
        C_ref[0, h] = s_end * C0 + _dot_tn(kw.astype(BF16), v)
        n_ref[0, h:h + 1, :] = s_end * n0 + jnp.sum(kw, axis=0, keepdims=True)
        m_ref[0, :, h:h + 1] = m_end


def _mlstm(z, gates, gate_bias, norm_w, C0, n0, m0, l, ls, B, T, col0):
    H = MLSTM_HEADS
    dh = C0.shape[-1]
    W = H * dh
    L = min(MLSTM_CHUNK, T)
    nC = T // L
    M = B * T
    zspec = lambda cb: pl.BlockSpec((L, W), lambda b, c: (b * nC + c, cb))
    return pl.pallas_call(
        functools.partial(_mlstm_kernel, L=L, dh=dh),
        grid=(B, nC),
        in_specs=[
            zspec(col0), zspec(col0 + 1), zspec(col0 + 2), zspec(col0 + 3),
            pl.BlockSpec((L, GATE_PAD), lambda b, c: (b * nC + c, 0)),
            pl.BlockSpec((None, 1, GATE_PAD), lambda b, c: (l, 0, 0)),
            pl.BlockSpec((None, 1, W), lambda b, c: (l, 0, 0)),
            pl.BlockSpec((None, 1, H, dh, dh), lambda b, c: (ls, b, 0, 0, 0)),
            pl.BlockSpec((None, 1, H, dh), lambda b, c: (ls, b, 0, 0)),
            pl.BlockSpec((None, 1, 1, H), lambda b, c: (ls, b, 0, 0)),
        ],
        out_specs=[
            pl.BlockSpec((L, W), lambda b, c: (b * nC + c, 0)),
            pl.BlockSpec((1, H, dh, dh), lambda b, c: (b, 0, 0, 0)),
            pl.BlockSpec((1, H, dh), lambda b, c: (b, 0, 0)),
            pl.BlockSpec((1, 1, H), lambda b, c: (b, 0, 0)),
        ],
        out_shape=[
            jax.ShapeDtypeStruct((M, W), BF16),
            jax.ShapeDtypeStruct((B, H, dh, dh), F32),
            jax.ShapeDtypeStruct((B, H, dh), F32),
            jax.ShapeDtypeStruct((B, 1, H), F32),
        ],
        compiler_params=_params("parallel", "arbitrary"),
        name="mlstm",
    )(z, z, z, z, gates, gate_bias, norm_w, C0, n0, m0)


def _rglru_kernel(rx_ref, rg_ref, cw_ref, cbias_ref, wa_ref, ba_ref, wx_ref, bx_ref, lam_ref, buf0_ref, h0_ref,
                  y_ref, buf_ref, h_ref, cbuf_ref, *, L):
    c = pl.program_id(1)
    W = rx_ref.shape[-1]
    TAIL = CONV_W - 1

    @pl.when(c == 0)
    def _():
        cbuf_ref[0:8, :] = jnp.zeros((8, W), F32)
        cbuf_ref[8 - TAIL:8, :] = buf0_ref[0]
        h_ref[...] = h0_ref[...]

    cbuf_ref[8:8 + L, :] = rx_ref[...].astype(F32)
    xc = cbias_ref[...]
    for j in range(CONV_W):
        xc = xc + cbuf_ref[8 - TAIL + j:8 - TAIL + j + L, :] * cw_ref[j:j + 1, :]

    xcb = xc.astype(BF16)
    r = jax.nn.sigmoid(_dot(xcb, wa_ref[...]) + ba_ref[...])
    i = jax.nn.sigmoid(_dot(xcb, wx_ref[...]) + bx_ref[...])
    log_a = (-RG_C * _softplus(-lam_ref[...])) * r
    a = jnp.exp(log_a)
    th = jnp.tanh(log_a)
    u = jnp.sqrt(-2.0 * th / (1.0 - th)) * (i * xc)

    ridx = lax.broadcasted_iota(jnp.int32, (L, W), 0)
    d = 1
    while d < L:
        a_sh = _shift_rows(a, d, 1.0, ridx)
        u_sh = _shift_rows(u, d, 0.0, ridx)
        u = a * u_sh + u
        a = a * a_sh
        d *= 2
    h = a * h_ref[0] + u
    h_ref[0] = h[L - 1:L, :]
    y_ref[...] = (h * jax.nn.gelu(rg_ref[...].astype(F32))).astype(BF16)

    cbuf_ref[0:8, :] = cbuf_ref[L:L + 8, :]

    @pl.when(c == pl.num_programs(1) - 1)
    def _():
        buf_ref[0] = cbuf_ref[8 + L - TAIL:8 + L, :]


def _rglru(z, P, buf0, h0, l, ls, B, T, col0):
    W = h0.shape[-1]
    L = min(RG_CHUNK, T)
    nC = T // L
    M = B * T
    vec = lambda: pl.BlockSpec((None, 1, W), lambda b, c: (l, 0, 0))
    mat = lambda: pl.BlockSpec((None, W, W), lambda b, c: (l, 0, 0))
    return pl.pallas_call(
        functools.partial(_rglru_kernel, L=L),
        grid=(B, nC),
        in_specs=[
            pl.BlockSpec((L, W), lambda b, c: (b * nC + c, col0)),
            pl.BlockSpec((L, W), lambda b, c: (b * nC + c, col0 + 1)),
            pl.BlockSpec((None, CONV_W, W), lambda b, c: (l, 0, 0)),
            vec(), mat(), vec(), mat(), vec(), vec(),
            pl.BlockSpec((None, 1, CONV_W - 1, W), lambda b, c: (ls, b, 0, 0)),
            pl.BlockSpec((None, 1, 1, W), lambda b, c: (ls, b, 0, 0)),
        ],
        out_specs=[
            pl.BlockSpec((L, W), lambda b, c: (b * nC + c, 0)),
            pl.BlockSpec((1, CONV_W - 1, W), lambda b, c: (b, 0, 0)),
            pl.BlockSpec((1, 1, W), lambda b, c: (b, 0, 0)),
        ],
        out_shape=[
            jax.ShapeDtypeStruct((M, W), BF16),
            jax.ShapeDtypeStruct((B, CONV_W - 1, W), F32),
            jax.ShapeDtypeStruct((B, 1, W), F32),
        ],
        scratch_shapes=[pltpu.VMEM((L + 8, W), F32)],
        compiler_params=_params("parallel", "arbitrary"),
        name="rglru",
    )(z, z, P['rg_conv_w'], P['rg_conv_b'], P['rg_wa'], P['rg_ba'], P['rg_wx'], P['rg_bx'], P['rg_lambda'],
      buf0, h0)


def _block_ref_rows(cb, m, ridx):
    L, W = cb.shape
    n = 2 * m
    if m == 1:
        return jnp.where((ridx & 1) == 0, cb, pltpu.roll(cb, 1, 0))
    if m == 2:
        off = ridx & 3
        return jnp.where(off == 0, pltpu.roll(cb, L - 1, 0),
                         jnp.where(off == 1, cb, jnp.where(off == 2, pltpu.roll(cb, 1, 0), pltpu.roll(cb, 2, 0))))
    return jnp.concatenate(
        [jnp.broadcast_to(cb[j * n + m - 1:j * n + m, :], (n, W)) for j in range(L // n)], axis=0)


def _hgrn_kernel(q_ref, f_ref, i_ref, g_ref, lbl_ref, nw_ref, S0_ref, y_ref, S_ref, *, L, layer):
    H = HG_HEADS
    W = q_ref.shape[-1]
    dk = W // H

    @pl.when(pl.program_id(1) == 0)
    def _():
        S_ref[...] = S0_ref[...]

    lg = lbl_ref[...]
    e = jnp.exp(lg - jnp.max(lg, axis=0, keepdims=True))
    p = e / jnp.sum(e, axis=0, keepdims=True)
    lb = jnp.zeros((1, W), F32)
    for r in range(1, layer + 1):
        lb = lb + p[r:r + 1, :]

    f = lb + (1.0 - lb) * jax.nn.sigmoid(f_ref[...].astype(F32))
    cb = jnp.log(f)
    ridx = lax.broadcasted_iota(jnp.int32, (L, W), 0)
    d = 1
    while d < L:
        cb = cb + _shift_rows(cb, d, 0.0, ridx)
        d *= 2
    kk = 1.0 - f
    q = q_ref[...].astype(F32)
    v = i_ref[...].astype(F32)
    vb = v.astype(BF16)

    row = lax.broadcasted_iota(jnp.int32, (L, L), 0)
    col = lax.broadcasted_iota(jnp.int32, (L, L), 1)
    split = jnp.where(row > col, row ^ col, 0)
    a = [jnp.zeros((L, L), F32) for _ in range(H)]
    m = 1
    while m < L:
        dref = cb - _block_ref_rows(cb, m, ridx)
        qt = (q * jnp.exp(jnp.minimum(dref, 0.0))).astype(BF16)
        ks = (kk * jnp.exp(jnp.minimum(-dref, 0.0))).astype(BF16)
        level = (split // m) == 1
        for h in range(H):
            hs = slice(h * dk, (h + 1) * dk)
            a[h] = jnp.where(level, _dot_nt(qt[:, hs], ks[:, hs]), a[h])
        m *= 2

    cb_end = cb[L - 1:L, :]
    qe = (q * jnp.exp(cb)).astype(BF16)
    ke = (kk * jnp.exp(cb_end - cb)).astype(BF16)
    qk = q * kk
    eye = (lax.broadcasted_iota(jnp.int32, (dk, dk), 0) == lax.broadcasted_iota(jnp.int32, (dk, dk), 1))
    outs = []
    for h in range(H):
        hs = slice(h * dk, (h + 1) * dk)
        S = S_ref[0, h]
        o = (_dot(a[h].astype(BF16), vb[:, hs]) + jnp.sum(qk[:, hs], axis=1, keepdims=True) * v[:, hs]
             + _dot(qe[:, hs], S.astype(BF16)))
        dec_col = jnp.sum(jnp.where(eye, jnp.exp(cb_end[:, hs]), 0.0), axis=1, keepdims=True)
        S_ref[0, h] = dec_col * S + _dot_tn(ke[:, hs], vb[:, hs])
        outs.append(o * lax.rsqrt(jnp.mean(o * o, axis=-1, keepdims=True) + EPS))
    gate = g_ref[...].astype(F32)
    y_ref[...] = (jnp.concatenate(outs, axis=1) * nw_ref[...] * (gate * jax.nn.sigmoid(gate))).astype(BF16)


def _hgrn(z, lb_logits, norm_w, S0, l, ls, B, T, col0):
    H, dk, dv = S0.shape[-3:]
    W = H * dk
    L = min(HG_CHUNK, T)
    nC = T // L
    M = B * T
    depth = lb_logits.shape[0]
    zspec = lambda cb: pl.BlockSpec((L, W), lambda b, c: (b * nC + c, cb))
    return pl.pallas_call(
        functools.partial(_hgrn_kernel, L=L, layer=l),
        grid=(B, nC),
        in_specs=[
            zspec(col0), zspec(col0 + 1), zspec(col0 + 2), zspec(col0 + 3),
            pl.BlockSpec((depth, W), lambda b, c: (0, 0)),
            pl.BlockSpec((None, 1, W), lambda b, c: (l, 0, 0)),
            pl.BlockSpec((None, 1, H, dk, dv), lambda b, c: (ls, b, 0, 0, 0)),
        ],
        out_specs=[
            pl.BlockSpec((L, W), lambda b, c: (b * nC + c, 0)),
            pl.BlockSpec((1, H, dk, dv), lambda b, c: (b, 0, 0, 0)),
        ],
        out_shape=[
            jax.ShapeDtypeStruct((M, W), BF16),
            jax.ShapeDtypeStruct((B, H, dk, dv), F32),
        ],
        compiler_params=_params("parallel", "arbitrary"),
        name="hgrn",
    )(z, z, z, z, lb_logits, norm_w, S0)


def _merge_kernel(a_ref, b_ref, c_ref, ga_ref, gb_ref, gc_ref, wa_ref, wb_ref, wc_ref, wo_ref, npost_ref, x_ref,
                  o_ref):
    sig = lambda r: jax.nn.sigmoid(r[...].astype(F32))
    merged = (sig(ga_ref) * _dot(a_ref[...], wa_ref[...])
              + sig(gb_ref) * _dot(b_ref[...], wb_ref[...])
              + sig(gc_ref) * _dot(c_ref[...], wc_ref[...]))
    out = _dot(merged.astype(BF16), wo_ref[...])
    o_ref[...] = x_ref[...] + _rms(out, npost_ref[...])


def _merge(x, ya, yb, yc, z, P, l, tm):
    M, D = x.shape
    rows = lambda w: pl.BlockSpec((tm, w), lambda i: (i, 0))
    wspec = lambda k: pl.BlockSpec((None, k, D), lambda i: (l, 0, 0), pipeline_mode=pl.Buffered(1))
    return pl.pallas_call(
        _merge_kernel,
        grid=(M // tm,),
        in_specs=[
            rows(ya.shape[1]), rows(yb.shape[1]), rows(yc.shape[1]),
            pl.BlockSpec((tm, D), lambda i: (i, 0)),
            pl.BlockSpec((tm, D), lambda i: (i, 1)),
            pl.BlockSpec((tm, D), lambda i: (i, 2)),
            wspec(ya.shape[1]), wspec(yb.shape[1]), wspec(yc.shape[1]), wspec(D),
            pl.BlockSpec((None, 1, D), lambda i: (l, 0, 0)),
            rows(D),
        ],
        out_specs=rows(D),
        out_shape=jax.ShapeDtypeStruct((M, D), F32),
        compiler_params=_params("parallel"),
        name="merge",
    )(ya, yb, yc, z, z, z, P['w_br_a'], P['w_br_b'], P['w_br_c'], P['w_out'], P['mix_norm_post'], x)


def _xattn_kernel(x_ref, npre_ref, wq_ref, k_ref, v_ref, wo_ref, npost_ref, o_ref, kb_ref, vb_ref):
    @pl.when(pl.program_id(1) == 0)
    def _():
        kb_ref[...] = k_ref[...].astype(BF16)
        vb_ref[...] = v_ref[...].astype(BF16)

    D = x_ref.shape[-1]
    dh = D // XA_HEADS
    x = x_ref[...]
    q = _dot(_rms(x, npre_ref[...]).astype(BF16), wq_ref[...]).astype(BF16)
    outs = []
    for h in range(XA_HEADS):
        hs = slice(h * dh, (h + 1) * dh)
        s = _dot_nt(q[:, hs], kb_ref[:, hs]) * (dh ** -0.5)
        e = jnp.exp(s - jnp.max(s, axis=-1, keepdims=True))
        p = e / jnp.sum(e, axis=-1, keepdims=True)
        outs.append(_dot(p.astype(BF16), vb_ref[:, hs]).astype(BF16))
    out = _dot(jnp.concatenate(outs, axis=1), wo_ref[...])
    o_ref[...] = x + _rms(out, npost_ref[...])


def _xattn(x, mem_k, mem_v, P, l, lk, B, T, tq):
    M, D = x.shape
    n_mem = mem_k.shape[2]
    nT = T // tq
    wspec = lambda: pl.BlockSpec((None, D, D), lambda b, t: (l, 0, 0), pipeline_mode=pl.Buffered(1))
    nspec = lambda: pl.BlockSpec((None, 1, D), lambda b, t: (l, 0, 0))
    kvspec = lambda: pl.BlockSpec((None, None, n_mem, D), lambda b, t: (lk, b, 0, 0))
    return pl.pallas_call(
        _xattn_kernel,
        grid=(B, nT),
        in_specs=[
            pl.BlockSpec((tq, D), lambda b, t: (b * nT + t, 0)),
            nspec(), wspec(), kvspec(), kvspec(), wspec(), nspec(),
        ],
        out_specs=pl.BlockSpec((tq, D), lambda b, t: (b * nT + t, 0)),
        out_shape=jax.ShapeDtypeStruct((M, D), F32),
        scratch_shapes=[pltpu.VMEM((n_mem, D), BF16), pltpu.VMEM((n_mem, D), BF16)],
        compiler_params=_params("parallel", "arbitrary"),
        name="xattn",
    )(x, P['xa_norm_pre'], P['xa_wq'], mem_k, mem_v, P['xa_wo'], P['xa_norm_post'])


def _prep_params(R):
    depth, D = R['ffn1_norm_pre'].shape
    P = {}
    row = lambda a: a.astype(F32).reshape(depth, 1, a.shape[-1])
    for name in ('ffn1_norm_pre', 'ffn1_norm_post', 'mix_norm_pre', 'mix_norm_post', 'xa_norm_pre', 'xa_mem_norm',
                 'xa_norm_post', 'ffn2_norm_pre', 'ffn2_norm_post', 'mlstm_norm', 'rg_conv_b', 'rg_ba', 'rg_bx',
                 'rg_lambda', 'hg_norm'):
        P[name] = row(R[name])
    P['rg_conv_w'] = R['rg_conv_w'].astype(F32)
    P['hg_lb_logits'] = R['hg_lb_logits'].astype(F32)

    for name in ('ffn1', 'ffn2'):
        w_in, w_out = R[name + '_w_in'], R[name + '_w_out']
        F = w_out.shape[1]
        Fp = -(-F // D_FF_TILE) * D_FF_TILE
        padc = lambda a: jnp.pad(a, ((0, 0), (0, 0), (0, Fp - F)))
        P[name + '_w_in'] = jnp.concatenate([padc(w_in[..., :F]), padc(w_in[..., F:])], axis=-1).astype(BF16)
        P[name + '_w_out'] = jnp.pad(w_out, ((0, 0), (0, Fp - F), (0, 0))).astype(BF16)

    mw = R['w_br_a'].shape[1]
    rw = R['w_br_b'].shape[1]
    hw = R['w_br_c'].shape[1]
    nh = R['mlstm_bi'].shape[1]
    splits = (mw, mw, mw, mw, nh, nh, rw, rw, hw, hw, hw, hw, D, D, D)
    pts = [0] + [int(v) for v in np.cumsum(splits)]
    seg = lambda n: R['w_in'][..., pts[n]:pts[n + 1]]
    order = (12, 13, 14, 0, 1, 2, 3, 6, 7, 8, 9, 10, 11)
    P['w_in_main'] = jnp.concatenate([seg(n) for n in order], axis=-1).astype(BF16)
    P['w_in_gate'] = jnp.pad(jnp.concatenate([seg(4), seg(5)], axis=-1),
                             ((0, 0), (0, 0), (0, GATE_PAD - 2 * nh))).astype(BF16)
    P['gate_bias'] = jnp.pad(jnp.concatenate([R['mlstm_bi'], R['mlstm_bf']], axis=-1).astype(F32),
                             ((0, 0), (0, GATE_PAD - 2 * nh))).reshape(depth, 1, GATE_PAD)

    eye = jnp.eye(RG_BLOCKS, dtype=F32)
    dense = lambda w: jnp.einsum('lnde,nm->lndme', w, eye).reshape(depth, rw, rw).astype(BF16)
    P['rg_wa'] = dense(R['rg_wa'])
    P['rg_wx'] = dense(R['rg_wx'])

    for name in ('w_br_a', 'w_br_b', 'w_br_c', 'w_out', 'xa_wq', 'xa_wk', 'xa_wv', 'xa_wo'):
        P[name] = R[name].astype(BF16)
    return P


def _run_trunk(x3, states, state_has_layers, mem, mem_k, mem_v, P):
    B, T, D = x3.shape
    M = B * T
    depth = P['ffn1_norm_pre'].shape[0]
    x = x3.reshape(M, D)
    tm_ffn = min(512, M)
    tm_proj = min(1024, M)
    tm_merge = min(512, M)
    tq = min(512, T)
    C0, n0, m0, hr0, buf0, S0 = states
    m0 = m0.reshape(m0.shape[0], m0.shape[1], 1, m0.shape[2])
    hr0 = hr0.reshape(hr0.shape[0], hr0.shape[1], 1, hr0.shape[2])
    mw = P['w_br_a'].shape[1]
    rw = P['w_br_b'].shape[1]
    hw = P['w_br_c'].shape[1]
    col_m = 3 * D // mw
    col_r = (3 * D + 4 * mw) // rw
    col_h = (3 * D + 4 * mw + 2 * rw) // hw

    if mem is not None:
        n_mem = mem.shape[1]
        mem2 = mem.reshape(B * n_mem, D)
        ks, vs = [], []
        for l in range(depth):
            ks.append(_norm_matmul(mem2, P['xa_mem_norm'], P['xa_wk'], l, F32, B * n_mem, 1024, "mem_k"))
            vs.append(_norm_matmul(mem2, P['xa_mem_norm'], P['xa_wv'], l, F32, B * n_mem, 1024, "mem_v"))
        mem_k = jnp.stack(ks).reshape(depth, B, n_mem, D)
        mem_v = jnp.stack(vs).reshape(depth, B, n_mem, D)

    outs = [[] for _ in range(6)]
    for l in range(depth):
        ls = l if state_has_layers else 0
        x = _ffn(x, P['ffn1_norm_pre'], P['ffn1_w_in'], P['ffn1_w_out'], P['ffn1_norm_post'], l, tm_ffn)
        z, gates = _mix_in(x, P['mix_norm_pre'], P['w_in_main'], P['w_in_gate'], l, tm_proj, 1024)
        ya, C1, n1, m1 = _mlstm(z, gates, P['gate_bias'], P['mlstm_norm'], C0, n0, m0, l, ls, B, T, col_m)
        yb, buf1, hr1 = _rglru(z, P, buf0, hr0, l, ls, B, T, col_r)
        yc, S1 = _hgrn(z, P['hg_lb_logits'], P['hg_norm'], S0, l, ls, B, T, col_h)
        x = _merge(x, ya, yb, yc, z, P, l, tm_merge)
        x = _xattn(x, mem_k, mem_v, P, l, l, B, T, tq)
        x = _ffn(x, P['ffn2_norm_pre'], P['ffn2_w_in'], P['ffn2_w_out'], P['ffn2_norm_post'], l, tm_ffn)
        for lst, s in zip(outs, (C1, n1, m1.reshape(B, -1), hr1.reshape(B, -1), buf1, S1)):
            lst.append(s)
    stacked = tuple(jnp.stack(lst) for lst in outs)
    return x.reshape(B, T, D), stacked, mem_k, mem_v


def kernel(x_prompt, x_sample, mem_prompt, cache_mem_k, cache_mem_v, state_mlstm_C, state_mlstm_n, state_mlstm_m, state_rglru_h, state_rglru_conv, state_hgrn_S, ffn1_norm_pre, ffn1_w_in, ffn1_w_out, ffn1_norm_post, mix_norm_pre, w_in, mlstm_bi, mlstm_bf, mlstm_norm, rg_conv_w, rg_conv_b, rg_wa, rg_ba, rg_wx, rg_bx, rg_lambda, hg_lb_logits, hg_norm, w_br_a, w_br_b, w_br_c, w_out, mix_norm_post, xa_norm_pre, xa_mem_norm, xa_wq, xa_wk, xa_wv, xa_wo, xa_norm_post, ffn2_norm_pre, ffn2_w_in, ffn2_w_out, ffn2_norm_post):
    R = dict(ffn1_norm_pre=ffn1_norm_pre, ffn1_w_in=ffn1_w_in, ffn1_w_out=ffn1_w_out, ffn1_norm_post=ffn1_norm_post,
             mix_norm_pre=mix_norm_pre, w_in=w_in, mlstm_bi=mlstm_bi, mlstm_bf=mlstm_bf, mlstm_norm=mlstm_norm,
             rg_conv_w=rg_conv_w, rg_conv_b=rg_conv_b, rg_wa=rg_wa, rg_ba=rg_ba, rg_wx=rg_wx, rg_bx=rg_bx,
             rg_lambda=rg_lambda, hg_lb_logits=hg_lb_logits, hg_norm=hg_norm,
             w_br_a=w_br_a, w_br_b=w_br_b, w_br_c=w_br_c, w_out=w_out, mix_norm_post=mix_norm_post,
             xa_norm_pre=xa_norm_pre, xa_mem_norm=xa_mem_norm, xa_wq=xa_wq, xa_wk=xa_wk, xa_wv=xa_wv,
             xa_wo=xa_wo, xa_norm_post=xa_norm_post,
             ffn2_norm_pre=ffn2_norm_pre, ffn2_w_in=ffn2_w_in, ffn2_w_out=ffn2_w_out, ffn2_norm_post=ffn2_norm_post)
    P = _prep_params(R)
    B = x_prompt.shape[0]
    zeros_like_state = lambda s: jnp.zeros((1, B) + s.shape[2:], F32)
    init = tuple(zeros_like_state(s) for s in (state_mlstm_C, state_mlstm_n, state_mlstm_m, state_rglru_h,
                                               state_rglru_conv, state_hgrn_S))
    y_prompt, p_states, p_mem_k, p_mem_v = _run_trunk(x_prompt, init, False, mem_prompt, None, None, P)
    n_mem = mem_prompt.shape[1]
    kv_shape = (p_mem_k.shape[0], B, n_mem) + cache_mem_k.shape[-2:]
    s_init = (state_mlstm_C, state_mlstm_n, state_mlstm_m, state_rglru_h, state_rglru_conv, state_hgrn_S)
    Bs = x_sample.shape[0]
    ck = cache_mem_k.reshape(cache_mem_k.shape[0], Bs, n_mem, -1)
    cv = cache_mem_v.reshape(cache_mem_v.shape[0], Bs, n_mem, -1)
    y_sample, s_states, _, _ = _run_trunk(x_sample, s_init, True, None, ck, cv, P)
    return ((y_prompt, y_sample) + p_states + (p_mem_k.reshape(kv_shape), p_mem_v.reshape(kv_shape)) + s_states)
```

```python
import functools

import jax
import jax.numpy as jnp
import numpy as np
from jax import lax
from jax.experimental import pallas as pl
from jax.experimental.pallas import tpu as pltpu

F32 = jnp.float32
BF16 = jnp.bfloat16
EPS = 1e-6

V7X_VMEM_LIMIT_BYTES = 56 * 1024 * 1024
LANES = 128

XA_HEADS = 4
MLSTM_HEADS = 4
HG_HEADS = 4
RG_BLOCKS = 8
RG_C = 8.0
CONV_W = 4
D_FF_TILE = 512
MLSTM_CHUNK = 256
RG_CHUNK = 256
HG_CHUNK = 128
GATE_PAD = LANES


def _params(*sem):
    return pltpu.CompilerParams(dimension_semantics=sem, vmem_limit_bytes=V7X_VMEM_LIMIT_BYTES)


def _rms(x, w):
    return x * lax.rsqrt(jnp.mean(x * x, axis=-1, keepdims=True) + EPS) * w


def _log_sigmoid(x):
    return jnp.minimum(x, 0.0) - jnp.log1p(jnp.exp(-jnp.abs(x)))


def _softplus(x):
    return jnp.maximum(x, 0.0) + jnp.log1p(jnp.exp(-jnp.abs(x)))


def _dot(a, b):
    return jnp.dot(a, b, preferred_element_type=F32)


def _dot_nt(a, b):
    return lax.dot_general(a, b, (((1,), (1,)), ((), ())), preferred_element_type=F32)


def _dot_tn(a, b):
    return lax.dot_general(a, b, (((0,), (0,)), ((), ())), preferred_element_type=F32)


def _shift_rows(x, d, fill, ridx):
    return jnp.where(ridx >= d, pltpu.roll(x, d, 0), fill)


def _ffn_kernel(x_ref, npre_ref, wg_ref, wu_ref, wo_ref, npost_ref, o_ref, hn_ref):
    j = pl.program_id(1)

    @pl.when(j == 0)
    def _():
        hn_ref[...] = _rms(x_ref[...], npre_ref[...]).astype(BF16)

    hn = hn_ref[...]
    g = _dot(hn, wg_ref[...])
    u = _dot(hn, wu_ref[...])
    a = (g * jax.nn.sigmoid(g) * u).astype(BF16)
    y = _dot(a, wo_ref[...])

    @pl.when(j == 0)
    def _():
        o_ref[...] = y

    @pl.when(j > 0)
    def _():
        o_ref[...] += y

    @pl.when(j == pl.num_programs(1) - 1)
    def _():
        o_ref[...] = x_ref[...] + 0.5 * _rms(o_ref[...], npost_ref[...])


def _ffn(x, npre, w_in, w_out, npost, l, tm):
    M, D = x.shape
    Fp = w_out.shape[1]
    tn = D_FF_TILE
    nj = Fp // tn
    return pl.pallas_call(
        _ffn_kernel,
        grid=(M // tm, nj),
        in_specs=[
            pl.BlockSpec((tm, D), lambda i, j: (i, 0)),
            pl.BlockSpec((None, 1, D), lambda i, j: (l, 0, 0)),
            pl.BlockSpec((None, D, tn), lambda i, j: (l, 0, j)),
            pl.BlockSpec((None, D, tn), lambda i, j: (l, 0, j + nj)),
            pl.BlockSpec((None, tn, D), lambda i, j: (l, j, 0)),
            pl.BlockSpec((None, 1, D), lambda i, j: (l, 0, 0)),
        ],
        out_specs=pl.BlockSpec((tm, D), lambda i, j: (i, 0)),
        out_shape=jax.ShapeDtypeStruct((M, D), F32),
        scratch_shapes=[pltpu.VMEM((tm, D), BF16)],
        compiler_params=_params("parallel", "arbitrary"),
        name="ffn",
    )(x, npre, w_in, w_in, w_out, npost)


def _nmm_kernel(x_ref, nw_ref, w_ref, o_ref, hn_ref):
    @pl.when(pl.program_id(1) == 0)
    def _():
        hn_ref[...] = _rms(x_ref[...], nw_ref[...]).astype(BF16)

    o_ref[...] = _dot(hn_ref[...], w_ref[...]).astype(o_ref.dtype)


def _norm_matmul(x, nw, w, l, out_dtype, tm, tn, name):
    M, D = x.shape
    N = w.shape[-1]
    return pl.pallas_call(
        _nmm_kernel,
        grid=(M // tm, N // tn),
        in_specs=[
            pl.BlockSpec((tm, D), lambda i, j: (i, 0)),
            pl.BlockSpec((None, 1, D), lambda i, j: (l, 0, 0)),
            pl.BlockSpec((None, D, tn), lambda i, j: (l, 0, j)),
        ],
        out_specs=pl.BlockSpec((tm, tn), lambda i, j: (i, j)),
        out_shape=jax.ShapeDtypeStruct((M, N), out_dtype),
        scratch_shapes=[pltpu.VMEM((tm, D), BF16)],
        compiler_params=_params("parallel", "arbitrary"),
        name=name,
    )(x, nw, w)


def _mix_in_kernel(x_ref, nw_ref, w_ref, wg_ref, z_ref, g_ref, hn_ref):
    @pl.when(pl.program_id(1) == 0)
    def _():
        hn = _rms(x_ref[...], nw_ref[...]).astype(BF16)
        hn_ref[...] = hn
        g_ref[...] = _dot(hn, wg_ref[...])

    z_ref[...] = _dot(hn_ref[...], w_ref[...]).astype(z_ref.dtype)


def _mix_in(x, nw, w, wg, l, tm, tn):
    M, D = x.shape
    N = w.shape[-1]
    G = wg.shape[-1]
    return pl.pallas_call(
        _mix_in_kernel,
        grid=(M // tm, N // tn),
        in_specs=[
            pl.BlockSpec((tm, D), lambda i, j: (i, 0)),
            pl.BlockSpec((None, 1, D), lambda i, j: (l, 0, 0)),
            pl.BlockSpec((None, D, tn), lambda i, j: (l, 0, j)),
            pl.BlockSpec((None, D, G), lambda i, j: (l, 0, 0)),
        ],
        out_specs=[pl.BlockSpec((tm, tn), lambda i, j: (i, j)), pl.BlockSpec((tm, G), lambda i, j: (i, 0))],
        out_shape=[jax.ShapeDtypeStruct((M, N), BF16), jax.ShapeDtypeStruct((M, G), F32)],
        scratch_shapes=[pltpu.VMEM((tm, D), BF16)],
        compiler_params=_params("parallel", "arbitrary"),
        name="mix_in",
    )(x, nw, w, wg)


def _mlstm_kernel(q_ref, k_ref, v_ref, og_ref, g_ref, gb_ref, nw_ref, C0_ref, n0_ref, m0_ref,
                  hm_ref, C_ref, n_ref, m_ref, *, L, dh):
    H = MLSTM_HEADS

    @pl.when(pl.program_id(1) == 0)
    def _():
        C_ref[...] = C0_ref[...]
        n_ref[...] = n0_ref[...]
        m_ref[...] = m0_ref[...]

    ga = g_ref[...] + gb_ref[...]
    m_prev = m_ref[0]
    row = lax.broadcasted_iota(jnp.int32, (L, L), 0)
    col = lax.broadcasted_iota(jnp.int32, (L, L), 1)
    causal = row >= col
    eye = row == col

    def as_row(x_col):
        return jnp.sum(jnp.where(eye, x_col, 0.0), axis=0, keepdims=True)

    for h in range(H):
        hs = slice(h * dh, (h + 1) * dh)
        ig_c = ga[:, h:h + 1]
        lf_c = _log_sigmoid(ga[:, H + h:H + h + 1])
        ig_r = as_row(ig_c)
        lf_r = as_row(lf_c)
        b_c = jnp.sum(jnp.where(causal, lf_r, 0.0), axis=1, keepdims=True)
        b_r = jnp.sum(jnp.where(row <= col, lf_c, 0.0), axis=0, keepdims=True)
        m0 = m_prev[:, h:h + 1]
        log_w = jnp.where(causal, b_c - b_r + ig_r, -jnp.inf)
        log_inter = b_c + m0
        m_c = jnp.maximum(log_inter, jnp.max(log_w, axis=1, keepdims=True))
        w = jnp.exp(log_w - m_c)
        w_inter = jnp.exp(log_inter - m_c)

        q = q_ref[:, hs]
        k = k_ref[:, hs] * (dh ** -0.5)
        v = v_ref[:, hs]
        C0 = C_ref[0, h]
        n0 = n_ref[0, h:h + 1, :]
        wqk = w * _dot_nt(q, k)
        num = _dot(wqk.astype(BF16), v) + w_inter * _dot(q, C0.astype(BF16))
        den = (jnp.sum(wqk, axis=1, keepdims=True)
               + w_inter * jnp.sum(q.astype(F32) * n0, axis=1, keepdims=True))
        hh = num / jnp.maximum(jnp.abs(den), jnp.exp(-m_c))
        hn = hh * lax.rsqrt(jnp.mean(hh * hh, axis=-1, keepdims=True) + EPS) * nw_ref[:, hs]
        hm_ref[:, hs] = (jax.nn.sigmoid(og_ref[:, hs].astype(F32)) * hn).astype(BF16)

        m_end = m_c[L - 1:L, :]
        w_end = jnp.exp(b_c[L - 1:L, :] - b_c + ig_c - m_end)
        s_end = w_inter[L - 1:L, :]
        kw = k.astype(F32) * w_end
        C_ref[0, h] = s_end * C0 + _dot_tn(kw.astype(BF16), v)
        n_ref[0, h:h + 1, :] = s_end * n0 + jnp.sum(kw, axis=0, keepdims=True)
        m_ref[0, :, h:h + 1] = m_end


def _mlstm(z, gates, gate_bias, norm_w, C0, n0, m0, l, ls, B, T, col0):
    H = MLSTM_HEADS
    dh = C0.shape[-1]
    W = H * dh
    L = min(MLSTM_CHUNK, T)
    nC = T // L
    M = B * T
    zspec = lambda cb: pl.BlockSpec((L, W), lambda b, c: (b * nC + c, cb))
    return pl.pallas_call(
        functools.partial(_mlstm_kernel, L=L, dh=dh),
        grid=(B, nC),
        in_specs=[
            zspec(col0), zspec(col0 + 1), zspec(col0 + 2), zspec(col0 + 3),
            pl.BlockSpec((L, GATE_PAD), lambda b, c: (b * nC + c, 0)),
            pl.BlockSpec((None, 1, GATE_PAD), lambda b, c: (l, 0, 0)),
            pl.BlockSpec((None, 1, W), lambda b, c: (l, 0, 0)),
            pl.BlockSpec((None, 1, H, dh, dh), lambda b, c: (ls, b, 0, 0, 0)),
            pl.BlockSpec((None, 1, H, dh), lambda b, c: (ls, b, 0, 0)),
            pl.BlockSpec((None, 1, 1, H), lambda b, c: (ls, b, 0, 0)),
        ],
        out_specs=[
            pl.BlockSpec((L, W), lambda b, c: (b * nC + c, 0)),
            pl.BlockSpec((1, H, dh, dh), lambda b, c: (b, 0, 0, 0)),
            pl.BlockSpec((1, H, dh), lambda b, c: (b, 0, 0)),
            pl.BlockSpec((1, 1, H), lambda b, c: (b, 0, 0)),
        ],
        out_shape=[
            jax.ShapeDtypeStruct((M, W), BF16),
            jax.ShapeDtypeStruct((B, H, dh, dh), F32),
            jax.ShapeDtypeStruct((B, H, dh), F32),
            jax.ShapeDtypeStruct((B, 1, H), F32),
        ],
        compiler_params=_params("parallel", "arbitrary"),
        name="mlstm",
    )(z, z, z, z, gates, gate_bias, norm_w, C0, n0, m0)


def _rglru_kernel(rx_ref, rg_ref, cw_ref, cbias_ref, wa_ref, ba_ref, wx_ref, bx_ref, lam_ref, buf0_ref, h0_ref,
                  y_ref, buf_ref, h_ref, cbuf_ref, *, L):
    c = pl.program_id(1)
    W = rx_ref.shape[-1]
    TAIL = CONV_W - 1

    @pl.when(c == 0)
    def _():
        cbuf_ref[0:8, :] = jnp.zeros((8, W), F32)
        cbuf_ref[8 - TAIL:8, :] = buf0_ref[0]
        h_ref[...] = h0_ref[...]

    cbuf_ref[8:8 + L, :] = rx_ref[...].astype(F32)
    xc = cbias_ref[...]
    for j in range(CONV_W):
        xc = xc + cbuf_ref[8 - TAIL + j:8 - TAIL + j + L, :] * cw_ref[j:j + 1, :]

    xcb = xc.astype(BF16)
    r = jax.nn.sigmoid(_dot(xcb, wa_ref[...]) + ba_ref[...])
    i = jax.nn.sigmoid(_dot(xcb, wx_ref[...]) + bx_ref[...])
    log_a = (-RG_C * _softplus(-lam_ref[...])) * r
    a = jnp.exp(log_a)
    th = jnp.tanh(log_a)
    u = jnp.sqrt(-2.0 * th / (1.0 - th)) * (i * xc)

    ridx = lax.broadcasted_iota(jnp.int32, (L, W), 0)
    d = 1
    while d < L:
        a_sh = _shift_rows(a, d, 1.0, ridx)
        u_sh = _shift_rows(u, d, 0.0, ridx)
        u = a * u_sh + u
        a = a * a_sh
        d *= 2
    h = a * h_ref[0] + u
    h_ref[0] = h[L - 1:L, :]
    y_ref[...] = (h * jax.nn.gelu(rg_ref[...].astype(F32))).astype(BF16)

    cbuf_ref[0:8, :] = cbuf_ref[L:L + 8, :]

    @pl.when(c == pl.num_programs(1) - 1)
    def _():
        buf_ref[0] = cbuf_ref[8 + L - TAIL:8 + L, :]


def _rglru(z, P, buf0, h0, l, ls, B, T, col0):
    W = h0.shape[-1]
    L = min(RG_CHUNK, T)
    nC = T // L
    M = B * T
    vec = lambda: pl.BlockSpec((None, 1, W), lambda b, c: (l, 0, 0))
    mat = lambda: pl.BlockSpec((None, W, W), lambda b, c: (l, 0, 0))
    return pl.pallas_call(
        functools.partial(_rglru_kernel, L=L),
        grid=(B, nC),
        in_specs=[
            pl.BlockSpec((L, W), lambda b, c: (b * nC + c, col0)),
            pl.BlockSpec((L, W), lambda b, c: (b * nC + c, col0 + 1)),
            pl.BlockSpec((None, CONV_W, W), lambda b, c: (l, 0, 0)),
            vec(), mat(), vec(), mat(), vec(), vec(),
            pl.BlockSpec((None, 1, CONV_W - 1, W), lambda b, c: (ls, b, 0, 0)),
            pl.BlockSpec((None, 1, 1, W), lambda b, c: (ls, b, 0, 0)),
        ],
        out_specs=[
            pl.BlockSpec((L, W), lambda b, c: (b * nC + c, 0)),
            pl.BlockSpec((1, CONV_W - 1, W), lambda b, c: (b, 0, 0)),
            pl.BlockSpec((1, 1, W), lambda b, c: (b, 0, 0)),
        ],
        out_shape=[
            jax.ShapeDtypeStruct((M, W), BF16),
            jax.ShapeDtypeStruct((B, CONV_W - 1, W), F32),
            jax.ShapeDtypeStruct((B, 1, W), F32),
        ],
        scratch_shapes=[pltpu.VMEM((L + 8, W), F32)],
        compiler_params=_params("parallel", "arbitrary"),
        name="rglru",
    )(z, z, P['rg_conv_w'], P['rg_conv_b'], P['rg_wa'], P['rg_ba'], P['rg_wx'], P['rg_bx'], P['rg_lambda'],
      buf0, h0)


def _block_ref_rows(cb, m, ridx):
    L, W = cb.shape
    n = 2 * m
    if m == 1:
        return jnp.where((ridx & 1) == 0, cb, pltpu.roll(cb, 1, 0))
    if m == 2:
        off = ridx & 3
        return jnp.where(off == 0, pltpu.roll(cb, L - 1, 0),
                         jnp.where(off == 1, cb, jnp.where(off == 2, pltpu.roll(cb, 1, 0), pltpu.roll(cb, 2, 0))))
    return jnp.concatenate(
        [jnp.broadcast_to(cb[j * n + m - 1:j * n + m, :], (n, W)) for j in range(L // n)], axis=0)


def _hgrn_kernel(q_ref, f_ref, i_ref, g_ref, lbl_ref, nw_ref, S0_ref, y_ref, S_ref, *, L, layer):
    H = HG_HEADS
    W = q_ref.shape[-1]
    dk = W // H

    @pl.when(pl.program_id(1) == 0)
    def _():
        S_ref[...] = S0_ref[...]

    lg = lbl_ref[...]
    e = jnp.exp(lg - jnp.max(lg, axis=0, keepdims=True))
    p = e / jnp.sum(e, axis=0, keepdims=True)
    lb = jnp.zeros((1, W), F32)
    for r in range(1, layer + 1):
        lb = lb + p[r:r + 1, :]

    f = lb + (1.0 - lb) * jax.nn.sigmoid(f_ref[...].astype(F32))
    cb = jnp.log(f)
    ridx = lax.broadcasted_iota(jnp.int32, (L, W), 0)
    d = 1
    while d < L:
        cb = cb + _shift_rows(cb, d, 0.0, ridx)
        d *= 2
    kk = 1.0 - f
    q = q_ref[...].astype(F32)
    v = i_ref[...].astype(F32)
    vb = v.astype(BF16)

    row = lax.broadcasted_iota(jnp.int32, (L, L), 0)
    col = lax.broadcasted_iota(jnp.int32, (L, L), 1)
    split = jnp.where(row > col, row ^ col, 0)
    a = [jnp.zeros((L, L), F32) for _ in range(H)]
    m = 1
    while m < L:
        dref = cb - _block_ref_rows(cb, m, ridx)
        qt = (q * jnp.exp(jnp.minimum(dref, 0.0))).astype(BF16)
        ks = (kk * jnp.exp(jnp.minimum(-dref, 0.0))).astype(BF16)
        level = (split // m) == 1
        for h in range(H):
            hs = slice(h * dk, (h + 1) * dk)
            a[h] = jnp.where(level, _dot_nt(qt[:, hs], ks[:, hs]), a[h])
        m *= 2

    cb_end = cb[L - 1:L, :]
    qe = (q * jnp.exp(cb)).astype(BF16)
    ke = (kk * jnp.exp(cb_end - cb)).astype(BF16)
    qk = q * kk
    eye = (lax.broadcasted_iota(jnp.int32, (dk, dk), 0) == lax.broadcasted_iota(jnp.int32, (dk, dk), 1))
    outs = []
    for h in range(H):
        hs = slice(h * dk, (h + 1) * dk)
        S = S_ref[0, h]
        o = (_dot(a[h].astype(BF16), vb[:, hs]) + jnp.sum(qk[:, hs], axis=1, keepdims=True) * v[:, hs]
             + _dot(qe[:, hs], S.astype(BF16)))
        dec_col = jnp.sum(jnp.where(eye, jnp.exp(cb_end[:, hs]), 0.0), axis=1, keepdims=True)
        S_ref[0, h] = dec_col * S + _dot_tn(ke[:, hs], vb[:, hs])
        outs.append(o * lax.rsqrt(jnp.mean(o * o, axis=-1, keepdims=True) + EPS))
    gate = g_ref[...].astype(F32)
    y_ref[...] = (jnp.concatenate(outs, axis=1) * nw_ref[...] * (gate * jax.nn.sigmoid(gate))).astype(BF16)


def _hgrn(z, lb_logits, norm_w, S0, l, ls, B, T, col0):
    H, dk, dv = S0.shape[-3:]
    W = H * dk
    L = min(HG_CHUNK, T)
    nC = T // L
    M = B * T
    depth = lb_logits.shape[0]
    zspec = lambda cb: pl.BlockSpec((L, W), lambda b, c: (b * nC + c, cb))
    return pl.pallas_call(
        functools.partial(_hgrn_kernel, L=L, layer=l),
        grid=(B, nC),
        in_specs=[
            zspec(col0), zspec(col0 + 1), zspec(col0 + 2), zspec(col0 + 3),
            pl.BlockSpec((depth, W), lambda b, c: (0, 0)),
            pl.BlockSpec((None, 1, W), lambda b, c: (l, 0, 0)),
            pl.BlockSpec((None, 1, H, dk, dv), lambda b, c: (ls, b, 0, 0, 0)),
        ],
        out_specs=[
            pl.BlockSpec((L, W), lambda b, c: (b * nC + c, 0)),
            pl.BlockSpec((1, H, dk, dv), lambda b, c: (b, 0, 0, 0)),
        ],
        out_shape=[
            jax.ShapeDtypeStruct((M, W), BF16),
            jax.ShapeDtypeStruct((B, H, dk, dv), F32),
        ],
        compiler_params=_params("parallel", "arbitrary"),
        name="hgrn",
    )(z, z, z, z, lb_logits, norm_w, S0)


def _merge_kernel(a_ref, b_ref, c_ref, ga_ref, gb_ref, gc_ref, wa_ref, wb_ref, wc_ref, wo_ref, npost_ref, x_ref,
                  o_ref):
    sig = lambda r: jax.nn.sigmoid(r[...].astype(F32))
    merged = (sig(ga_ref) * _dot(a_ref[...], wa_ref[...])
              + sig(gb_ref) * _dot(b_ref[...], wb_ref[...])
              + sig(gc_ref) * _dot(c_ref[...], wc_ref[...]))
    out = _dot(merged.astype(BF16), wo_ref[...])
    o_ref[...] = x_ref[...] + _rms(out, npost_ref[...])


def _merge(x, ya, yb, yc, z, P, l, tm):
    M, D = x.shape
    rows = lambda w: pl.BlockSpec((tm, w), lambda i: (i, 0))
    wspec = lambda k: pl.BlockSpec((None, k, D), lambda i: (l, 0, 0), pipeline_mode=pl.Buffered(1))
    return pl.pallas_call(
        _merge_kernel,
        grid=(M // tm,),
        in_specs=[
            rows(ya.shape[1]), rows(yb.shape[1]), rows(yc.shape[1]),
            pl.BlockSpec((tm, D), lambda i: (i, 0)),
            pl.BlockSpec((tm, D), lambda i: (i, 1)),
            pl.BlockSpec((tm, D), lambda i: (i, 2)),
            wspec(ya.shape[1]), wspec(yb.shape[1]), wspec(yc.shape[1]), wspec(D),
            pl.BlockSpec((None, 1, D), lambda i: (l, 0, 0)),
            rows(D),
        ],
        out_specs=rows(D),
        out_shape=jax.ShapeDtypeStruct((M, D), F32),
        compiler_params=_params("parallel"),
        name="merge",
    )(ya, yb, yc, z, z, z, P['w_br_a'], P['w_br_b'], P['w_br_c'], P['w_out'], P['mix_norm_post'], x)


def _xattn_kernel(x_ref, npre_ref, wq_ref, k_ref, v_ref, wo_ref, npost_ref, o_ref, kb_ref, vb_ref):
    D = x_ref.shape[-1]
    dh = D // XA_HEADS

    @pl.when(pl.program_id(1) == 0)
    def _():
        for h in range(XA_HEADS):
            kb_ref[:, h * dh:(h + 1) * dh] = k_ref[:, h, :].astype(BF16)
            vb_ref[:, h * dh:(h + 1) * dh] = v_ref[:, h, :].astype(BF16)

    x = x_ref[...]
    q = _dot(_rms(x, npre_ref[...]).astype(BF16), wq_ref[...]).astype(BF16)
    outs = []
    for h in range(XA_HEADS):
        hs = slice(h * dh, (h + 1) * dh)
        s = _dot_nt(q[:, hs], kb_ref[:, hs]) * (dh ** -0.5)
        e = jnp.exp(s - jnp.max(s, axis=-1, keepdims=True))
        p = e / jnp.sum(e, axis=-1, keepdims=True)
        outs.append(_dot(p.astype(BF16), vb_ref[:, hs]).astype(BF16))
    out = _dot(jnp.concatenate(outs, axis=1), wo_ref[...])
    o_ref[...] = x + _rms(out, npost_ref[...])


def _xattn(x, mem_k, mem_v, P, l, lk, B, T, tq):
    M, D = x.shape
    n_mem, H, dh = mem_k.shape[2:]
    nT = T // tq
    wspec = lambda: pl.BlockSpec((None, D, D), lambda b, t: (l, 0, 0), pipeline_mode=pl.Buffered(1))
    nspec = lambda: pl.BlockSpec((None, 1, D), lambda b, t: (l, 0, 0))
    kvspec = lambda: pl.BlockSpec((None, None, n_mem, H, dh), lambda b, t: (lk, b, 0, 0, 0))
    return pl.pallas_call(
        _xattn_kernel,
        grid=(B, nT),
        in_specs=[
            pl.BlockSpec((tq, D), lambda b, t: (b * nT + t, 0)),
            nspec(), wspec(), kvspec(), kvspec(), wspec(), nspec(),
        ],
        out_specs=pl.BlockSpec((tq, D), lambda b, t: (b * nT + t, 0)),
        out_shape=jax.ShapeDtypeStruct((M, D), F32),
        scratch_shapes=[pltpu.VMEM((n_mem, D), BF16), pltpu.VMEM((n_mem, D), BF16)],
        compiler_params=_params("parallel", "arbitrary"),
        name="xattn",
    )(x, P['xa_norm_pre'], P['xa_wq'], mem_k, mem_v, P['xa_wo'], P['xa_norm_post'])


def _mem_kv_kernel(m_ref, nw_ref, wk_ref, wv_ref, k_ref, v_ref):
    H, dh = k_ref.shape[-2:]
    hn = _rms(m_ref[...], nw_ref[...]).astype(BF16)
    k = _dot(hn, wk_ref[...])
    v = _dot(hn, wv_ref[...])
    for h in range(H):
        k_ref[:, h, :] = k[:, h * dh:(h + 1) * dh]
        v_ref[:, h, :] = v[:, h * dh:(h + 1) * dh]


def _mem_kv(mem, P, H):
    B, n_mem, D = mem.shape
    depth = P['xa_wk'].shape[0]
    out = jax.ShapeDtypeStruct((depth, B, n_mem, H, D // H), F32)
    ospec = lambda: pl.BlockSpec((None, None, n_mem, H, D // H), lambda l, b: (l, b, 0, 0, 0))
    wspec = lambda: pl.BlockSpec((None, D, D), lambda l, b: (l, 0, 0))
    return pl.pallas_call(
        _mem_kv_kernel,
        grid=(depth, B),
        in_specs=[
            pl.BlockSpec((None, n_mem, D), lambda l, b: (b, 0, 0)),
            pl.BlockSpec((None, 1, D), lambda l, b: (l, 0, 0)),
            wspec(), wspec(),
        ],
        out_specs=[ospec(), ospec()],
        out_shape=[out, out],
        compiler_params=_params("parallel", "arbitrary"),
        name="mem_kv",
    )(mem, P['xa_mem_norm'], P['xa_wk'], P['xa_wv'])


def _ffn_w_in_cast_kernel(x_ref, o_ref, *, F, Fp):
    tr = x_ref.shape[0]
    o_ref[:, 0:F] = x_ref[:, 0:F].astype(BF16)
    o_ref[:, Fp:Fp + F] = x_ref[:, F:2 * F].astype(BF16)
    if Fp > F:
        o_ref[:, F:Fp] = jnp.zeros((tr, Fp - F), BF16)
        o_ref[:, Fp + F:2 * Fp] = jnp.zeros((tr, Fp - F), BF16)


def _ffn_w_in_cast(w_in, F, Fp, tr=256):
    depth, D, _ = w_in.shape
    return pl.pallas_call(
        functools.partial(_ffn_w_in_cast_kernel, F=F, Fp=Fp),
        grid=(depth, D // tr),
        in_specs=[pl.BlockSpec((None, tr, 2 * F), lambda l, i: (l, i, 0))],
        out_specs=pl.BlockSpec((None, tr, 2 * Fp), lambda l, i: (l, i, 0)),
        out_shape=jax.ShapeDtypeStruct((depth, D, 2 * Fp), BF16),
        compiler_params=_params("parallel", "parallel"),
        name="cast_ffn_w_in",
    )(w_in)


def _ffn_w_out_cast_kernel(x_ref, o_ref, *, F, Fp):
    o_ref[0:F, :] = x_ref[...].astype(BF16)
    if Fp > F:
        o_ref[F:Fp, :] = jnp.zeros((Fp - F, o_ref.shape[1]), BF16)


def _ffn_w_out_cast(w_out, F, Fp, tc=256):
    depth, _, D = w_out.shape
    return pl.pallas_call(
        functools.partial(_ffn_w_out_cast_kernel, F=F, Fp=Fp),
        grid=(depth, D // tc),
        in_specs=[pl.BlockSpec((None, F, tc), lambda l, i: (l, 0, i))],
        out_specs=pl.BlockSpec((None, Fp, tc), lambda l, i: (l, 0, i)),
        out_shape=jax.ShapeDtypeStruct((depth, Fp, D), BF16),
        compiler_params=_params("parallel", "parallel"),
        name="cast_ffn_w_out",
    )(w_out)


def _mix_w_in_cast_kernel(x_ref, o_ref, g_ref, *, segs, gate0, ngate):
    for s0, w, d0 in segs:
        o_ref[:, d0:d0 + w] = x_ref[:, s0:s0 + w].astype(BF16)
    lane = lax.broadcasted_iota(jnp.int32, g_ref.shape, 1)
    g_ref[...] = jnp.where(lane < ngate, x_ref[:, gate0:gate0 + GATE_PAD], 0.0).astype(BF16)


def _mix_w_in_cast(w, segs, gate0, ngate, tr=128):
    depth, D, NW = w.shape
    N = sum(s[1] for s in segs)
    return pl.pallas_call(
        functools.partial(_mix_w_in_cast_kernel, segs=segs, gate0=gate0, ngate=ngate),
        grid=(depth, D // tr),
        in_specs=[pl.BlockSpec((None, tr, NW), lambda l, i: (l, i, 0))],
        out_specs=[pl.BlockSpec((None, tr, N), lambda l, i: (l, i, 0)),
                   pl.BlockSpec((None, tr, GATE_PAD), lambda l, i: (l, i, 0))],
        out_shape=[jax.ShapeDtypeStruct((depth, D, N), BF16), jax.ShapeDtypeStruct((depth, D, GATE_PAD), BF16)],
        compiler_params=_params("parallel", "parallel"),
        name="cast_mix_w_in",
    )(w)


def _prep_params(R):
    depth, D = R['ffn1_norm_pre'].shape
    P = {}
    row = lambda a: a.astype(F32).reshape(depth, 1, a.shape[-1])
    for name in ('ffn1_norm_pre', 'ffn1_norm_post', 'mix_norm_pre', 'mix_norm_post', 'xa_norm_pre', 'xa_mem_norm',
                 'xa_norm_post', 'ffn2_norm_pre', 'ffn2_norm_post', 'mlstm_norm', 'rg_conv_b', 'rg_ba', 'rg_bx',
                 'rg_lambda', 'hg_norm'):
        P[name] = row(R[name])
    P['rg_conv_w'] = R['rg_conv_w'].astype(F32)
    P['hg_lb_logits'] = R['hg_lb_logits'].astype(F32)

    for name in ('ffn1', 'ffn2'):
        w_in, w_out = R[name + '_w_in'], R[name + '_w_out']
        F = w_out.shape[1]
        Fp = -(-F // D_FF_TILE) * D_FF_TILE
        P[name + '_w_in'] = _ffn_w_in_cast(w_in, F, Fp)
        P[name + '_w_out'] = _ffn_w_out_cast(w_out, F, Fp)

    mw = R['w_br_a'].shape[1]
    rw = R['w_br_b'].shape[1]
    hw = R['w_br_c'].shape[1]
    nh = R['mlstm_bi'].shape[1]
    splits = (mw, mw, mw, mw, nh, nh, rw, rw, hw, hw, hw, hw, D, D, D)
    pts = [0] + [int(v) for v in np.cumsum(splits)]
    runs = ((12, 15), (0, 4), (6, 12))
    segs, d0 = [], 0
    for a, b in runs:
        segs.append((pts[a], pts[b] - pts[a], d0))
        d0 += pts[b] - pts[a]
    P['w_in_main'], P['w_in_gate'] = _mix_w_in_cast(R['w_in'], tuple(segs), pts[4], 2 * nh)
    P['gate_bias'] = jnp.pad(jnp.concatenate([R['mlstm_bi'], R['mlstm_bf']], axis=-1).astype(F32),
                             ((0, 0), (0, GATE_PAD - 2 * nh))).reshape(depth, 1, GATE_PAD)

    eye = jnp.eye(RG_BLOCKS, dtype=F32)
    dense = lambda w: jnp.einsum('lnde,nm->lndme', w, eye).reshape(depth, rw, rw).astype(BF16)
    P['rg_wa'] = dense(R['rg_wa'])
    P['rg_wx'] = dense(R['rg_wx'])

    for name in ('w_br_a', 'w_br_b', 'w_br_c', 'w_out', 'xa_wq', 'xa_wk', 'xa_wv', 'xa_wo'):
        P[name] = R[name].astype(BF16)
    return P


def _run_trunk(x3, states, state_has_layers, mem, mem_k, mem_v, P):
    B, T, D = x3.shape
    M = B * T
    depth = P['ffn1_norm_pre'].shape[0]
    x = x3.reshape(M, D)
    tm_ffn = min(512, M)
    tm_proj = min(1024, M)
    tm_merge = min(512, M)
    tq = min(512, T)
    C0, n0, m0, hr0, buf0, S0 = states
    m0 = m0.reshape(m0.shape[0], m0.shape[1], 1, m0.shape[2])
    hr0 = hr0.reshape(hr0.shape[0], hr0.shape[1], 1, hr0.shape[2])
    mw = P['w_br_a'].shape[1]
    rw = P['w_br_b'].shape[1]
    hw = P['w_br_c'].shape[1]
    col_m = 3 * D // mw
    col_r = (3 * D + 4 * mw) // rw
    col_h = (3 * D + 4 * mw + 2 * rw) // hw

    if mem is not None:
        mem_k, mem_v = _mem_kv(mem, P, XA_HEADS)

    outs = [[] for _ in range(6)]
    for l in range(depth):
        ls = l if state_has_layers else 0
        x = _ffn(x, P['ffn1_norm_pre'], P['ffn1_w_in'], P['ffn1_w_out'], P['ffn1_norm_post'], l, tm_ffn)
        z, gates = _mix_in(x, P['mix_norm_pre'], P['w_in_main'], P['w_in_gate'], l, tm_proj, 1024)
        ya, C1, n1, m1 = _mlstm(z, gates, P['gate_bias'], P['mlstm_norm'], C0, n0, m0, l, ls, B, T, col_m)
        yb, buf1, hr1 = _rglru(z, P, buf0, hr0, l, ls, B, T, col_r)
        yc, S1 = _hgrn(z, P['hg_lb_logits'], P['hg_norm'], S0, l, ls, B, T, col_h)
        x = _merge(x, ya, yb, yc, z, P, l, tm_merge)
        x = _xattn(x, mem_k, mem_v, P, l, l, B, T, tq)
        x = _ffn(x, P['ffn2_norm_pre'], P['ffn2_w_in'], P['ffn2_w_out'], P['ffn2_norm_post'], l, tm_ffn)
        for lst, s in zip(outs, (C1, n1, m1.reshape(B, -1), hr1.reshape(B, -1), buf1, S1)):
            lst.append(s)
    stacked = tuple(jnp.stack(lst) for lst in outs)
    return x.reshape(B, T, D), stacked, mem_k, mem_v


def kernel(x_prompt, x_sample, mem_prompt, cache_mem_k, cache_mem_v, state_mlstm_C, state_mlstm_n, state_mlstm_m, state_rglru_h, state_rglru_conv, state_hgrn_S, ffn1_norm_pre, ffn1_w_in, ffn1_w_out, ffn1_norm_post, mix_norm_pre, w_in, mlstm_bi, mlstm_bf, mlstm_norm, rg_conv_w, rg_conv_b, rg_wa, rg_ba, rg_wx, rg_bx, rg_lambda, hg_lb_logits, hg_norm, w_br_a, w_br_b, w_br_c, w_out, mix_norm_post, xa_norm_pre, xa_mem_norm, xa_wq, xa_wk, xa_wv, xa_wo, xa_norm_post, ffn2_norm_pre, ffn2_w_in, ffn2_w_out, ffn2_norm_post):
    R = dict(ffn1_norm_pre=ffn1_norm_pre, ffn1_w_in=ffn1_w_in, ffn1_w_out=ffn1_w_out, ffn1_norm_post=ffn1_norm_post,
             mix_norm_pre=mix_norm_pre, w_in=w_in, mlstm_bi=mlstm_bi, mlstm_bf=mlstm_bf, mlstm_norm=mlstm_norm,
             rg_conv_w=rg_conv_w, rg_conv_b=rg_conv_b, rg_wa=rg_wa, rg_ba=rg_ba, rg_wx=rg_wx, rg_bx=rg_bx,
             rg_lambda=rg_lambda, hg_lb_logits=hg_lb_logits, hg_norm=hg_norm,
             w_br_a=w_br_a, w_br_b=w_br_b, w_br_c=w_br_c, w_out=w_out, mix_norm_post=mix_norm_post,
             xa_norm_pre=xa_norm_pre, xa_mem_norm=xa_mem_norm, xa_wq=xa_wq, xa_wk=xa_wk, xa_wv=xa_wv,
             xa_wo=xa_wo, xa_norm_post=xa_norm_post,
             ffn2_norm_pre=ffn2_norm_pre, ffn2_w_in=ffn2_w_in, ffn2_w_out=ffn2_w_out, ffn2_norm_post=ffn2_norm_post)
    P = _prep_params(R)
    B = x_prompt.shape[0]
    zeros_like_state = lambda s: jnp.zeros((1, B) + s.shape[2:], F32)
    init = tuple(zeros_like_state(s) for s in (state_mlstm_C, state_mlstm_n, state_mlstm_m, state_rglru_h,
                                               state_rglru_conv, state_hgrn_S))
    y_prompt, p_states, p_mem_k, p_mem_v = _run_trunk(x_prompt, init, False, mem_prompt, None, None, P)
    s_init = (state_mlstm_C, state_mlstm_n, state_mlstm_m, state_rglru_h, state_rglru_conv, state_hgrn_S)
    y_sample, s_states, _, _ = _run_trunk(x_sample, s_init, True, None, cache_mem_k, cache_mem_v, P)
    return (y_prompt, y_sample) + p_states + (p_mem_k, p_mem_v) + s_states
```

```python
import functools

import jax
import jax.numpy as jnp
import numpy as np
from jax import lax
from jax.experimental import pallas as pl
from jax.experimental.pallas import tpu as pltpu

F32 = jnp.float32
BF16 = jnp.bfloat16
EPS = 1e-6

V7X_VMEM_LIMIT_BYTES = 56 * 1024 * 1024
LANES = 128

XA_HEADS = 4
MLSTM_HEADS = 4
HG_HEADS = 4
RG_BLOCKS = 8
RG_C = 8.0
CONV_W = 4
D_FF_TILE = 512
MLSTM_CHUNK = 256
RG_CHUNK = 256
HG_CHUNK = 128
GATE_PAD = LANES


def _params(*sem):
    return pltpu.CompilerParams(dimension_semantics=sem, vmem_limit_bytes=V7X_VMEM_LIMIT_BYTES)


def _rms(x, w):
    return x * lax.rsqrt(jnp.mean(x * x, axis=-1, keepdims=True) + EPS) * w


def _log_sigmoid(x):
    return jnp.minimum(x, 0.0) - jnp.log1p(jnp.exp(-jnp.abs(x)))


def _softplus(x):
    return jnp.maximum(x, 0.0) + jnp.log1p(jnp.exp(-jnp.abs(x)))


def _dot(a, b):
    return jnp.dot(a, b, preferred_element_type=F32)


def _dot_nt(a, b):
    return lax.dot_general(a, b, (((1,), (1,)), ((), ())), preferred_element_type=F32)


def _dot_tn(a, b):
    return lax.dot_general(a, b, (((0,), (0,)), ((), ())), preferred_element_type=F32)


def _shift_rows(x, d, fill, ridx):
    return jnp.where(ridx >= d, pltpu.roll(x, d, 0), fill)


def _ffn_kernel(x_ref, npre_ref, wg_ref, wu_ref, wo_ref, npost_ref, o_ref, hn_ref, acc_ref):
    j = pl.program_id(1)

    @pl.when(j == 0)
    def _():
        hn_ref[...] = _rms(x_ref[...], npre_ref[...]).astype(BF16)
        acc_ref[...] = jnp.zeros_like(acc_ref)

    hn = hn_ref[...]
    g = _dot(hn, wg_ref[...])
    u = _dot(hn, wu_ref[...])
    a = (g * jax.nn.sigmoid(g) * u).astype(BF16)
    acc_ref[...] += _dot(a, wo_ref[...])

    @pl.when(j == pl.num_programs(1) - 1)
    def _():
        o_ref[...] = x_ref[...] + 0.5 * _rms(acc_ref[...], npost_ref[...])


def _ffn(x, npre, w_in, w_out, npost, l, tm):
    M, D = x.shape
    Fp = w_out.shape[1]
    tn = D_FF_TILE
    nj = Fp // tn
    return pl.pallas_call(
        _ffn_kernel,
        grid=(M // tm, nj),
        in_specs=[
            pl.BlockSpec((tm, D), lambda i, j: (i, 0)),
            pl.BlockSpec((None, 1, D), lambda i, j: (l, 0, 0)),
            pl.BlockSpec((None, D, tn), lambda i, j: (l, 0, j)),
            pl.BlockSpec((None, D, tn), lambda i, j: (l, 0, j + nj)),
            pl.BlockSpec((None, tn, D), lambda i, j: (l, j, 0)),
            pl.BlockSpec((None, 1, D), lambda i, j: (l, 0, 0)),
        ],
        out_specs=pl.BlockSpec((tm, D), lambda i, j: (i, 0)),
        out_shape=jax.ShapeDtypeStruct((M, D), F32),
        scratch_shapes=[pltpu.VMEM((tm, D), BF16), pltpu.VMEM((tm, D), F32)],
        compiler_params=_params("parallel", "arbitrary"),
        name="ffn",
    )(x, npre, w_in, w_in, w_out, npost)


def _nmm_kernel(x_ref, nw_ref, w_ref, o_ref, hn_ref):
    @pl.when(pl.program_id(1) == 0)
    def _():
        hn_ref[...] = _rms(x_ref[...], nw_ref[...]).astype(BF16)

    o_ref[...] = _dot(hn_ref[...], w_ref[...]).astype(o_ref.dtype)


def _norm_matmul(x, nw, w, l, out_dtype, tm, tn, name):
    M, D = x.shape
    N = w.shape[-1]
    return pl.pallas_call(
        _nmm_kernel,
        grid=(M // tm, N // tn),
        in_specs=[
            pl.BlockSpec((tm, D), lambda i, j: (i, 0)),
            pl.BlockSpec((None, 1, D), lambda i, j: (l, 0, 0)),
            pl.BlockSpec((None, D, tn), lambda i, j: (l, 0, j)),
        ],
        out_specs=pl.BlockSpec((tm, tn), lambda i, j: (i, j)),
        out_shape=jax.ShapeDtypeStruct((M, N), out_dtype),
        scratch_shapes=[pltpu.VMEM((tm, D), BF16)],
        compiler_params=_params("parallel", "arbitrary"),
        name=name,
    )(x, nw, w)


def _mix_in_kernel(x_ref, nw_ref, w_ref, wg_ref, z_ref, g_ref, hn_ref):
    @pl.when(pl.program_id(1) == 0)
    def _():
        hn = _rms(x_ref[...], nw_ref[...]).astype(BF16)
        hn_ref[...] = hn
        g_ref[...] = _dot_nt(hn, wg_ref[...])

    z_ref[...] = _dot_nt(hn_ref[...], w_ref[...]).astype(z_ref.dtype)


def _mix_in(x, nw, w, wg, l, tm, tn):
    M, D = x.shape
    N = w.shape[-2]
    G = wg.shape[-2]
    return pl.pallas_call(
        _mix_in_kernel,
        grid=(M // tm, N // tn),
        in_specs=[
            pl.BlockSpec((tm, D), lambda i, j: (i, 0)),
            pl.BlockSpec((None, 1, D), lambda i, j: (l, 0, 0)),
            pl.BlockSpec((None, tn, D), lambda i, j: (l, j, 0)),
            pl.BlockSpec((None, G, D), lambda i, j: (l, 0, 0)),
        ],
        out_specs=[pl.BlockSpec((tm, tn), lambda i, j: (i, j)), pl.BlockSpec((tm, G), lambda i, j: (i, 0))],
        out_shape=[jax.ShapeDtypeStruct((M, N), BF16), jax.ShapeDtypeStruct((M, G), F32)],
        scratch_shapes=[pltpu.VMEM((tm, D), BF16)],
        compiler_params=_params("parallel", "arbitrary"),
        name="mix_in",
    )(x, nw, w, wg)


def _mlstm_kernel(q_ref, k_ref, v_ref, og_ref, g_ref, gb_ref, nw_ref, C0_ref, n0_ref, m0_ref,
                  hm_ref, C_ref, n_ref, m_ref, *, L, dh):
    H = MLSTM_HEADS

    @pl.when(pl.program_id(1) == 0)
    def _():
        C_ref[...] = C0_ref[...]
        n_ref[...] = n0_ref[...]
        m_ref[...] = m0_ref[...]

    ga = g_ref[...] + gb_ref[...]
    m_prev = m_ref[0]
    row = lax.broadcasted_iota(jnp.int32, (L, L), 0)
    col = lax.broadcasted_iota(jnp.int32, (L, L), 1)
    causal = row >= col
    eye = row == col

    def as_row(x_col):
        return jnp.sum(jnp.where(eye, x_col, 0.0), axis=0, keepdims=True)

    for h in range(H):
        hs = slice(h * dh, (h + 1) * dh)
        ig_c = ga[:, h:h + 1]
        lf_c = _log_sigmoid(ga[:, H + h:H + h + 1])
        ig_r = as_row(ig_c)
        lf_r = as_row(lf_c)
        b_c = jnp.sum(jnp.where(causal, lf_r, 0.0), axis=1, keepdims=True)
        b_r = jnp.sum(jnp.where(row <= col, lf_c, 0.0), axis=0, keepdims=True)
        m0 = m_prev[:, h:h + 1]
        log_w = jnp.where(causal, b_c - b_r + ig_r, -jnp.inf)
        log_inter = b_c + m0
        m_c = jnp.maximum(log_inter, jnp.max(log_w, axis=1, keepdims=True))
        w = jnp.exp(log_w - m_c)
        w_inter = jnp.exp(log_inter - m_c)

        q = q_ref[:, hs]
        k = k_ref[:, hs] * (dh ** -0.5)
        v = v_ref[:, hs]
        C0 = C_ref[0, h]
        n0 = n_ref[0, h:h + 1, :]
        wqk = w * _dot_nt(q, k)
        num = _dot(wqk.astype(BF16), v) + w_inter * _dot(q, C0.astype(BF16))
        den = (jnp.sum(wqk, axis=1, keepdims=True)
               + w_inter * jnp.sum(q.astype(F32) * n0, axis=1, keepdims=True))
        hh = num / jnp.maximum(jnp.abs(den), jnp.exp(-m_c))
        hn = hh * lax.rsqrt(jnp.mean(hh * hh, axis=-1, keepdims=True) + EPS) * nw_ref[:, hs]
        hm_ref[:, hs] = (jax.nn.sigmoid(og_ref[:, hs].astype(F32)) * hn).astype(BF16)

        m_end = m_c[L - 1:L, :]
        w_end = jnp.exp(b_c[L - 1:L, :] - b_c + ig_c - m_end)
        s_end = w_inter[L - 1:L, :]
        kw = k.astype(F32) * w_end
        C_ref[0, h] = s_end * C0 + _dot_tn(kw.astype(BF16), v)
        n_ref[0, h:h + 1, :] = s_end * n0 + jnp.sum(kw, axis=0, keepdims=True)
        m_ref[0, :, h:h + 1] = m_end


def _mlstm(z, gates, gate_bias, norm_w, C0, n0, m0, l, ls, B, T, col0):
    H = MLSTM_HEADS
    dh = C0.shape[-1]
    W = H * dh
    L = min(MLSTM_CHUNK, T)
    nC = T // L
    M = B * T
    zspec = lambda cb: pl.BlockSpec((L, W), lambda b, c: (b * nC + c, cb))
    return pl.pallas_call(
        functools.partial(_mlstm_kernel, L=L, dh=dh),
        grid=(B, nC),
        in_specs=[
            zspec(col0), zspec(col0 + 1), zspec(col0 + 2), zspec(col0 + 3),
            pl.BlockSpec((L, GATE_PAD), lambda b, c: (b * nC + c, 0)),
            pl.BlockSpec((None, 1, GATE_PAD), lambda b, c: (l, 0, 0)),
            pl.BlockSpec((None, 1, W), lambda b, c: (l, 0, 0)),
            pl.BlockSpec((None, 1, H, dh, dh), lambda b, c: (ls, b, 0, 0, 0)),
            pl.BlockSpec((None, 1, H, dh), lambda b, c: (ls, b, 0, 0)),
            pl.BlockSpec((None, 1, 1, H), lambda b, c: (ls, b, 0, 0)),
        ],
        out_specs=[
            pl.BlockSpec((L, W), lambda b, c: (b * nC + c, 0)),
            pl.BlockSpec((1, H, dh, dh), lambda b, c: (b, 0, 0, 0)),
            pl.BlockSpec((1, H, dh), lambda b, c: (b, 0, 0)),
            pl.BlockSpec((1, 1, H), lambda b, c: (b, 0, 0)),
        ],
        out_shape=[
            jax.ShapeDtypeStruct((M, W), BF16),
            jax.ShapeDtypeStruct((B, H, dh, dh), F32),
            jax.ShapeDtypeStruct((B, H, dh), F32),
            jax.ShapeDtypeStruct((B, 1, H), F32),
        ],
        compiler_params=_params("parallel", "arbitrary"),
        name="mlstm",
    )(z, z, z, z, gates, gate_bias, norm_w, C0, n0, m0)


def _rglru_kernel(rx_ref, rg_ref, cw_ref, cbias_ref, wa_ref, ba_ref, wx_ref, bx_ref, lam_ref, buf0_ref, h0_ref,
                  y_ref, buf_ref, h_ref, cbuf_ref, *, L):
    c = pl.program_id(1)
    W = rx_ref.shape[-1]
    TAIL = CONV_W - 1

    @pl.when(c == 0)
    def _():
        cbuf_ref[0:8, :] = jnp.zeros((8, W), F32)
        cbuf_ref[8 - TAIL:8, :] = buf0_ref[0]
        h_ref[...] = h0_ref[...]

    cbuf_ref[8:8 + L, :] = rx_ref[...].astype(F32)
    xc = cbias_ref[...]
    for j in range(CONV_W):
        xc = xc + cbuf_ref[8 - TAIL + j:8 - TAIL + j + L, :] * cw_ref[j:j + 1, :]

    xcb = xc.astype(BF16)
    r = jax.nn.sigmoid(_dot(xcb, wa_ref[...]) + ba_ref[...])
    i = jax.nn.sigmoid(_dot(xcb, wx_ref[...]) + bx_ref[...])
    log_a = (-RG_C * _softplus(-lam_ref[...])) * r
    a = jnp.exp(log_a)
    th = jnp.tanh(log_a)
    u = jnp.sqrt(-2.0 * th / (1.0 - th)) * (i * xc)

    ridx = lax.broadcasted_iota(jnp.int32, (L, W), 0)
    d = 1
    while d < L:
        a_sh = _shift_rows(a, d, 1.0, ridx)
        u_sh = _shift_rows(u, d, 0.0, ridx)
        u = a * u_sh + u
        a = a * a_sh
        d *= 2
    h = a * h_ref[0] + u
    h_ref[0] = h[L - 1:L, :]
    y_ref[...] = (h * jax.nn.gelu(rg_ref[...].astype(F32))).astype(BF16)

    cbuf_ref[0:8, :] = cbuf_ref[L:L + 8, :]

    @pl.when(c == pl.num_programs(1) - 1)
    def _():
        buf_ref[0] = cbuf_ref[8 + L - TAIL:8 + L, :]


def _rglru(z, P, buf0, h0, l, ls, B, T, col0):
    W = h0.shape[-1]
    L = min(RG_CHUNK, T)
    nC = T // L
    M = B * T
    vec = lambda: pl.BlockSpec((None, 1, W), lambda b, c: (l, 0, 0))
    mat = lambda: pl.BlockSpec((None, W, W), lambda b, c: (l, 0, 0))
    return pl.pallas_call(
        functools.partial(_rglru_kernel, L=L),
        grid=(B, nC),
        in_specs=[
            pl.BlockSpec((L, W), lambda b, c: (b * nC + c, col0)),
            pl.BlockSpec((L, W), lambda b, c: (b * nC + c, col0 + 1)),
            pl.BlockSpec((None, CONV_W, W), lambda b, c: (l, 0, 0)),
            vec(), mat(), vec(), mat(), vec(), vec(),
            pl.BlockSpec((None, 1, CONV_W - 1, W), lambda b, c: (ls, b, 0, 0)),
            pl.BlockSpec((None, 1, 1, W), lambda b, c: (ls, b, 0, 0)),
        ],
        out_specs=[
            pl.BlockSpec((L, W), lambda b, c: (b * nC + c, 0)),
            pl.BlockSpec((1, CONV_W - 1, W), lambda b, c: (b, 0, 0)),
            pl.BlockSpec((1, 1, W), lambda b, c: (b, 0, 0)),
        ],
        out_shape=[
            jax.ShapeDtypeStruct((M, W), BF16),
            jax.ShapeDtypeStruct((B, CONV_W - 1, W), F32),
            jax.ShapeDtypeStruct((B, 1, W), F32),
        ],
        scratch_shapes=[pltpu.VMEM((L + 8, W), F32)],
        compiler_params=_params("parallel", "arbitrary"),
        name="rglru",
    )(z, z, P['rg_conv_w'], P['rg_conv_b'], P['rg_wa'], P['rg_ba'], P['rg_wx'], P['rg_bx'], P['rg_lambda'],
      buf0, h0)


def _block_ref_rows(cb, m, ridx):
    L, W = cb.shape
    n = 2 * m
    if m == 1:
        return jnp.where((ridx & 1) == 0, cb, pltpu.roll(cb, 1, 0))
    if m == 2:
        off = ridx & 3
        return jnp.where(off == 0, pltpu.roll(cb, L - 1, 0),
                         jnp.where(off == 1, cb, jnp.where(off == 2, pltpu.roll(cb, 1, 0), pltpu.roll(cb, 2, 0))))
    return jnp.concatenate(
        [jnp.broadcast_to(cb[j * n + m - 1:j * n + m, :], (n, W)) for j in range(L // n)], axis=0)


def _hgrn_kernel(q_ref, f_ref, i_ref, g_ref, lbl_ref, nw_ref, S0_ref, y_ref, S_ref, *, L, layer):
    H = HG_HEADS
    W = q_ref.shape[-1]
    dk = W // H

    @pl.when(pl.program_id(1) == 0)
    def _():
        S_ref[...] = S0_ref[...]

    lg = lbl_ref[...]
    e = jnp.exp(lg - jnp.max(lg, axis=0, keepdims=True))
    p = e / jnp.sum(e, axis=0, keepdims=True)
    lb = jnp.zeros((1, W), F32)
    for r in range(1, layer + 1):
        lb = lb + p[r:r + 1, :]

    f = lb + (1.0 - lb) * jax.nn.sigmoid(f_ref[...].astype(F32))
    cb = jnp.log(f)
    ridx = lax.broadcasted_iota(jnp.int32, (L, W), 0)
    d = 1
    while d < L:
        cb = cb + _shift_rows(cb, d, 0.0, ridx)
        d *= 2
    kk = 1.0 - f
    q = q_ref[...].astype(F32)
    v = i_ref[...].astype(F32)
    vb = v.astype(BF16)

    row = lax.broadcasted_iota(jnp.int32, (L, L), 0)
    col = lax.broadcasted_iota(jnp.int32, (L, L), 1)
    split = jnp.where(row > col, row ^ col, 0)
    a = [jnp.zeros((L, L), F32) for _ in range(H)]
    m = 1
    while m < L:
        dref = cb - _block_ref_rows(cb, m, ridx)
        qt = (q * jnp.exp(jnp.minimum(dref, 0.0))).astype(BF16)
        ks = (kk * jnp.exp(jnp.minimum(-dref, 0.0))).astype(BF16)
        level = (split // m) == 1
        for h in range(H):
            hs = slice(h * dk, (h + 1) * dk)
            a[h] = jnp.where(level, _dot_nt(qt[:, hs], ks[:, hs]), a[h])
        m *= 2

    cb_end = cb[L - 1:L, :]
    qe = (q * jnp.exp(cb)).astype(BF16)
    ke = (kk * jnp.exp(cb_end - cb)).astype(BF16)
    qk = q * kk
    eye = (lax.broadcasted_iota(jnp.int32, (dk, dk), 0) == lax.broadcasted_iota(jnp.int32, (dk, dk), 1))
    outs = []
    for h in range(H):
        hs = slice(h * dk, (h + 1) * dk)
        S = S_ref[0, h]
        o = (_dot(a[h].astype(BF16), vb[:, hs]) + jnp.sum(qk[:, hs], axis=1, keepdims=True) * v[:, hs]
             + _dot(qe[:, hs], S.astype(BF16)))
        dec_col = jnp.sum(jnp.where(eye, jnp.exp(cb_end[:, hs]), 0.0), axis=1, keepdims=True)
        S_ref[0, h] = dec_col * S + _dot_tn(ke[:, hs], vb[:, hs])
        outs.append(o * lax.rsqrt(jnp.mean(o * o, axis=-1, keepdims=True) + EPS))
    gate = g_ref[...].astype(F32)
    y_ref[...] = (jnp.concatenate(outs, axis=1) * nw_ref[...] * (gate * jax.nn.sigmoid(gate))).astype(BF16)


def _hgrn(z, lb_logits, norm_w, S0, l, ls, B, T, col0):
    H, dk, dv = S0.shape[-3:]
    W = H * dk
    L = min(HG_CHUNK, T)
    nC = T // L
    M = B * T
    depth = lb_logits.shape[0]
    zspec = lambda cb: pl.BlockSpec((L, W), lambda b, c: (b * nC + c, cb))
    return pl.pallas_call(
        functools.partial(_hgrn_kernel, L=L, layer=l),
        grid=(B, nC),
        in_specs=[
            zspec(col0), zspec(col0 + 1), zspec(col0 + 2), zspec(col0 + 3),
            pl.BlockSpec((depth, W), lambda b, c: (0, 0)),
            pl.BlockSpec((None, 1, W), lambda b, c: (l, 0, 0)),
            pl.BlockSpec((None, 1, H, dk, dv), lambda b, c: (ls, b, 0, 0, 0)),
        ],
        out_specs=[
            pl.BlockSpec((L, W), lambda b, c: (b * nC + c, 0)),
            pl.BlockSpec((1, H, dk, dv), lambda b, c: (b, 0, 0, 0)),
        ],
        out_shape=[
            jax.ShapeDtypeStruct((M, W), BF16),
            jax.ShapeDtypeStruct((B, H, dk, dv), F32),
        ],
        compiler_params=_params("parallel", "arbitrary"),
        name="hgrn",
    )(z, z, z, z, lb_logits, norm_w, S0)


def _merge_kernel(a_ref, b_ref, c_ref, ga_ref, gb_ref, gc_ref, wa_ref, wb_ref, wc_ref, wo_ref, npost_ref, x_ref,
                  o_ref):
    sig = lambda r: jax.nn.sigmoid(r[...].astype(F32))
    merged = (sig(ga_ref) * _dot(a_ref[...], wa_ref[...])
              + sig(gb_ref) * _dot(b_ref[...], wb_ref[...])
              + sig(gc_ref) * _dot(c_ref[...], wc_ref[...]))
    out = _dot(merged.astype(BF16), wo_ref[...])
    o_ref[...] = x_ref[...] + _rms(out, npost_ref[...])


def _merge(x, ya, yb, yc, z, P, l, tm):
    M, D = x.shape
    rows = lambda w: pl.BlockSpec((tm, w), lambda i: (i, 0))
    wspec = lambda k: pl.BlockSpec((None, k, D), lambda i: (l, 0, 0), pipeline_mode=pl.Buffered(1))
    return pl.pallas_call(
        _merge_kernel,
        grid=(M // tm,),
        in_specs=[
            rows(ya.shape[1]), rows(yb.shape[1]), rows(yc.shape[1]),
            pl.BlockSpec((tm, D), lambda i: (i, 0)),
            pl.BlockSpec((tm, D), lambda i: (i, 1)),
            pl.BlockSpec((tm, D), lambda i: (i, 2)),
            wspec(ya.shape[1]), wspec(yb.shape[1]), wspec(yc.shape[1]), wspec(D),
            pl.BlockSpec((None, 1, D), lambda i: (l, 0, 0)),
            rows(D),
        ],
        out_specs=rows(D),
        out_shape=jax.ShapeDtypeStruct((M, D), F32),
        compiler_params=_params("parallel"),
        name="merge",
    )(ya, yb, yc, z, z, z, P['w_br_a'], P['w_br_b'], P['w_br_c'], P['w_out'], P['mix_norm_post'], x)


def _xattn_kernel(x_ref, npre_ref, wq_ref, k_ref, v_ref, wo_ref, npost_ref, o_ref, kb_ref, vb_ref):
    D = x_ref.shape[-1]
    dh = D // XA_HEADS

    @pl.when(pl.program_id(1) == 0)
    def _():
        for h in range(XA_HEADS):
            kb_ref[:, h * dh:(h + 1) * dh] = k_ref[:, h, :].astype(BF16)
            vb_ref[:, h * dh:(h + 1) * dh] = v_ref[:, h, :].astype(BF16)

    x = x_ref[...]
    q = _dot(_rms(x, npre_ref[...]).astype(BF16), wq_ref[...]).astype(BF16)
    outs = []
    for h in range(XA_HEADS):
        hs = slice(h * dh, (h + 1) * dh)
        s = _dot_nt(q[:, hs], kb_ref[:, hs]) * (dh ** -0.5)
        e = jnp.exp(s - jnp.max(s, axis=-1, keepdims=True))
        p = e / jnp.sum(e, axis=-1, keepdims=True)
        outs.append(_dot(p.astype(BF16), vb_ref[:, hs]).astype(BF16))
    out = _dot(jnp.concatenate(outs, axis=1), wo_ref[...])
    o_ref[...] = x + _rms(out, npost_ref[...])


def _xattn(x, mem_k, mem_v, P, l, lk, B, T, tq):
    M, D = x.shape
    n_mem, H, dh = mem_k.shape[2:]
    nT = T // tq
    wspec = lambda: pl.BlockSpec((None, D, D), lambda b, t: (l, 0, 0), pipeline_mode=pl.Buffered(1))
    nspec = lambda: pl.BlockSpec((None, 1, D), lambda b, t: (l, 0, 0))
    kvspec = lambda: pl.BlockSpec((None, None, n_mem, H, dh), lambda b, t: (lk, b, 0, 0, 0))
    return pl.pallas_call(
        _xattn_kernel,
        grid=(B, nT),
        in_specs=[
            pl.BlockSpec((tq, D), lambda b, t: (b * nT + t, 0)),
            nspec(), wspec(), kvspec(), kvspec(), wspec(), nspec(),
        ],
        out_specs=pl.BlockSpec((tq, D), lambda b, t: (b * nT + t, 0)),
        out_shape=jax.ShapeDtypeStruct((M, D), F32),
        scratch_shapes=[pltpu.VMEM((n_mem, D), BF16), pltpu.VMEM((n_mem, D), BF16)],
        compiler_params=_params("parallel", "arbitrary"),
        name="xattn",
    )(x, P['xa_norm_pre'], P['xa_wq'], mem_k, mem_v, P['xa_wo'], P['xa_norm_post'])


def _mem_kv_kernel(m_ref, nw_ref, wk_ref, wv_ref, k_ref, v_ref):
    H, dh = k_ref.shape[-2:]
    hn = _rms(m_ref[...], nw_ref[...]).astype(BF16)
    k = _dot(hn, wk_ref[...])
    v = _dot(hn, wv_ref[...])
    for h in range(H):
        k_ref[:, h, :] = k[:, h * dh:(h + 1) * dh]
        v_ref[:, h, :] = v[:, h * dh:(h + 1) * dh]


def _mem_kv(mem, P, H):
    B, n_mem, D = mem.shape
    depth = P['xa_wk'].shape[0]
    out = jax.ShapeDtypeStruct((depth, B, n_mem, H, D // H), F32)
    ospec = lambda: pl.BlockSpec((None, None, n_mem, H, D // H), lambda l, b: (l, b, 0, 0, 0))
    wspec = lambda: pl.BlockSpec((None, D, D), lambda l, b: (l, 0, 0))
    return pl.pallas_call(
        _mem_kv_kernel,
        grid=(depth, B),
        in_specs=[
            pl.BlockSpec((None, n_mem, D), lambda l, b: (b, 0, 0)),
            pl.BlockSpec((None, 1, D), lambda l, b: (l, 0, 0)),
            wspec(), wspec(),
        ],
        out_specs=[ospec(), ospec()],
        out_shape=[out, out],
        compiler_params=_params("parallel", "arbitrary"),
        name="mem_kv",
    )(mem, P['xa_mem_norm'], P['xa_wk'], P['xa_wv'])


def _ffn_w_in_cast_kernel(x_ref, o_ref, *, F, Fp):
    tr = x_ref.shape[0]
    o_ref[:, 0:F] = x_ref[:, 0:F].astype(BF16)
    o_ref[:, Fp:Fp + F] = x_ref[:, F:2 * F].astype(BF16)
    if Fp > F:
        o_ref[:, F:Fp] = jnp.zeros((tr, Fp - F), BF16)
        o_ref[:, Fp + F:2 * Fp] = jnp.zeros((tr, Fp - F), BF16)


def _ffn_w_in_cast(w_in, F, Fp, tr=256):
    depth, D, _ = w_in.shape
    return pl.pallas_call(
        functools.partial(_ffn_w_in_cast_kernel, F=F, Fp=Fp),
        grid=(depth, D // tr),
        in_specs=[pl.BlockSpec((None, tr, 2 * F), lambda l, i: (l, i, 0))],
        out_specs=pl.BlockSpec((None, tr, 2 * Fp), lambda l, i: (l, i, 0)),
        out_shape=jax.ShapeDtypeStruct((depth, D, 2 * Fp), BF16),
        compiler_params=_params("parallel", "parallel"),
        name="cast_ffn_w_in",
    )(w_in)


def _ffn_w_out_cast_kernel(x_ref, o_ref, *, F, Fp):
    o_ref[0:F, :] = x_ref[...].astype(BF16)
    if Fp > F:
        o_ref[F:Fp, :] = jnp.zeros((Fp - F, o_ref.shape[1]), BF16)


def _ffn_w_out_cast(w_out, F, Fp, tc=256):
    depth, _, D = w_out.shape
    return pl.pallas_call(
        functools.partial(_ffn_w_out_cast_kernel, F=F, Fp=Fp),
        grid=(depth, D // tc),
        in_specs=[pl.BlockSpec((None, F, tc), lambda l, i: (l, 0, i))],
        out_specs=pl.BlockSpec((None, Fp, tc), lambda l, i: (l, 0, i)),
        out_shape=jax.ShapeDtypeStruct((depth, Fp, D), BF16),
        compiler_params=_params("parallel", "parallel"),
        name="cast_ffn_w_out",
    )(w_out)


def _mix_w_in_cast_kernel(x_ref, o_ref, g_ref, *, segs, gate0, ngate):
    for s0, w, d0 in segs:
        o_ref[d0:d0 + w, :] = x_ref[s0:s0 + w, :].astype(BF16)
    r = lax.broadcasted_iota(jnp.int32, g_ref.shape, 0)
    g_ref[...] = jnp.where(r < ngate, x_ref[gate0:gate0 + GATE_PAD, :], 0.0).astype(BF16)


def _mix_w_in_cast(wt, segs, gate0, ngate, tc=256):
    depth, NW, D = wt.shape
    N = sum(s[1] for s in segs)
    return pl.pallas_call(
        functools.partial(_mix_w_in_cast_kernel, segs=segs, gate0=gate0, ngate=ngate),
        grid=(depth, D // tc),
        in_specs=[pl.BlockSpec((None, NW, tc), lambda l, i: (l, 0, i))],
        out_specs=[pl.BlockSpec((None, N, tc), lambda l, i: (l, 0, i)),
                   pl.BlockSpec((None, GATE_PAD, tc), lambda l, i: (l, 0, i))],
        out_shape=[jax.ShapeDtypeStruct((depth, N, D), BF16), jax.ShapeDtypeStruct((depth, GATE_PAD, D), BF16)],
        compiler_params=_params("parallel", "parallel"),
        name="cast_mix_w_in",
    )(wt)


def _prep_params(R):
    depth, D = R['ffn1_norm_pre'].shape
    P = {}
    row = lambda a: a.astype(F32).reshape(depth, 1, a.shape[-1])
    for name in ('ffn1_norm_pre', 'ffn1_norm_post', 'mix_norm_pre', 'mix_norm_post', 'xa_norm_pre', 'xa_mem_norm',
                 'xa_norm_post', 'ffn2_norm_pre', 'ffn2_norm_post', 'mlstm_norm', 'rg_conv_b', 'rg_ba', 'rg_bx',
                 'rg_lambda', 'hg_norm'):
        P[name] = row(R[name])
    P['rg_conv_w'] = R['rg_conv_w'].astype(F32)
    P['hg_lb_logits'] = R['hg_lb_logits'].astype(F32)

    for name in ('ffn1', 'ffn2'):
        w_in, w_out = R[name + '_w_in'], R[name + '_w_out']
        F = w_out.shape[1]
        Fp = -(-F // D_FF_TILE) * D_FF_TILE
        P[name + '_w_in'] = _ffn_w_in_cast(w_in, F, Fp)
        P[name + '_w_out'] = _ffn_w_out_cast(w_out, F, Fp)

    mw = R['w_br_a'].shape[1]
    rw = R['w_br_b'].shape[1]
    hw = R['w_br_c'].shape[1]
    nh = R['mlstm_bi'].shape[1]
    splits = (mw, mw, mw, mw, nh, nh, rw, rw, hw, hw, hw, hw, D, D, D)
    pts = [0] + [int(v) for v in np.cumsum(splits)]
    runs = ((12, 15), (0, 4), (6, 12))
    segs, d0 = [], 0
    for a, b in runs:
        segs.append((pts[a], pts[b] - pts[a], d0))
        d0 += pts[b] - pts[a]
    P['w_in_main'], P['w_in_gate'] = _mix_w_in_cast(jnp.swapaxes(R['w_in'], 1, 2), tuple(segs), pts[4], 2 * nh)
    P['gate_bias'] = jnp.pad(jnp.concatenate([R['mlstm_bi'], R['mlstm_bf']], axis=-1).astype(F32),
                             ((0, 0), (0, GATE_PAD - 2 * nh))).reshape(depth, 1, GATE_PAD)

    eye = jnp.eye(RG_BLOCKS, dtype=F32)
    dense = lambda w: jnp.einsum('lnde,nm->lndme', w, eye).reshape(depth, rw, rw).astype(BF16)
    P['rg_wa'] = dense(R['rg_wa'])
    P['rg_wx'] = dense(R['rg_wx'])

    for name in ('w_br_a', 'w_br_b', 'w_br_c', 'w_out', 'xa_wq', 'xa_wk', 'xa_wv', 'xa_wo'):
        P[name] = R[name].astype(BF16)
    return P


def _run_trunk(x3, states, state_has_layers, mem, mem_k, mem_v, P):
    B, T, D = x3.shape
    M = B * T
    depth = P['ffn1_norm_pre'].shape[0]
    x = x3.reshape(M, D)
    tm_ffn = min(512, M)
    tm_proj = min(1024, M)
    tm_merge = min(512, M)
    tq = min(512, T)
    C0, n0, m0, hr0, buf0, S0 = states
    m0 = m0.reshape(m0.shape[0], m0.shape[1], 1, m0.shape[2])
    hr0 = hr0.reshape(hr0.shape[0], hr0.shape[1], 1, hr0.shape[2])
    mw = P['w_br_a'].shape[1]
    rw = P['w_br_b'].shape[1]
    hw = P['w_br_c'].shape[1]
    col_m = 3 * D // mw
    col_r = (3 * D + 4 * mw) // rw
    col_h = (3 * D + 4 * mw + 2 * rw) // hw

    if mem is not None:
        mem_k, mem_v = _mem_kv(mem, P, XA_HEADS)

    outs = [[] for _ in range(6)]
    for l in range(depth):
        ls = l if state_has_layers else 0
        x = _ffn(x, P['ffn1_norm_pre'], P['ffn1_w_in'], P['ffn1_w_out'], P['ffn1_norm_post'], l, tm_ffn)
        z, gates = _mix_in(x, P['mix_norm_pre'], P['w_in_main'], P['w_in_gate'], l, tm_proj, 1024)
        ya, C1, n1, m1 = _mlstm(z, gates, P['gate_bias'], P['mlstm_norm'], C0, n0, m0, l, ls, B, T, col_m)
        yb, buf1, hr1 = _rglru(z, P, buf0, hr0, l, ls, B, T, col_r)
        yc, S1 = _hgrn(z, P['hg_lb_logits'], P['hg_norm'], S0, l, ls, B, T, col_h)
        x = _merge(x, ya, yb, yc, z, P, l, tm_merge)
        x = _xattn(x, mem_k, mem_v, P, l, l, B, T, tq)
        x = _ffn(x, P['ffn2_norm_pre'], P['ffn2_w_in'], P['ffn2_w_out'], P['ffn2_norm_post'], l, tm_ffn)
        for lst, s in zip(outs, (C1, n1, m1.reshape(B, -1), hr1.reshape(B, -1), buf1, S1)):
            lst.append(s)
    stacked = tuple(jnp.stack(lst) for lst in outs)
    return x.reshape(B, T, D), stacked, mem_k, mem_v


def kernel(x_prompt, x_sample, mem_prompt, cache_mem_k, cache_mem_v, state_mlstm_C, state_mlstm_n, state_mlstm_m, state_rglru_h, state_rglru_conv, state_hgrn_S, ffn1_norm_pre, ffn1_w_in, ffn1_w_out, ffn1_norm_post, mix_norm_pre, w_in, mlstm_bi, mlstm_bf, mlstm_norm, rg_conv_w, rg_conv_b, rg_wa, rg_ba, rg_wx, rg_bx, rg_lambda, hg_lb_logits, hg_norm, w_br_a, w_br_b, w_br_c, w_out, mix_norm_post, xa_norm_pre, xa_mem_norm, xa_wq, xa_wk, xa_wv, xa_wo, xa_norm_post, ffn2_norm_pre, ffn2_w_in, ffn2_w_out, ffn2_norm_post):
    R = dict(ffn1_norm_pre=ffn1_norm_pre, ffn1_w_in=ffn1_w_in, ffn1_w_out=ffn1_w_out, ffn1_norm_post=ffn1_norm_post,
             mix_norm_pre=mix_norm_pre, w_in=w_in, mlstm_bi=mlstm_bi, mlstm_bf=mlstm_bf, mlstm_norm=mlstm_norm,
             rg_conv_w=rg_conv_w, rg_conv_b=rg_conv_b, rg_wa=rg_wa, rg_ba=rg_ba, rg_wx=rg_wx, rg_bx=rg_bx,
             rg_lambda=rg_lambda, hg_lb_logits=hg_lb_logits, hg_norm=hg_norm,
             w_br_a=w_br_a, w_br_b=w_br_b, w_br_c=w_br_c, w_out=w_out, mix_norm_post=mix_norm_post,
             xa_norm_pre=xa_norm_pre, xa_mem_norm=xa_mem_norm, xa_wq=xa_wq, xa_wk=xa_wk, xa_wv=xa_wv,
             xa_wo=xa_wo, xa_norm_post=xa_norm_post,
             ffn2_norm_pre=ffn2_norm_pre, ffn2_w_in=ffn2_w_in, ffn2_w_out=ffn2_w_out, ffn2_norm_post=ffn2_norm_post)
    P = _prep_params(R)
    B = x_prompt.shape[0]
    zeros_like_state = lambda s: jnp.zeros((1, B) + s.shape[2:], F32)
    init = tuple(zeros_like_state(s) for s in (state_mlstm_C, state_mlstm_n, state_mlstm_m, state_rglru_h,
                                               state_rglru_conv, state_hgrn_S))
    y_prompt, p_states, p_mem_k, p_mem_v = _run_trunk(x_prompt, init, False, mem_prompt, None, None, P)
    s_init = (state_mlstm_C, state_mlstm_n, state_mlstm_m, state_rglru_h, state_rglru_conv, state_hgrn_S)
    y_sample, s_states, _, _ = _run_trunk(x_sample, s_init, True, None, cache_mem_k, cache_mem_v, P)
    return (y_prompt, y_sample) + p_states + (p_mem_k, p_mem_v) + s_states
```

```python
import functools

import jax
import jax.numpy as jnp
import numpy as np
from jax import lax
from jax.experimental import pallas as pl
from jax.experimental.pallas import tpu as pltpu

F32 = jnp.float32
BF16 = jnp.bfloat16
EPS = 1e-6

V7X_VMEM_LIMIT_BYTES = 56 * 1024 * 1024
LANES = 128

XA_HEADS = 4
MLSTM_HEADS = 4
HG_HEADS = 4
RG_BLOCKS = 8
RG_C = 8.0
CONV_W = 4
D_FF_TILE = 512
MLSTM_CHUNK = 256
RG_CHUNK = 256
HG_CHUNK = 128
GATE_PAD = LANES


def _params(*sem):
    return pltpu.CompilerParams(dimension_semantics=sem, vmem_limit_bytes=V7X_VMEM_LIMIT_BYTES)


def _rms(x, w):
    return x * lax.rsqrt(jnp.mean(x * x, axis=-1, keepdims=True) + EPS) * w


def _log_sigmoid(x):
    return jnp.minimum(x, 0.0) - jnp.log1p(jnp.exp(-jnp.abs(x)))


def _softplus(x):
    return jnp.maximum(x, 0.0) + jnp.log1p(jnp.exp(-jnp.abs(x)))


def _dot(a, b):
    return jnp.dot(a, b, preferred_element_type=F32)


def _dot_nt(a, b):
    return lax.dot_general(a, b, (((1,), (1,)), ((), ())), preferred_element_type=F32)


def _dot_tn(a, b):
    return lax.dot_general(a, b, (((0,), (0,)), ((), ())), preferred_element_type=F32)


def _shift_rows(x, d, fill, ridx):
    return jnp.where(ridx >= d, pltpu.roll(x, d, 0), fill)


def _ffn_step(j, nj, x_ref, npre_ref, wg, wu, wo, npost_ref, o_ref, hn_ref, acc_ref):
    @pl.when(j == 0)
    def _():
        hn_ref[...] = _rms(x_ref[...], npre_ref[...]).astype(BF16)
        acc_ref[...] = jnp.zeros_like(acc_ref)

    hn = hn_ref[...]
    g = _dot(hn, wg)
    u = _dot(hn, wu)
    a = (g * jax.nn.sigmoid(g) * u).astype(BF16)
    acc_ref[...] += _dot(a, wo)

    @pl.when(j == nj - 1)
    def _():
        o_ref[...] = x_ref[...] + 0.5 * _rms(acc_ref[...], npost_ref[...])


def _ffn_kernel(x_ref, npre_ref, wg_ref, wu_ref, wo_ref, npost_ref, *rest):
    o_ref, hn_ref, acc_ref = rest[-3:]
    _ffn_step(pl.program_id(1), pl.num_programs(1), x_ref, npre_ref, wg_ref[...], wu_ref[...], wo_ref[...],
              npost_ref, o_ref, hn_ref, acc_ref)


def _ffn(x, npre, wg, wu, wo, npost, l, tm, done=None):
    M, D = x.shape
    Fp = wo.shape[0]
    tn = D_FF_TILE
    nj = Fp // tn
    skip = 0 if done is None else 1
    args = [x, npre, wg, wu, wo, npost]
    in_specs = [
        pl.BlockSpec((tm, D), lambda i, j: (i + skip, 0)),
        pl.BlockSpec((None, 1, D), lambda i, j: (l, 0, 0)),
        pl.BlockSpec((D, tn), lambda i, j: (0, j)),
        pl.BlockSpec((D, tn), lambda i, j: (0, j)),
        pl.BlockSpec((tn, D), lambda i, j: (j, 0)),
        pl.BlockSpec((None, 1, D), lambda i, j: (l, 0, 0)),
    ]
    aliases = {}
    if done is not None:
        args.append(done)
        in_specs.append(pl.BlockSpec(memory_space=pl.ANY))
        aliases = {len(args) - 1: 0}
    return pl.pallas_call(
        _ffn_kernel,
        grid=(M // tm - skip, nj),
        in_specs=in_specs,
        out_specs=pl.BlockSpec((tm, D), lambda i, j: (i + skip, 0)),
        out_shape=jax.ShapeDtypeStruct((M, D), F32),
        scratch_shapes=[pltpu.VMEM((tm, D), BF16), pltpu.VMEM((tm, D), F32)],
        input_output_aliases=aliases,
        compiler_params=_params("parallel", "arbitrary"),
        name="ffn",
    )(*args)


def _ffn_first_kernel(x_ref, npre_ref, g32_ref, ua32_ref, ub32_ref, wo32_ref, npost_ref,
                      o_ref, wg_ref, wu_ref, wo_ref, hn_ref, acc_ref, *, F, shift):
    j = pl.program_id(0)
    tn = wg_ref.shape[-1]
    col_ok = lax.broadcasted_iota(jnp.int32, (1, tn), 1) + j * tn < F
    row_ok = lax.broadcasted_iota(jnp.int32, (tn, 1), 0) + j * tn < F
    if shift:
        u32 = jnp.concatenate([ua32_ref[:, shift:], ub32_ref[:, :shift]], axis=1)
    else:
        u32 = ua32_ref[...]
    wg = jnp.where(col_ok, g32_ref[...], 0.0).astype(BF16)
    wu = jnp.where(col_ok, u32, 0.0).astype(BF16)
    wo = jnp.where(row_ok, wo32_ref[...], 0.0).astype(BF16)
    wg_ref[...] = wg
    wu_ref[...] = wu
    wo_ref[...] = wo
    _ffn_step(j, pl.num_programs(0), x_ref, npre_ref, wg, wu, wo, npost_ref, o_ref, hn_ref, acc_ref)


def _ffn_first(x, npre, w_in, w_out, npost, l, tm, tn=256):
    M, D = x.shape
    F = w_out.shape[1]
    Fp = -(-F // D_FF_TILE) * D_FF_TILE
    nj = Fp // tn
    q, shift = divmod(F, tn)
    last_in = -(-2 * F // tn) - 1
    return pl.pallas_call(
        functools.partial(_ffn_first_kernel, F=F, shift=shift),
        grid=(nj,),
        in_specs=[
            pl.BlockSpec((tm, D), lambda j: (0, 0)),
            pl.BlockSpec((None, 1, D), lambda j: (l, 0, 0)),
            pl.BlockSpec((None, D, tn), lambda j: (l, 0, j)),
            pl.BlockSpec((None, D, tn), lambda j: (l, 0, jnp.minimum(q + j, last_in))),
            pl.BlockSpec((None, D, tn), lambda j: (l, 0, jnp.minimum(q + j + 1, last_in))),
            pl.BlockSpec((None, tn, D), lambda j: (l, j, 0)),
            pl.BlockSpec((None, 1, D), lambda j: (l, 0, 0)),
        ],
        out_specs=[
            pl.BlockSpec((tm, D), lambda j: (0, 0)),
            pl.BlockSpec((D, tn), lambda j: (0, j)),
            pl.BlockSpec((D, tn), lambda j: (0, j)),
            pl.BlockSpec((tn, D), lambda j: (j, 0)),
        ],
        out_shape=[
            jax.ShapeDtypeStruct((M, D), F32),
            jax.ShapeDtypeStruct((D, Fp), BF16),
            jax.ShapeDtypeStruct((D, Fp), BF16),
            jax.ShapeDtypeStruct((Fp, D), BF16),
        ],
        scratch_shapes=[pltpu.VMEM((tm, D), BF16), pltpu.VMEM((tm, D), F32)],
        compiler_params=_params("arbitrary"),
        name="ffn_first",
    )(x, npre, w_in, w_in, w_in, w_out, npost)


def _nmm_kernel(x_ref, nw_ref, w_ref, o_ref, hn_ref):
    @pl.when(pl.program_id(1) == 0)
    def _():
        hn_ref[...] = _rms(x_ref[...], nw_ref[...]).astype(BF16)

    o_ref[...] = _dot(hn_ref[...], w_ref[...]).astype(o_ref.dtype)


def _norm_matmul(x, nw, w, l, out_dtype, tm, tn, name):
    M, D = x.shape
    N = w.shape[-1]
    return pl.pallas_call(
        _nmm_kernel,
        grid=(M // tm, N // tn),
        in_specs=[
            pl.BlockSpec((tm, D), lambda i, j: (i, 0)),
            pl.BlockSpec((None, 1, D), lambda i, j: (l, 0, 0)),
            pl.BlockSpec((None, D, tn), lambda i, j: (l, 0, j)),
        ],
        out_specs=pl.BlockSpec((tm, tn), lambda i, j: (i, j)),
        out_shape=jax.ShapeDtypeStruct((M, N), out_dtype),
        scratch_shapes=[pltpu.VMEM((tm, D), BF16)],
        compiler_params=_params("parallel", "arbitrary"),
        name=name,
    )(x, nw, w)


def _mix_in_kernel(x_ref, nw_ref, w_ref, wg_ref, z_ref, g_ref, hn_ref):
    @pl.when(pl.program_id(1) == 0)
    def _():
        hn = _rms(x_ref[...], nw_ref[...]).astype(BF16)
        hn_ref[...] = hn
        g_ref[...] = _dot_nt(hn, wg_ref[...])

    z_ref[...] = _dot_nt(hn_ref[...], w_ref[...]).astype(z_ref.dtype)


def _mix_in(x, nw, w, wg, l, tm, tn):
    M, D = x.shape
    N = w.shape[-2]
    G = wg.shape[-2]
    return pl.pallas_call(
        _mix_in_kernel,
        grid=(M // tm, N // tn),
        in_specs=[
            pl.BlockSpec((tm, D), lambda i, j: (i, 0)),
            pl.BlockSpec((None, 1, D), lambda i, j: (l, 0, 0)),
            pl.BlockSpec((None, tn, D), lambda i, j: (l, j, 0)),
            pl.BlockSpec((None, G, D), lambda i, j: (l, 0, 0)),
        ],
        out_specs=[pl.BlockSpec((tm, tn), lambda i, j: (i, j)), pl.BlockSpec((tm, G), lambda i, j: (i, 0))],
        out_shape=[jax.ShapeDtypeStruct((M, N), BF16), jax.ShapeDtypeStruct((M, G), F32)],
        scratch_shapes=[pltpu.VMEM((tm, D), BF16)],
        compiler_params=_params("parallel", "arbitrary"),
        name="mix_in",
    )(x, nw, w, wg)


def _mlstm_kernel(q_ref, k_ref, v_ref, og_ref, g_ref, gb_ref, nw_ref, C0_ref, n0_ref, m0_ref,
                  hm_ref, C_ref, n_ref, m_ref, *, L, dh):
    H = MLSTM_HEADS

    @pl.when(pl.program_id(1) == 0)
    def _():
        C_ref[...] = C0_ref[...]
        n_ref[...] = n0_ref[...]
        m_ref[...] = m0_ref[...]

    ga = g_ref[...] + gb_ref[...]
    m_prev = m_ref[0]
    row = lax.broadcasted_iota(jnp.int32, (L, L), 0)
    col = lax.broadcasted_iota(jnp.int32, (L, L), 1)
    causal = row >= col
    eye = row == col
    lane_scan = L % LANES == 0
    gaT = ga.T if lane_scan else None
    lane = lax.broadcasted_iota(jnp.int32, (1, L), 1)

    def as_row(x_col):
        return jnp.sum(jnp.where(eye, x_col, 0.0), axis=0, keepdims=True)

    def as_col(x_row):
        return jnp.sum(jnp.where(eye, x_row, 0.0), axis=1, keepdims=True)

    def prefix_row(x_row, x_col, op, reduce, fill):
        if not lane_scan:
            return reduce(jnp.where(row <= col, x_col, fill), axis=0, keepdims=True)
        d = 1
        while d < L:
            x_row = op(x_row, jnp.where(lane >= d, pltpu.roll(x_row, d, 1), fill))
            d *= 2
        return x_row

    for h in range(H):
        hs = slice(h * dh, (h + 1) * dh)
        ig_c = ga[:, h:h + 1]
        lf_c = _log_sigmoid(ga[:, H + h:H + h + 1])
        if lane_scan:
            ig_r = gaT[h:h + 1, :]
            lf_r = _log_sigmoid(gaT[H + h:H + h + 1, :])
        else:
            ig_r = as_row(ig_c)
            lf_r = as_row(lf_c)
        b_r = prefix_row(lf_r, lf_c, jnp.add, jnp.sum, 0.0)
        b_c = jnp.sum(jnp.where(causal, lf_r, 0.0), axis=1, keepdims=True)
        m0 = m_prev[:, h:h + 1]
        src_r = ig_r - b_r
        src_c = ig_c - b_c
        peak_c = jnp.maximum(m0, as_col(prefix_row(src_r, src_c, jnp.maximum, jnp.max, -jnp.inf)))
        m_c = b_c + peak_c
        w = jnp.exp(jnp.where(causal, src_r - peak_c, -jnp.inf))
        w_inter = jnp.exp(m0 - peak_c)

        q = q_ref[:, hs]
        k = k_ref[:, hs] * (dh ** -0.5)
        v = v_ref[:, hs]
        C0 = C_ref[0, h]
        n0 = n_ref[0, h:h + 1, :]
        wqk = w * _dot_nt(q, k)
        num = _dot(wqk.astype(BF16), v) + w_inter * _dot(q, C0.astype(BF16))
        den = (jnp.sum(wqk, axis=1, keepdims=True)
               + w_inter * jnp.sum(q.astype(F32) * n0, axis=1, keepdims=True))
        hh = num / jnp.maximum(jnp.abs(den), jnp.exp(-m_c))
        hn = hh * lax.rsqrt(jnp.mean(hh * hh, axis=-1, keepdims=True) + EPS) * nw_ref[:, hs]
        hm_ref[:, hs] = (jax.nn.sigmoid(og_ref[:, hs].astype(F32)) * hn).astype(BF16)

        m_end = m_c[L - 1:L, :]
        w_end = jnp.exp(b_c[L - 1:L, :] - b_c + ig_c - m_end)
        s_end = w_inter[L - 1:L, :]
        kw = k.astype(F32) * w_end
        C_ref[0, h] = s_end * C0 + _dot_tn(kw.astype(BF16), v)
        n_ref[0, h:h + 1, :] = s_end * n0 + jnp.sum(kw, axis=0, keepdims=True)
        m_ref[0, :, h:h + 1] = m_end


def _mlstm(z, gates, gate_bias, norm_w, C0, n0, m0, l, ls, B, T, col0):
    H = MLSTM_HEADS
    dh = C0.shape[-1]
    W = H * dh
    L = min(MLSTM_CHUNK, T)
    nC = T // L
    M = B * T
    zspec = lambda cb: pl.BlockSpec((L, W), lambda b, c: (b * nC + c, cb))
    return pl.pallas_call(
        functools.partial(_mlstm_kernel, L=L, dh=dh),
        grid=(B, nC),
        in_specs=[
            zspec(col0), zspec(col0 + 1), zspec(col0 + 2), zspec(col0 + 3),
            pl.BlockSpec((L, GATE_PAD), lambda b, c: (b * nC + c, 0)),
            pl.BlockSpec((None, 1, GATE_PAD), lambda b, c: (l, 0, 0)),
            pl.BlockSpec((None, 1, W), lambda b, c: (l, 0, 0)),
            pl.BlockSpec((None, 1, H, dh, dh), lambda b, c: (ls, b, 0, 0, 0)),
            pl.BlockSpec((None, 1, H, dh), lambda b, c: (ls, b, 0, 0)),
            pl.BlockSpec((None, 1, 1, H), lambda b, c: (ls, b, 0, 0)),
        ],
        out_specs=[
            pl.BlockSpec((L, W), lambda b, c: (b * nC + c, 0)),
            pl.BlockSpec((1, H, dh, dh), lambda b, c: (b, 0, 0, 0)),
            pl.BlockSpec((1, H, dh), lambda b, c: (b, 0, 0)),
            pl.BlockSpec((1, 1, H), lambda b, c: (b, 0, 0)),
        ],
        out_shape=[
            jax.ShapeDtypeStruct((M, W), BF16),
            jax.ShapeDtypeStruct((B, H, dh, dh), F32),
            jax.ShapeDtypeStruct((B, H, dh), F32),
            jax.ShapeDtypeStruct((B, 1, H), F32),
        ],
        compiler_params=_params("parallel", "arbitrary"),
        name="mlstm",
    )(z, z, z, z, gates, gate_bias, norm_w, C0, n0, m0)


def _rglru_kernel(rx_ref, rg_ref, cw_ref, cbias_ref, wa_ref, ba_ref, wx_ref, bx_ref, lam_ref, buf0_ref, h0_ref,
                  y_ref, buf_ref, h_ref, cbuf_ref, *, L):
    c = pl.program_id(1)
    W = rx_ref.shape[-1]
    TAIL = CONV_W - 1

    @pl.when(c == 0)
    def _():
        cbuf_ref[0:8, :] = jnp.zeros((8, W), F32)
        cbuf_ref[8 - TAIL:8, :] = buf0_ref[0]
        h_ref[...] = h0_ref[...]

    cbuf_ref[8:8 + L, :] = rx_ref[...].astype(F32)
    xc = cbias_ref[...]
    for j in range(CONV_W):
        xc = xc + cbuf_ref[8 - TAIL + j:8 - TAIL + j + L, :] * cw_ref[j:j + 1, :]

    xcb = xc.astype(BF16)
    r = jax.nn.sigmoid(_dot(xcb, wa_ref[...]) + ba_ref[...])
    i = jax.nn.sigmoid(_dot(xcb, wx_ref[...]) + bx_ref[...])
    log_a = (-RG_C * _softplus(-lam_ref[...])) * r
    a = jnp.exp(log_a)
    th = jnp.tanh(log_a)
    u = jnp.sqrt(-2.0 * th / (1.0 - th)) * (i * xc)

    ridx = lax.broadcasted_iota(jnp.int32, (L, W), 0)
    d = 1
    while d < L:
        a_sh = _shift_rows(a, d, 1.0, ridx)
        u_sh = _shift_rows(u, d, 0.0, ridx)
        u = a * u_sh + u
        a = a * a_sh
        d *= 2
    h = a * h_ref[0] + u
    h_ref[0] = h[L - 1:L, :]
    y_ref[...] = (h * jax.nn.gelu(rg_ref[...].astype(F32))).astype(BF16)

    cbuf_ref[0:8, :] = cbuf_ref[L:L + 8, :]

    @pl.when(c == pl.num_programs(1) - 1)
    def _():
        buf_ref[0] = cbuf_ref[8 + L - TAIL:8 + L, :]


def _rglru(z, P, buf0, h0, l, ls, B, T, col0):
    W = h0.shape[-1]
    L = min(RG_CHUNK, T)
    nC = T // L
    M = B * T
    vec = lambda: pl.BlockSpec((None, 1, W), lambda b, c: (l, 0, 0))
    mat = lambda: pl.BlockSpec((None, W, W), lambda b, c: (l, 0, 0))
    return pl.pallas_call(
        functools.partial(_rglru_kernel, L=L),
        grid=(B, nC),
        in_specs=[
            pl.BlockSpec((L, W), lambda b, c: (b * nC + c, col0)),
            pl.BlockSpec((L, W), lambda b, c: (b * nC + c, col0 + 1)),
            pl.BlockSpec((None, CONV_W, W), lambda b, c: (l, 0, 0)),
            vec(), mat(), vec(), mat(), vec(), vec(),
            pl.BlockSpec((None, 1, CONV_W - 1, W), lambda b, c: (ls, b, 0, 0)),
            pl.BlockSpec((None, 1, 1, W), lambda b, c: (ls, b, 0, 0)),
        ],
        out_specs=[
            pl.BlockSpec((L, W), lambda b, c: (b * nC + c, 0)),
            pl.BlockSpec((1, CONV_W - 1, W), lambda b, c: (b, 0, 0)),
            pl.BlockSpec((1, 1, W), lambda b, c: (b, 0, 0)),
        ],
        out_shape=[
            jax.ShapeDtypeStruct((M, W), BF16),
            jax.ShapeDtypeStruct((B, CONV_W - 1, W), F32),
            jax.ShapeDtypeStruct((B, 1, W), F32),
        ],
        scratch_shapes=[pltpu.VMEM((L + 8, W), F32)],
        compiler_params=_params("parallel", "arbitrary"),
        name="rglru",
    )(z, z, P['rg_conv_w'], P['rg_conv_b'], P['rg_wa'], P['rg_ba'], P['rg_wx'], P['rg_bx'], P['rg_lambda'],
      buf0, h0)


def _block_ref_rows(cb, m, ridx):
    L, W = cb.shape
    n = 2 * m
    if m == 1:
        return jnp.where((ridx & 1) == 0, cb, pltpu.roll(cb, 1, 0))
    if m == 2:
        off = ridx & 3
        return jnp.where(off == 0, pltpu.roll(cb, L - 1, 0),
                         jnp.where(off == 1, cb, jnp.where(off == 2, pltpu.roll(cb, 1, 0), pltpu.roll(cb, 2, 0))))
    return jnp.concatenate(
        [jnp.broadcast_to(cb[j * n + m - 1:j * n + m, :], (n, W)) for j in range(L // n)], axis=0)


def _hgrn_kernel(q_ref, f_ref, i_ref, g_ref, lbl_ref, nw_ref, S0_ref, y_ref, S_ref, *, L, layer):
    H = HG_HEADS
    W = q_ref.shape[-1]
    dk = W // H

    @pl.when(pl.program_id(1) == 0)
    def _():
        S_ref[...] = S0_ref[...]

    lg = lbl_ref[...]
    e = jnp.exp(lg - jnp.max(lg, axis=0, keepdims=True))
    p = e / jnp.sum(e, axis=0, keepdims=True)
    lb = jnp.zeros((1, W), F32)
    for r in range(1, layer + 1):
        lb = lb + p[r:r + 1, :]

    f = lb + (1.0 - lb) * jax.nn.sigmoid(f_ref[...].astype(F32))
    cb = jnp.log(f)
    ridx = lax.broadcasted_iota(jnp.int32, (L, W), 0)
    d = 1
    while d < L:
        cb = cb + _shift_rows(cb, d, 0.0, ridx)
        d *= 2
    kk = 1.0 - f
    q = q_ref[...].astype(F32)
    v = i_ref[...].astype(F32)
    vb = v.astype(BF16)

    row = lax.broadcasted_iota(jnp.int32, (L, L), 0)
    col = lax.broadcasted_iota(jnp.int32, (L, L), 1)
    split = jnp.where(row > col, row ^ col, 0)
    a = [jnp.zeros((L, L), F32) for _ in range(H)]
    m = 1
    while m < L:
        dref = cb - _block_ref_rows(cb, m, ridx)
        qt = (q * jnp.exp(jnp.minimum(dref, 0.0))).astype(BF16)
        ks = (kk * jnp.exp(jnp.minimum(-dref, 0.0))).astype(BF16)
        level = (split // m) == 1
        for h in range(H):
            hs = slice(h * dk, (h + 1) * dk)
            a[h] = jnp.where(level, _dot_nt(qt[:, hs], ks[:, hs]), a[h])
        m *= 2

    cb_end = cb[L - 1:L, :]
    qe = (q * jnp.exp(cb)).astype(BF16)
    ke = (kk * jnp.exp(cb_end - cb)).astype(BF16)
    qk = q * kk
    eye = (lax.broadcasted_iota(jnp.int32, (dk, dk), 0) == lax.broadcasted_iota(jnp.int32, (dk, dk), 1))
    outs = []
    for h in range(H):
        hs = slice(h * dk, (h + 1) * dk)
        S = S_ref[0, h]
        o = (_dot(a[h].astype(BF16), vb[:, hs]) + jnp.sum(qk[:, hs], axis=1, keepdims=True) * v[:, hs]
             + _dot(qe[:, hs], S.astype(BF16)))
        dec_col = jnp.sum(jnp.where(eye, jnp.exp(cb_end[:, hs]), 0.0), axis=1, keepdims=True)
        S_ref[0, h] = dec_col * S + _dot_tn(ke[:, hs], vb[:, hs])
        outs.append(o * lax.rsqrt(jnp.mean(o * o, axis=-1, keepdims=True) + EPS))
    gate = g_ref[...].astype(F32)
    y_ref[...] = (jnp.concatenate(outs, axis=1) * nw_ref[...] * (gate * jax.nn.sigmoid(gate))).astype(BF16)


def _hgrn(z, lb_logits, norm_w, S0, l, ls, B, T, col0):
    H, dk, dv = S0.shape[-3:]
    W = H * dk
    L = min(HG_CHUNK, T)
    nC = T // L
    M = B * T
    depth = lb_logits.shape[0]
    zspec = lambda cb: pl.BlockSpec((L, W), lambda b, c: (b * nC + c, cb))
    return pl.pallas_call(
        functools.partial(_hgrn_kernel, L=L, layer=l),
        grid=(B, nC),
        in_specs=[
            zspec(col0), zspec(col0 + 1), zspec(col0 + 2), zspec(col0 + 3),
            pl.BlockSpec((depth, W), lambda b, c: (0, 0)),
            pl.BlockSpec((None, 1, W), lambda b, c: (l, 0, 0)),
            pl.BlockSpec((None, 1, H, dk, dv), lambda b, c: (ls, b, 0, 0, 0)),
        ],
        out_specs=[
            pl.BlockSpec((L, W), lambda b, c: (b * nC + c, 0)),
            pl.BlockSpec((1, H, dk, dv), lambda b, c: (b, 0, 0, 0)),
        ],
        out_shape=[
            jax.ShapeDtypeStruct((M, W), BF16),
            jax.ShapeDtypeStruct((B, H, dk, dv), F32),
        ],
        compiler_params=_params("parallel", "arbitrary"),
        name="hgrn",
    )(z, z, z, z, lb_logits, norm_w, S0)


def _merge_kernel(a_ref, b_ref, c_ref, ga_ref, gb_ref, gc_ref, wa_ref, wb_ref, wc_ref, wo_ref, npost_ref, x_ref,
                  o_ref):
    sig = lambda r: jax.nn.sigmoid(r[...].astype(F32))
    merged = (sig(ga_ref) * _dot(a_ref[...], wa_ref[...])
              + sig(gb_ref) * _dot(b_ref[...], wb_ref[...])
              + sig(gc_ref) * _dot(c_ref[...], wc_ref[...]))
    out = _dot(merged.astype(BF16), wo_ref[...])
    o_ref[...] = x_ref[...] + _rms(out, npost_ref[...])


def _merge(x, ya, yb, yc, z, P, l, tm):
    M, D = x.shape
    rows = lambda w: pl.BlockSpec((tm, w), lambda i: (i, 0))
    wspec = lambda k: pl.BlockSpec((None, k, D), lambda i: (l, 0, 0), pipeline_mode=pl.Buffered(1))
    return pl.pallas_call(
        _merge_kernel,
        grid=(M // tm,),
        in_specs=[
            rows(ya.shape[1]), rows(yb.shape[1]), rows(yc.shape[1]),
            pl.BlockSpec((tm, D), lambda i: (i, 0)),
            pl.BlockSpec((tm, D), lambda i: (i, 1)),
            pl.BlockSpec((tm, D), lambda i: (i, 2)),
            wspec(ya.shape[1]), wspec(yb.shape[1]), wspec(yc.shape[1]), wspec(D),
            pl.BlockSpec((None, 1, D), lambda i: (l, 0, 0)),
            rows(D),
        ],
        out_specs=rows(D),
        out_shape=jax.ShapeDtypeStruct((M, D), F32),
        compiler_params=_params("parallel"),
        name="merge",
    )(ya, yb, yc, z, z, z, P['w_br_a'], P['w_br_b'], P['w_br_c'], P['w_out'], P['mix_norm_post'], x)


def _xattn_kernel(x_ref, npre_ref, wq_ref, k_ref, v_ref, wo_ref, npost_ref, o_ref, kb_ref, vb_ref):
    D = x_ref.shape[-1]
    dh = D // XA_HEADS

    @pl.when(pl.program_id(1) == 0)
    def _():
        for h in range(XA_HEADS):
            kb_ref[:, h * dh:(h + 1) * dh] = k_ref[:, h, :].astype(BF16)
            vb_ref[:, h * dh:(h + 1) * dh] = v_ref[:, h, :].astype(BF16)

    x = x_ref[...]
    q = _dot(_rms(x, npre_ref[...]).astype(BF16), wq_ref[...]).astype(BF16)
    outs = []
    for h in range(XA_HEADS):
        hs = slice(h * dh, (h + 1) * dh)
        s = _dot_nt(q[:, hs], kb_ref[:, hs]) * (dh ** -0.5)
        e = jnp.exp(s - jnp.max(s, axis=-1, keepdims=True))
        p = e / jnp.sum(e, axis=-1, keepdims=True)
        outs.append(_dot(p.astype(BF16), vb_ref[:, hs]).astype(BF16))
    out = _dot(jnp.concatenate(outs, axis=1), wo_ref[...])
    o_ref[...] = x + _rms(out, npost_ref[...])


def _xattn(x, mem_k, mem_v, P, l, lk, B, T, tq):
    M, D = x.shape
    n_mem, H, dh = mem_k.shape[2:]
    nT = T // tq
    wspec = lambda: pl.BlockSpec((None, D, D), lambda b, t: (l, 0, 0), pipeline_mode=pl.Buffered(1))
    nspec = lambda: pl.BlockSpec((None, 1, D), lambda b, t: (l, 0, 0))
    kvspec = lambda: pl.BlockSpec((None, None, n_mem, H, dh), lambda b, t: (lk, b, 0, 0, 0))
    return pl.pallas_call(
        _xattn_kernel,
        grid=(B, nT),
        in_specs=[
            pl.BlockSpec((tq, D), lambda b, t: (b * nT + t, 0)),
            nspec(), wspec(), kvspec(), kvspec(), wspec(), nspec(),
        ],
        out_specs=pl.BlockSpec((tq, D), lambda b, t: (b * nT + t, 0)),
        out_shape=jax.ShapeDtypeStruct((M, D), F32),
        scratch_shapes=[pltpu.VMEM((n_mem, D), BF16), pltpu.VMEM((n_mem, D), BF16)],
        compiler_params=_params("parallel", "arbitrary"),
        name="xattn",
    )(x, P['xa_norm_pre'], P['xa_wq'], mem_k, mem_v, P['xa_wo'], P['xa_norm_post'])


def _mem_kv_kernel(m_ref, nw_ref, wk_ref, wv_ref, k_ref, v_ref):
    H, dh = k_ref.shape[-2:]
    hn = _rms(m_ref[...], nw_ref[...]).astype(BF16)
    k = _dot(hn, wk_ref[...])
    v = _dot(hn, wv_ref[...])
    for h in range(H):
        k_ref[:, h, :] = k[:, h * dh:(h + 1) * dh]
        v_ref[:, h, :] = v[:, h * dh:(h + 1) * dh]


def _mem_kv(mem, P, H):
    B, n_mem, D = mem.shape
    depth = P['xa_wk'].shape[0]
    out = jax.ShapeDtypeStruct((depth, B, n_mem, H, D // H), F32)
    ospec = lambda: pl.BlockSpec((None, None, n_mem, H, D // H), lambda l, b: (l, b, 0, 0, 0))
    wspec = lambda: pl.BlockSpec((None, D, D), lambda l, b: (l, 0, 0))
    return pl.pallas_call(
        _mem_kv_kernel,
        grid=(depth, B),
        in_specs=[
            pl.BlockSpec((None, n_mem, D), lambda l, b: (b, 0, 0)),
            pl.BlockSpec((None, 1, D), lambda l, b: (l, 0, 0)),
            wspec(), wspec(),
        ],
        out_specs=[ospec(), ospec()],
        out_shape=[out, out],
        compiler_params=_params("parallel", "arbitrary"),
        name="mem_kv",
    )(mem, P['xa_mem_norm'], P['xa_wk'], P['xa_wv'])


def _mix_w_in_cast_kernel(x_ref, o_ref, g_ref, *, segs, gate0, ngate):
    for s0, w, d0 in segs:
        o_ref[d0:d0 + w, :] = x_ref[s0:s0 + w, :].astype(BF16)
    r = lax.broadcasted_iota(jnp.int32, g_ref.shape, 0)
    g_ref[...] = jnp.where(r < ngate, x_ref[gate0:gate0 + GATE_PAD, :], 0.0).astype(BF16)


def _mix_w_in_cast(wt, segs, gate0, ngate, tc=256):
    depth, NW, D = wt.shape
    N = sum(s[1] for s in segs)
    return pl.pallas_call(
        functools.partial(_mix_w_in_cast_kernel, segs=segs, gate0=gate0, ngate=ngate),
        grid=(depth, D // tc),
        in_specs=[pl.BlockSpec((None, NW, tc), lambda l, i: (l, 0, i))],
        out_specs=[pl.BlockSpec((None, N, tc), lambda l, i: (l, 0, i)),
                   pl.BlockSpec((None, GATE_PAD, tc), lambda l, i: (l, 0, i))],
        out_shape=[jax.ShapeDtypeStruct((depth, N, D), BF16), jax.ShapeDtypeStruct((depth, GATE_PAD, D), BF16)],
        compiler_params=_params("parallel", "parallel"),
        name="cast_mix_w_in",
    )(wt)


def _prep_params(R):
    depth, D = R['ffn1_norm_pre'].shape
    P = {}
    row = lambda a: a.astype(F32).reshape(depth, 1, a.shape[-1])
    for name in ('ffn1_norm_pre', 'ffn1_norm_post', 'mix_norm_pre', 'mix_norm_post', 'xa_norm_pre', 'xa_mem_norm',
                 'xa_norm_post', 'ffn2_norm_pre', 'ffn2_norm_post', 'mlstm_norm', 'rg_conv_b', 'rg_ba', 'rg_bx',
                 'rg_lambda', 'hg_norm'):
        P[name] = row(R[name])
    P['rg_conv_w'] = R['rg_conv_w'].astype(F32)
    P['hg_lb_logits'] = R['hg_lb_logits'].astype(F32)

    for name in ('ffn1_w_in', 'ffn1_w_out', 'ffn2_w_in', 'ffn2_w_out'):
        P[name] = R[name]

    mw = R['w_br_a'].shape[1]
    rw = R['w_br_b'].shape[1]
    hw = R['w_br_c'].shape[1]
    nh = R['mlstm_bi'].shape[1]
    splits = (mw, mw, mw, mw, nh, nh, rw, rw, hw, hw, hw, hw, D, D, D)
    pts = [0] + [int(v) for v in np.cumsum(splits)]
    runs = ((12, 15), (0, 4), (6, 12))
    segs, d0 = [], 0
    for a, b in runs:
        segs.append((pts[a], pts[b] - pts[a], d0))
        d0 += pts[b] - pts[a]
    P['w_in_main'], P['w_in_gate'] = _mix_w_in_cast(jnp.swapaxes(R['w_in'], 1, 2), tuple(segs), pts[4], 2 * nh)
    P['gate_bias'] = jnp.pad(jnp.concatenate([R['mlstm_bi'], R['mlstm_bf']], axis=-1).astype(F32),
                             ((0, 0), (0, GATE_PAD - 2 * nh))).reshape(depth, 1, GATE_PAD)

    eye = jnp.eye(RG_BLOCKS, dtype=F32)
    dense = lambda w: jnp.einsum('lnde,nm->lndme', w, eye).reshape(depth, rw, rw).astype(BF16)
    P['rg_wa'] = dense(R['rg_wa'])
    P['rg_wx'] = dense(R['rg_wx'])

    for name in ('w_br_a', 'w_br_b', 'w_br_c', 'w_out', 'xa_wq', 'xa_wk', 'xa_wv', 'xa_wo'):
        P[name] = R[name].astype(BF16)
    return P


def _ffn_sublayer(x, P, name, l, tm, bf16_weights):
    npre, npost = P[name + '_norm_pre'], P[name + '_norm_post']
    if (name, l) in bf16_weights:
        return _ffn(x, npre, *bf16_weights[(name, l)], npost, l, tm)
    y, wg, wu, wo = _ffn_first(x, npre, P[name + '_w_in'], P[name + '_w_out'], npost, l, tm)
    bf16_weights[(name, l)] = (wg, wu, wo)
    if x.shape[0] > tm:
        y = _ffn(x, npre, wg, wu, wo, npost, l, tm, done=y)
    return y


def _run_trunk(x3, states, state_has_layers, mem, mem_k, mem_v, P, bf16_weights):
    B, T, D = x3.shape
    M = B * T
    depth = P['ffn1_norm_pre'].shape[0]
    x = x3.reshape(M, D)
    tm_ffn = min(512, M)
    tm_proj = min(1024, M)
    tm_merge = min(512, M)
    tq = min(512, T)
    C0, n0, m0, hr0, buf0, S0 = states
    m0 = m0.reshape(m0.shape[0], m0.shape[1], 1, m0.shape[2])
    hr0 = hr0.reshape(hr0.shape[0], hr0.shape[1], 1, hr0.shape[2])
    mw = P['w_br_a'].shape[1]
    rw = P['w_br_b'].shape[1]
    hw = P['w_br_c'].shape[1]
    col_m = 3 * D // mw
    col_r = (3 * D + 4 * mw) // rw
    col_h = (3 * D + 4 * mw + 2 * rw) // hw

    if mem is not None:
        mem_k, mem_v = _mem_kv(mem, P, XA_HEADS)

    outs = [[] for _ in range(6)]
    for l in range(depth):
        ls = l if state_has_layers else 0
        x = _ffn_sublayer(x, P, 'ffn1', l, tm_ffn, bf16_weights)
        z, gates = _mix_in(x, P['mix_norm_pre'], P['w_in_main'], P['w_in_gate'], l, tm_proj, 1024)
        ya, C1, n1, m1 = _mlstm(z, gates, P['gate_bias'], P['mlstm_norm'], C0, n0, m0, l, ls, B, T, col_m)
        yb, buf1, hr1 = _rglru(z, P, buf0, hr0, l, ls, B, T, col_r)
        yc, S1 = _hgrn(z, P['hg_lb_logits'], P['hg_norm'], S0, l, ls, B, T, col_h)
        x = _merge(x, ya, yb, yc, z, P, l, tm_merge)
        x = _xattn(x, mem_k, mem_v, P, l, l, B, T, tq)
        x = _ffn_sublayer(x, P, 'ffn2', l, tm_ffn, bf16_weights)
        for lst, s in zip(outs, (C1, n1, m1.reshape(B, -1), hr1.reshape(B, -1), buf1, S1)):
            lst.append(s)
    stacked = tuple(jnp.stack(lst) for lst in outs)
    return x.reshape(B, T, D), stacked, mem_k, mem_v


def kernel(x_prompt, x_sample, mem_prompt, cache_mem_k, cache_mem_v, state_mlstm_C, state_mlstm_n, state_mlstm_m, state_rglru_h, state_rglru_conv, state_hgrn_S, ffn1_norm_pre, ffn1_w_in, ffn1_w_out, ffn1_norm_post, mix_norm_pre, w_in, mlstm_bi, mlstm_bf, mlstm_norm, rg_conv_w, rg_conv_b, rg_wa, rg_ba, rg_wx, rg_bx, rg_lambda, hg_lb_logits, hg_norm, w_br_a, w_br_b, w_br_c, w_out, mix_norm_post, xa_norm_pre, xa_mem_norm, xa_wq, xa_wk, xa_wv, xa_wo, xa_norm_post, ffn2_norm_pre, ffn2_w_in, ffn2_w_out, ffn2_norm_post):
    R = dict(ffn1_norm_pre=ffn1_norm_pre, ffn1_w_in=ffn1_w_in, ffn1_w_out=ffn1_w_out, ffn1_norm_post=ffn1_norm_post,
             mix_norm_pre=mix_norm_pre, w_in=w_in, mlstm_bi=mlstm_bi, mlstm_bf=mlstm_bf, mlstm_norm=mlstm_norm,
             rg_conv_w=rg_conv_w, rg_conv_b=rg_conv_b, rg_wa=rg_wa, rg_ba=rg_ba, rg_wx=rg_wx, rg_bx=rg_bx,
             rg_lambda=rg_lambda, hg_lb_logits=hg_lb_logits, hg_norm=hg_norm,
             w_br_a=w_br_a, w_br_b=w_br_b, w_br_c=w_br_c, w_out=w_out, mix_norm_post=mix_norm_post,
             xa_norm_pre=xa_norm_pre, xa_mem_norm=xa_mem_norm, xa_wq=xa_wq, xa_wk=xa_wk, xa_wv=xa_wv,
             xa_wo=xa_wo, xa_norm_post=xa_norm_post,
             ffn2_norm_pre=ffn2_norm_pre, ffn2_w_in=ffn2_w_in, ffn2_w_out=ffn2_w_out, ffn2_norm_post=ffn2_norm_post)
    P = _prep_params(R)
    bf16_weights = {}
    B = x_prompt.shape[0]
    zeros_like_state = lambda s: jnp.zeros((1, B) + s.shape[2:], F32)
    init = tuple(zeros_like_state(s) for s in (state_mlstm_C, state_mlstm_n, state_mlstm_m, state_rglru_h,
                                               state_rglru_conv, state_hgrn_S))
    y_prompt, p_states, p_mem_k, p_mem_v = _run_trunk(x_prompt, init, False, mem_prompt, None, None, P, bf16_weights)
    s_init = (state_mlstm_C, state_mlstm_n, state_mlstm_m, state_rglru_h, state_rglru_conv, state_hgrn_S)
    y_sample, s_states, _, _ = _run_trunk(x_sample, s_init, True, None, cache_mem_k, cache_mem_v, P, bf16_weights)
    return (y_prompt, y_sample) + p_states + (p_mem_k, p_mem_v) + s_states
```

```python
import functools

import jax
import jax.numpy as jnp
import numpy as np
from jax import lax
from jax.experimental import pallas as pl
from jax.experimental.pallas import tpu as pltpu

F32 = jnp.float32
BF16 = jnp.bfloat16
EPS = 1e-6

V7X_VMEM_LIMIT_BYTES = 56 * 1024 * 1024
LANES = 128

XA_HEADS = 4
MLSTM_HEADS = 4
HG_HEADS = 4
RG_BLOCKS = 8
RG_C = 8.0
CONV_W = 4
D_FF_TILE = 512
MLSTM_CHUNK = 256
RG_CHUNK = 256
HG_CHUNK = 128
GATE_PAD = LANES


def _params(*sem):
    return pltpu.CompilerParams(dimension_semantics=sem, vmem_limit_bytes=V7X_VMEM_LIMIT_BYTES)


def _rms(x, w):
    return x * lax.rsqrt(jnp.mean(x * x, axis=-1, keepdims=True) + EPS) * w


def _sigmoid(x):
    return 0.5 * jnp.tanh(0.5 * x) + 0.5


def _log_sigmoid(x):
    return jnp.minimum(x, 0.0) - jnp.log1p(jnp.exp(-jnp.abs(x)))


def _softplus(x):
    return jnp.maximum(x, 0.0) + jnp.log1p(jnp.exp(-jnp.abs(x)))


def _dot(a, b):
    return jnp.dot(a, b, preferred_element_type=F32)


def _dot_nt(a, b):
    return lax.dot_general(a, b, (((1,), (1,)), ((), ())), preferred_element_type=F32)


def _dot_tn(a, b):
    return lax.dot_general(a, b, (((0,), (0,)), ((), ())), preferred_element_type=F32)


def _shift_rows(x, d, fill, ridx):
    return jnp.where(ridx >= d, pltpu.roll(x, d, 0), fill)


def _ffn_step(j, nj, x_ref, npre_ref, weights, npost_ref, o_ref, hn_ref, acc_ref):
    @pl.when(j == 0)
    def _():
        hn_ref[...] = _rms(x_ref[...], npre_ref[...]).astype(BF16)
        acc_ref[...] = jnp.zeros_like(acc_ref)

    wg, wu, wo = weights()
    hn = hn_ref[...]
    g = _dot(hn, wg)
    u = _dot(hn, wu)
    a = (g * _sigmoid(g) * u).astype(BF16)
    acc_ref[...] += _dot(a, wo)

    @pl.when(j == nj - 1)
    def _():
        o_ref[...] = x_ref[...] + 0.5 * _rms(acc_ref[...], npost_ref[...])


def _ffn_kernel(x_ref, npre_ref, wg_ref, wu_ref, wo_ref, npost_ref, *rest):
    o_ref, hn_ref, acc_ref = rest[-3:]
    _ffn_step(pl.program_id(1), pl.num_programs(1), x_ref, npre_ref,
              lambda: (wg_ref[...], wu_ref[...], wo_ref[...]), npost_ref, o_ref, hn_ref, acc_ref)


def _ffn(x, npre, wg, wu, wo, npost, l, tm, done=None):
    M, D = x.shape
    Fp = wo.shape[0]
    tn = D_FF_TILE
    nj = Fp // tn
    skip = 0 if done is None else 1
    args = [x, npre, wg, wu, wo, npost]
    in_specs = [
        pl.BlockSpec((tm, D), lambda i, j: (i + skip, 0)),
        pl.BlockSpec((None, 1, D), lambda i, j: (l, 0, 0)),
        pl.BlockSpec((D, tn), lambda i, j: (0, j)),
        pl.BlockSpec((D, tn), lambda i, j: (0, j)),
        pl.BlockSpec((tn, D), lambda i, j: (j, 0)),
        pl.BlockSpec((None, 1, D), lambda i, j: (l, 0, 0)),
    ]
    aliases = {}
    if done is not None:
        args.append(done)
        in_specs.append(pl.BlockSpec(memory_space=pl.ANY))
        aliases = {len(args) - 1: 0}
    return pl.pallas_call(
        _ffn_kernel,
        grid=(M // tm - skip, nj),
        in_specs=in_specs,
        out_specs=pl.BlockSpec((tm, D), lambda i, j: (i + skip, 0)),
        out_shape=jax.ShapeDtypeStruct((M, D), F32),
        scratch_shapes=[pltpu.VMEM((tm, D), BF16), pltpu.VMEM((tm, D), F32)],
        input_output_aliases=aliases,
        compiler_params=_params("parallel", "arbitrary"),
        name="ffn",
    )(*args)


def _ffn_first_kernel(x_ref, npre_ref, g32_ref, u0_ref, u32_ref, wo32_ref, npost_ref,
                      o_ref, wg_ref, wu_ref, wo_ref, hn_ref, acc_ref, uprev_ref, *, F, shift):
    j = pl.program_id(0)
    tn = wg_ref.shape[-1]

    def weights():
        col_ok = lax.broadcasted_iota(jnp.int32, (1, tn), 1) + j * tn < F
        row_ok = lax.broadcasted_iota(jnp.int32, (tn, 1), 0) + j * tn < F
        if shift:
            @pl.when(j == 0)
            def _():
                uprev_ref[...] = u0_ref[...]

            u32 = jnp.concatenate([uprev_ref[:, shift:], u32_ref[:, :shift]], axis=1)
            uprev_ref[...] = u32_ref[...]
        else:
            u32 = u32_ref[...]
        wg = jnp.where(col_ok, g32_ref[...], 0.0).astype(BF16)
        wu = jnp.where(col_ok, u32, 0.0).astype(BF16)
        wo = jnp.where(row_ok, wo32_ref[...], 0.0).astype(BF16)
        wg_ref[...] = wg
        wu_ref[...] = wu
        wo_ref[...] = wo
        return wg, wu, wo

    _ffn_step(j, pl.num_programs(0), x_ref, npre_ref, weights, npost_ref, o_ref, hn_ref, acc_ref)


def _ffn_first(x, npre, w_in, w_out, npost, l, tm, tn=256):
    M, D = x.shape
    F = w_out.shape[1]
    Fp = -(-F // D_FF_TILE) * D_FF_TILE
    nj = Fp // tn
    q, shift = divmod(F, tn)
    last_in = -(-2 * F // tn) - 1
    return pl.pallas_call(
        functools.partial(_ffn_first_kernel, F=F, shift=shift),
        grid=(nj,),
        in_specs=[
            pl.BlockSpec((tm, D), lambda j: (0, 0)),
            pl.BlockSpec((None, 1, D), lambda j: (l, 0, 0)),
            pl.BlockSpec((None, D, tn), lambda j: (l, 0, j)),
            pl.BlockSpec((None, D, tn), lambda j: (l, 0, q)),
            pl.BlockSpec((None, D, tn), lambda j: (l, 0, jnp.minimum(q + j + (1 if shift else 0), last_in))),
            pl.BlockSpec((None, tn, D), lambda j: (l, j, 0)),
            pl.BlockSpec((None, 1, D), lambda j: (l, 0, 0)),
        ],
        out_specs=[
            pl.BlockSpec((tm, D), lambda j: (0, 0)),
            pl.BlockSpec((D, tn), lambda j: (0, j)),
            pl.BlockSpec((D, tn), lambda j: (0, j)),
            pl.BlockSpec((tn, D), lambda j: (j, 0)),
        ],
        out_shape=[
            jax.ShapeDtypeStruct((M, D), F32),
            jax.ShapeDtypeStruct((D, Fp), BF16),
            jax.ShapeDtypeStruct((D, Fp), BF16),
            jax.ShapeDtypeStruct((Fp, D), BF16),
        ],
        scratch_shapes=[pltpu.VMEM((tm, D), BF16), pltpu.VMEM((tm, D), F32), pltpu.VMEM((D, tn), F32)],
        compiler_params=_params("arbitrary"),
        name="ffn_first",
    )(x, npre, w_in, w_in, w_in, w_out, npost)


def _nmm_kernel(x_ref, nw_ref, w_ref, o_ref, hn_ref):
    @pl.when(pl.program_id(1) == 0)
    def _():
        hn_ref[...] = _rms(x_ref[...], nw_ref[...]).astype(BF16)

    o_ref[...] = _dot(hn_ref[...], w_ref[...]).astype(o_ref.dtype)


def _norm_matmul(x, nw, w, l, out_dtype, tm, tn, name):
    M, D = x.shape
    N = w.shape[-1]
    return pl.pallas_call(
        _nmm_kernel,
        grid=(M // tm, N // tn),
        in_specs=[
            pl.BlockSpec((tm, D), lambda i, j: (i, 0)),
            pl.BlockSpec((None, 1, D), lambda i, j: (l, 0, 0)),
            pl.BlockSpec((None, D, tn), lambda i, j: (l, 0, j)),
        ],
        out_specs=pl.BlockSpec((tm, tn), lambda i, j: (i, j)),
        out_shape=jax.ShapeDtypeStruct((M, N), out_dtype),
        scratch_shapes=[pltpu.VMEM((tm, D), BF16)],
        compiler_params=_params("parallel", "arbitrary"),
        name=name,
    )(x, nw, w)


def _mix_in_kernel(x_ref, nw_ref, w_ref, wg_ref, z_ref, g_ref, hn_ref):
    @pl.when(pl.program_id(1) == 0)
    def _():
        hn = _rms(x_ref[...], nw_ref[...]).astype(BF16)
        hn_ref[...] = hn
        g_ref[...] = _dot_nt(hn, wg_ref[...])

    z_ref[...] = _dot_nt(hn_ref[...], w_ref[...]).astype(z_ref.dtype)


def _mix_in(x, nw, w, wg, l, tm, tn):
    M, D = x.shape
    N = w.shape[-2]
    G = wg.shape[-2]
    return pl.pallas_call(
        _mix_in_kernel,
        grid=(M // tm, N // tn),
        in_specs=[
            pl.BlockSpec((tm, D), lambda i, j: (i, 0)),
            pl.BlockSpec((None, 1, D), lambda i, j: (l, 0, 0)),
            pl.BlockSpec((None, tn, D), lambda i, j: (l, j, 0)),
            pl.BlockSpec((None, G, D), lambda i, j: (l, 0, 0)),
        ],
        out_specs=[pl.BlockSpec((tm, tn), lambda i, j: (i, j)), pl.BlockSpec((tm, G), lambda i, j: (i, 0))],
        out_shape=[jax.ShapeDtypeStruct((M, N), BF16), jax.ShapeDtypeStruct((M, G), F32)],
        scratch_shapes=[pltpu.VMEM((tm, D), BF16)],
        compiler_params=_params("parallel", "arbitrary"),
        name="mix_in",
    )(x, nw, w, wg)


def _mlstm_kernel(q_ref, k_ref, v_ref, og_ref, g_ref, gb_ref, nw_ref, C0_ref, n0_ref, m0_ref,
                  hm_ref, C_ref, n_ref, m_ref, *, L, dh):
    H = MLSTM_HEADS

    @pl.when(pl.program_id(1) == 0)
    def _():
        C_ref[...] = C0_ref[...]
        n_ref[...] = n0_ref[...]
        m_ref[...] = m0_ref[...]

    ga = g_ref[...] + gb_ref[...]
    m_prev = m_ref[0]
    row = lax.broadcasted_iota(jnp.int32, (L, L), 0)
    col = lax.broadcasted_iota(jnp.int32, (L, L), 1)
    causal = row >= col
    eye = row == col
    gaT = ga.T if L % LANES == 0 else None

    def as_row(x_col):
        return jnp.sum(jnp.where(eye, x_col, 0.0), axis=0, keepdims=True)

    for h in range(H):
        hs = slice(h * dh, (h + 1) * dh)
        ig_c = ga[:, h:h + 1]
        lf_c = _log_sigmoid(ga[:, H + h:H + h + 1])
        if gaT is not None:
            ig_r = gaT[h:h + 1, :]
            lf_r = _log_sigmoid(gaT[H + h:H + h + 1, :])
        else:
            ig_r = as_row(ig_c)
            lf_r = as_row(lf_c)
        b_c = jnp.sum(jnp.where(causal, lf_r, 0.0), axis=1, keepdims=True)
        b_r = jnp.sum(jnp.where(row <= col, lf_c, 0.0), axis=0, keepdims=True)
        m0 = m_prev[:, h:h + 1]
        src_r = ig_r - b_r
        peak_c = jnp.maximum(m0, jnp.max(jnp.where(causal, src_r, -jnp.inf), axis=1, keepdims=True))
        m_c = b_c + peak_c
        w = jnp.exp(jnp.where(causal, src_r - peak_c, -jnp.inf))
        w_inter = jnp.exp(m0 - peak_c)

        q = q_ref[:, hs]
        k = k_ref[:, hs] * (dh ** -0.5)
        v = v_ref[:, hs]
        C0 = C_ref[0, h]
        n0 = n_ref[0, h:h + 1, :]
        wqk = w * _dot_nt(q, k)
        num = _dot(wqk.astype(BF16), v) + w_inter * _dot(q, C0.astype(BF16))
        den = (jnp.sum(wqk, axis=1, keepdims=True)
               + w_inter * jnp.sum(q.astype(F32) * n0, axis=1, keepdims=True))
        hh = num / jnp.maximum(jnp.abs(den), jnp.exp(-m_c))
        hn = hh * lax.rsqrt(jnp.mean(hh * hh, axis=-1, keepdims=True) + EPS) * nw_ref[:, hs]
        hm_ref[:, hs] = (_sigmoid(og_ref[:, hs].astype(F32)) * hn).astype(BF16)

        m_end = m_c[L - 1:L, :]
        w_end = jnp.exp(b_c[L - 1:L, :] - b_c + ig_c - m_end)
        s_end = w_inter[L - 1:L, :]
        kw = k.astype(F32) * w_end
        C_ref[0, h] = s_end * C0 + _dot_tn(kw.astype(BF16), v)
        n_ref[0, h:h + 1, :] = s_end * n0 + jnp.sum(kw, axis=0, keepdims=True)
        m_ref[0, :, h:h + 1] = m_end


def _mlstm(z, gates, gate_bias, norm_w, C0, n0, m0, l, ls, B, T, col0):
    H = MLSTM_HEADS
    dh = C0.shape[-1]
    W = H * dh
    L = min(MLSTM_CHUNK, T)
    nC = T // L
    M = B * T
    zspec = lambda cb: pl.BlockSpec((L, W), lambda b, c: (b * nC + c, cb))
    return pl.pallas_call(
        functools.partial(_mlstm_kernel, L=L, dh=dh),
        grid=(B, nC),
        in_specs=[
            zspec(col0), zspec(col0 + 1), zspec(col0 + 2), zspec(col0 + 3),
            pl.BlockSpec((L, GATE_PAD), lambda b, c: (b * nC + c, 0)),
            pl.BlockSpec((None, 1, GATE_PAD), lambda b, c: (l, 0, 0)),
            pl.BlockSpec((None, 1, W), lambda b, c: (l, 0, 0)),
            pl.BlockSpec((None, 1, H, dh, dh), lambda b, c: (ls, b, 0, 0, 0)),
            pl.BlockSpec((None, 1, H, dh), lambda b, c: (ls, b, 0, 0)),
            pl.BlockSpec((None, 1, 1, H), lambda b, c: (ls, b, 0, 0)),
        ],
        out_specs=[
            pl.BlockSpec((L, W), lambda b, c: (b * nC + c, 0)),
            pl.BlockSpec((1, H, dh, dh), lambda b, c: (b, 0, 0, 0)),
            pl.BlockSpec((1, H, dh), lambda b, c: (b, 0, 0)),
            pl.BlockSpec((1, 1, H), lambda b, c: (b, 0, 0)),
        ],
        out_shape=[
            jax.ShapeDtypeStruct((M, W), BF16),
            jax.ShapeDtypeStruct((B, H, dh, dh), F32),
            jax.ShapeDtypeStruct((B, H, dh), F32),
            jax.ShapeDtypeStruct((B, 1, H), F32),
        ],
        compiler_params=_params("parallel", "arbitrary"),
        name="mlstm",
    )(z, z, z, z, gates, gate_bias, norm_w, C0, n0, m0)


def _rglru_kernel(rx_ref, rg_ref, cw_ref, cbias_ref, wa_ref, ba_ref, wx_ref, bx_ref, lam_ref, buf0_ref, h0_ref,
                  y_ref, buf_ref, h_ref, cbuf_ref, *, L):
    c = pl.program_id(1)
    W = rx_ref.shape[-1]
    TAIL = CONV_W - 1

    @pl.when(c == 0)
    def _():
        cbuf_ref[0:8, :] = jnp.zeros((8, W), F32)
        cbuf_ref[8 - TAIL:8, :] = buf0_ref[0]
        h_ref[...] = h0_ref[...]

    cbuf_ref[8:8 + L, :] = rx_ref[...].astype(F32)
    xc = cbias_ref[...]
    for j in range(CONV_W):
        xc = xc + cbuf_ref[8 - TAIL + j:8 - TAIL + j + L, :] * cw_ref[j:j + 1, :]

    xcb = xc.astype(BF16)
    r = _sigmoid(_dot(xcb, wa_ref[...]) + ba_ref[...])
    i = _sigmoid(_dot(xcb, wx_ref[...]) + bx_ref[...])
    log_a = (-RG_C * _softplus(-lam_ref[...])) * r
    a = jnp.exp(log_a)
    th = jnp.tanh(log_a)
    u = jnp.sqrt(-2.0 * th / (1.0 - th)) * (i * xc)

    ridx = lax.broadcasted_iota(jnp.int32, (L, W), 0)
    d = 1
    while d < L:
        a_sh = _shift_rows(a, d, 1.0, ridx)
        u_sh = _shift_rows(u, d, 0.0, ridx)
        u = a * u_sh + u
        a = a * a_sh
        d *= 2
    h = a * h_ref[0] + u
    h_ref[0] = h[L - 1:L, :]
    y_ref[...] = (h * jax.nn.gelu(rg_ref[...].astype(F32))).astype(BF16)

    cbuf_ref[0:8, :] = cbuf_ref[L:L + 8, :]

    @pl.when(c == pl.num_programs(1) - 1)
    def _():
        buf_ref[0] = cbuf_ref[8 + L - TAIL:8 + L, :]


def _rglru(z, P, buf0, h0, l, ls, B, T, col0):
    W = h0.shape[-1]
    L = min(RG_CHUNK, T)
    nC = T // L
    M = B * T
    vec = lambda: pl.BlockSpec((None, 1, W), lambda b, c: (l, 0, 0))
    mat = lambda: pl.BlockSpec((None, W, W), lambda b, c: (l, 0, 0))
    return pl.pallas_call(
        functools.partial(_rglru_kernel, L=L),
        grid=(B, nC),
        in_specs=[
            pl.BlockSpec((L, W), lambda b, c: (b * nC + c, col0)),
            pl.BlockSpec((L, W), lambda b, c: (b * nC + c, col0 + 1)),
            pl.BlockSpec((None, CONV_W, W), lambda b, c: (l, 0, 0)),
            vec(), mat(), vec(), mat(), vec(), vec(),
            pl.BlockSpec((None, 1, CONV_W - 1, W), lambda b, c: (ls, b, 0, 0)),
            pl.BlockSpec((None, 1, 1, W), lambda b, c: (ls, b, 0, 0)),
        ],
        out_specs=[
            pl.BlockSpec((L, W), lambda b, c: (b * nC + c, 0)),
            pl.BlockSpec((1, CONV_W - 1, W), lambda b, c: (b, 0, 0)),
            pl.BlockSpec((1, 1, W), lambda b, c: (b, 0, 0)),
        ],
        out_shape=[
            jax.ShapeDtypeStruct((M, W), BF16),
            jax.ShapeDtypeStruct((B, CONV_W - 1, W), F32),
            jax.ShapeDtypeStruct((B, 1, W), F32),
        ],
        scratch_shapes=[pltpu.VMEM((L + 8, W), F32)],
        compiler_params=_params("parallel", "arbitrary"),
        name="rglru",
    )(z, z, P['rg_conv_w'], P['rg_conv_b'], P['rg_wa'], P['rg_ba'], P['rg_wx'], P['rg_bx'], P['rg_lambda'],
      buf0, h0)


def _block_ref_rows(cb, m, ridx):
    L, W = cb.shape
    n = 2 * m
    if m == 1:
        return jnp.where((ridx & 1) == 0, cb, pltpu.roll(cb, 1, 0))
    if m == 2:
        off = ridx & 3
        return jnp.where(off == 0, pltpu.roll(cb, L - 1, 0),
                         jnp.where(off == 1, cb, jnp.where(off == 2, pltpu.roll(cb, 1, 0), pltpu.roll(cb, 2, 0))))
    return jnp.concatenate(
        [jnp.broadcast_to(cb[j * n + m - 1:j * n + m, :], (n, W)) for j in range(L // n)], axis=0)


def _hgrn_kernel(q_ref, f_ref, i_ref, g_ref, lbl_ref, nw_ref, S0_ref, y_ref, S_ref, *, L, layer):
    H = HG_HEADS
    W = q_ref.shape[-1]
    dk = W // H

    @pl.when(pl.program_id(1) == 0)
    def _():
        S_ref[...] = S0_ref[...]

    lg = lbl_ref[...]
    e = jnp.exp(lg - jnp.max(lg, axis=0, keepdims=True))
    p = e / jnp.sum(e, axis=0, keepdims=True)
    lb = jnp.zeros((1, W), F32)
    for r in range(1, layer + 1):
        lb = lb + p[r:r + 1, :]

    f = lb + (1.0 - lb) * jax.nn.sigmoid(f_ref[...].astype(F32))
    cb = jnp.log(f)
    ridx = lax.broadcasted_iota(jnp.int32, (L, W), 0)
    d = 1
    while d < L:
        cb = cb + _shift_rows(cb, d, 0.0, ridx)
        d *= 2
    kk = 1.0 - f
    q = q_ref[...].astype(F32)
    v = i_ref[...].astype(F32)
    vb = v.astype(BF16)

    row = lax.broadcasted_iota(jnp.int32, (L, L), 0)
    col = lax.broadcasted_iota(jnp.int32, (L, L), 1)
    split = jnp.where(row > col, row ^ col, 0)
    a = [jnp.zeros((L, L), F32) for _ in range(H)]
    m = 1
    while m < L:
        dref = cb - _block_ref_rows(cb, m, ridx)
        dec = jnp.exp(-jnp.abs(dref))
        qt = (q * dec).astype(BF16)
        ks = (kk * dec).astype(BF16)
        level = (split // m) == 1
        for h in range(H):
            hs = slice(h * dk, (h + 1) * dk)
            a[h] = jnp.where(level, _dot_nt(qt[:, hs], ks[:, hs]), a[h])
        m *= 2

    cb_end = cb[L - 1:L, :]
    qe = (q * jnp.exp(cb)).astype(BF16)
    ke = (kk * jnp.exp(cb_end - cb)).astype(BF16)
    qk = q * kk
    eye = (lax.broadcasted_iota(jnp.int32, (dk, dk), 0) == lax.broadcasted_iota(jnp.int32, (dk, dk), 1))
    outs = []
    for h in range(H):
        hs = slice(h * dk, (h + 1) * dk)
        S = S_ref[0, h]
        o = (_dot(a[h].astype(BF16), vb[:, hs]) + jnp.sum(qk[:, hs], axis=1, keepdims=True) * v[:, hs]
             + _dot(qe[:, hs], S.astype(BF16)))
        dec_col = jnp.sum(jnp.where(eye, jnp.exp(cb_end[:, hs]), 0.0), axis=1, keepdims=True)
        S_ref[0, h] = dec_col * S + _dot_tn(ke[:, hs], vb[:, hs])
        outs.append(o * lax.rsqrt(jnp.mean(o * o, axis=-1, keepdims=True) + EPS))
    gate = g_ref[...].astype(F32)
    y_ref[...] = (jnp.concatenate(outs, axis=1) * nw_ref[...] * (gate * _sigmoid(gate))).astype(BF16)


def _hgrn(z, lb_logits, norm_w, S0, l, ls, B, T, col0):
    H, dk, dv = S0.shape[-3:]
    W = H * dk
    L = min(HG_CHUNK, T)
    nC = T // L
    M = B * T
    depth = lb_logits.shape[0]
    zspec = lambda cb: pl.BlockSpec((L, W), lambda b, c: (b * nC + c, cb))
    return pl.pallas_call(
        functools.partial(_hgrn_kernel, L=L, layer=l),
        grid=(B, nC),
        in_specs=[
            zspec(col0), zspec(col0 + 1), zspec(col0 + 2), zspec(col0 + 3),
            pl.BlockSpec((depth, W), lambda b, c: (0, 0)),
            pl.BlockSpec((None, 1, W), lambda b, c: (l, 0, 0)),
            pl.BlockSpec((None, 1, H, dk, dv), lambda b, c: (ls, b, 0, 0, 0)),
        ],
        out_specs=[
            pl.BlockSpec((L, W), lambda b, c: (b * nC + c, 0)),
            pl.BlockSpec((1, H, dk, dv), lambda b, c: (b, 0, 0, 0)),
        ],
        out_shape=[
            jax.ShapeDtypeStruct((M, W), BF16),
            jax.ShapeDtypeStruct((B, H, dk, dv), F32),
        ],
        compiler_params=_params("parallel", "arbitrary"),
        name="hgrn",
    )(z, z, z, z, lb_logits, norm_w, S0)


def _merge_kernel(a_ref, b_ref, c_ref, ga_ref, gb_ref, gc_ref, wa_ref, wb_ref, wc_ref, wo_ref, npost_ref, x_ref,
                  o_ref):
    sig = lambda r: _sigmoid(r[...].astype(F32))
    merged = (sig(ga_ref) * _dot(a_ref[...], wa_ref[...])
              + sig(gb_ref) * _dot(b_ref[...], wb_ref[...])
              + sig(gc_ref) * _dot(c_ref[...], wc_ref[...]))
    out = _dot(merged.astype(BF16), wo_ref[...])
    o_ref[...] = x_ref[...] + _rms(out, npost_ref[...])


def _merge(x, ya, yb, yc, z, P, l, tm):
    M, D = x.shape
    rows = lambda w: pl.BlockSpec((tm, w), lambda i: (i, 0))
    wspec = lambda k: pl.BlockSpec((None, k, D), lambda i: (l, 0, 0), pipeline_mode=pl.Buffered(1))
    return pl.pallas_call(
        _merge_kernel,
        grid=(M // tm,),
        in_specs=[
            rows(ya.shape[1]), rows(yb.shape[1]), rows(yc.shape[1]),
            pl.BlockSpec((tm, D), lambda i: (i, 0)),
            pl.BlockSpec((tm, D), lambda i: (i, 1)),
            pl.BlockSpec((tm, D), lambda i: (i, 2)),
            wspec(ya.shape[1]), wspec(yb.shape[1]), wspec(yc.shape[1]), wspec(D),
            pl.BlockSpec((None, 1, D), lambda i: (l, 0, 0)),
            rows(D),
        ],
        out_specs=rows(D),
        out_shape=jax.ShapeDtypeStruct((M, D), F32),
        compiler_params=_params("parallel"),
        name="merge",
    )(ya, yb, yc, z, z, z, P['w_br_a'], P['w_br_b'], P['w_br_c'], P['w_out'], P['mix_norm_post'], x)


def _xattn_kernel(x_ref, npre_ref, wq_ref, k_ref, v_ref, wo_ref, npost_ref, o_ref, *kv_scratch):
    D = x_ref.shape[-1]
    dh = D // XA_HEADS
    if kv_scratch:
        kb_ref, vb_ref = kv_scratch

        @pl.when(pl.program_id(1) == 0)
        def _():
            for h in range(XA_HEADS):
                kb_ref[:, h * dh:(h + 1) * dh] = k_ref[:, h, :].astype(BF16)
                vb_ref[:, h * dh:(h + 1) * dh] = v_ref[:, h, :].astype(BF16)
    else:
        kb_ref, vb_ref = k_ref, v_ref

    x = x_ref[...]
    q = _dot(_rms(x, npre_ref[...]).astype(BF16), wq_ref[...]).astype(BF16)
    outs = []
    for h in range(XA_HEADS):
        hs = slice(h * dh, (h + 1) * dh)
        s = _dot_nt(q[:, hs], kb_ref[:, hs]) * (dh ** -0.5)
        e = jnp.exp(s - jnp.max(s, axis=-1, keepdims=True))
        p = e / jnp.sum(e, axis=-1, keepdims=True)
        outs.append(_dot(p.astype(BF16), vb_ref[:, hs]).astype(BF16))
    out = _dot(jnp.concatenate(outs, axis=1), wo_ref[...])
    o_ref[...] = x + _rms(out, npost_ref[...])


def _xattn(x, mem_k, mem_v, P, l, lk, B, T, tq):
    M, D = x.shape
    n_mem = mem_k.shape[2]
    nT = T // tq
    wspec = lambda: pl.BlockSpec((None, D, D), lambda b, t: (l, 0, 0), pipeline_mode=pl.Buffered(1))
    nspec = lambda: pl.BlockSpec((None, 1, D), lambda b, t: (l, 0, 0))
    kv_block = (None, None) + mem_k.shape[2:]
    kv_index = lambda b, t: (lk, b) + (0,) * (mem_k.ndim - 2)
    kvspec = lambda: pl.BlockSpec(kv_block, kv_index)
    scratch = [] if mem_k.dtype == BF16 else [pltpu.VMEM((n_mem, D), BF16), pltpu.VMEM((n_mem, D), BF16)]
    return pl.pallas_call(
        _xattn_kernel,
        grid=(B, nT),
        in_specs=[
            pl.BlockSpec((tq, D), lambda b, t: (b * nT + t, 0)),
            nspec(), wspec(), kvspec(), kvspec(), wspec(), nspec(),
        ],
        out_specs=pl.BlockSpec((tq, D), lambda b, t: (b * nT + t, 0)),
        out_shape=jax.ShapeDtypeStruct((M, D), F32),
        scratch_shapes=scratch,
        compiler_params=_params("parallel", "arbitrary"),
        name="xattn",
    )(x, P['xa_norm_pre'], P['xa_wq'], mem_k, mem_v, P['xa_wo'], P['xa_norm_post'])


def _mem_kv_kernel(m_ref, nw_ref, wk_ref, wv_ref, k_ref, v_ref, kb_ref, vb_ref):
    H, dh = k_ref.shape[-2:]
    hn = _rms(m_ref[...], nw_ref[...]).astype(BF16)
    k = _dot(hn, wk_ref[...])
    v = _dot(hn, wv_ref[...])
    kb_ref[...] = k.astype(BF16)
    vb_ref[...] = v.astype(BF16)
    for h in range(H):
        k_ref[:, h, :] = k[:, h * dh:(h + 1) * dh]
        v_ref[:, h, :] = v[:, h * dh:(h + 1) * dh]


def _mem_kv(mem, P, H):
    B, n_mem, D = mem.shape
    depth = P['xa_wk'].shape[0]
    out = jax.ShapeDtypeStruct((depth, B, n_mem, H, D // H), F32)
    outb = jax.ShapeDtypeStruct((depth, B, n_mem, D), BF16)
    ospec = lambda: pl.BlockSpec((None, None, n_mem, H, D // H), lambda l, b: (l, b, 0, 0, 0))
    obspec = lambda: pl.BlockSpec((None, None, n_mem, D), lambda l, b: (l, b, 0, 0))
    wspec = lambda: pl.BlockSpec((None, D, D), lambda l, b: (l, 0, 0))
    return pl.pallas_call(
        _mem_kv_kernel,
        grid=(depth, B),
        in_specs=[
            pl.BlockSpec((None, n_mem, D), lambda l, b: (b, 0, 0)),
            pl.BlockSpec((None, 1, D), lambda l, b: (l, 0, 0)),
            wspec(), wspec(),
        ],
        out_specs=[ospec(), ospec(), obspec(), obspec()],
        out_shape=[out, out, outb, outb],
        compiler_params=_params("parallel", "arbitrary"),
        name="mem_kv",
    )(mem, P['xa_mem_norm'], P['xa_wk'], P['xa_wv'])


def _mix_w_in_cast_kernel(x_ref, o_ref, g_ref, *, segs, gate0, ngate):
    for s0, w, d0 in segs:
        o_ref[d0:d0 + w, :] = x_ref[s0:s0 + w, :].astype(BF16)
    r = lax.broadcasted_iota(jnp.int32, g_ref.shape, 0)
    g_ref[...] = jnp.where(r < ngate, x_ref[gate0:gate0 + GATE_PAD, :], 0.0).astype(BF16)


def _mix_w_in_cast(wt, segs, gate0, ngate, tc=256):
    depth, NW, D = wt.shape
    N = sum(s[1] for s in segs)
    return pl.pallas_call(
        functools.partial(_mix_w_in_cast_kernel, segs=segs, gate0=gate0, ngate=ngate),
        grid=(depth, D // tc),
        in_specs=[pl.BlockSpec((None, NW, tc), lambda l, i: (l, 0, i))],
        out_specs=[pl.BlockSpec((None, N, tc), lambda l, i: (l, 0, i)),
                   pl.BlockSpec((None, GATE_PAD, tc), lambda l, i: (l, 0, i))],
        out_shape=[jax.ShapeDtypeStruct((depth, N, D), BF16), jax.ShapeDtypeStruct((depth, GATE_PAD, D), BF16)],
        compiler_params=_params("parallel", "parallel"),
        name="cast_mix_w_in",
    )(wt)


def _prep_params(R):
    depth, D = R['ffn1_norm_pre'].shape
    P = {}
    row = lambda a: a.astype(F32).reshape(depth, 1, a.shape[-1])
    for name in ('ffn1_norm_pre', 'ffn1_norm_post', 'mix_norm_pre', 'mix_norm_post', 'xa_norm_pre', 'xa_mem_norm',
                 'xa_norm_post', 'ffn2_norm_pre', 'ffn2_norm_post', 'mlstm_norm', 'rg_conv_b', 'rg_ba', 'rg_bx',
                 'rg_lambda', 'hg_norm'):
        P[name] = row(R[name])
    P['rg_conv_w'] = R['rg_conv_w'].astype(F32)
    P['hg_lb_logits'] = R['hg_lb_logits'].astype(F32)

    for name in ('ffn1_w_in', 'ffn1_w_out', 'ffn2_w_in', 'ffn2_w_out'):
        P[name] = R[name]

    mw = R['w_br_a'].shape[1]
    rw = R['w_br_b'].shape[1]
    hw = R['w_br_c'].shape[1]
    nh = R['mlstm_bi'].shape[1]
    splits = (mw, mw, mw, mw, nh, nh, rw, rw, hw, hw, hw, hw, D, D, D)
    pts = [0] + [int(v) for v in np.cumsum(splits)]
    runs = ((12, 15), (0, 4), (6, 12))
    segs, d0 = [], 0
    for a, b in runs:
        segs.append((pts[a], pts[b] - pts[a], d0))
        d0 += pts[b] - pts[a]
    P['w_in_main'], P['w_in_gate'] = _mix_w_in_cast(jnp.swapaxes(R['w_in'], 1, 2), tuple(segs), pts[4], 2 * nh)
    P['gate_bias'] = jnp.pad(jnp.concatenate([R['mlstm_bi'], R['mlstm_bf']], axis=-1).astype(F32),
                             ((0, 0), (0, GATE_PAD - 2 * nh))).reshape(depth, 1, GATE_PAD)

    eye = jnp.eye(RG_BLOCKS, dtype=F32)
    dense = lambda w: jnp.einsum('lnde,nm->lndme', w, eye).reshape(depth, rw, rw).astype(BF16)
    P['rg_wa'] = dense(R['rg_wa'])
    P['rg_wx'] = dense(R['rg_wx'])

    for name in ('w_br_a', 'w_br_b', 'w_br_c', 'w_out', 'xa_wq', 'xa_wk', 'xa_wv', 'xa_wo'):
        P[name] = R[name].astype(BF16)
    return P


def _ffn_sublayer(x, P, name, l, tm, bf16_weights):
    npre, npost = P[name + '_norm_pre'], P[name + '_norm_post']
    if (name, l) in bf16_weights:
        return _ffn(x, npre, *bf16_weights[(name, l)], npost, l, tm)
    y, wg, wu, wo = _ffn_first(x, npre, P[name + '_w_in'], P[name + '_w_out'], npost, l, tm)
    bf16_weights[(name, l)] = (wg, wu, wo)
    if x.shape[0] > tm:
        y = _ffn(x, npre, wg, wu, wo, npost, l, tm, done=y)
    return y


def _run_trunk(x3, states, state_has_layers, mem, mem_k, mem_v, P, bf16_weights):
    B, T, D = x3.shape
    M = B * T
    depth = P['ffn1_norm_pre'].shape[0]
    x = x3.reshape(M, D)
    tm_ffn = min(512, M)
    tm_proj = min(1024, M)
    tm_merge = min(512, M)
    tq = min(512, T)
    C0, n0, m0, hr0, buf0, S0 = states
    m0 = m0.reshape(m0.shape[0], m0.shape[1], 1, m0.shape[2])
    hr0 = hr0.reshape(hr0.shape[0], hr0.shape[1], 1, hr0.shape[2])
    mw = P['w_br_a'].shape[1]
    rw = P['w_br_b'].shape[1]
    hw = P['w_br_c'].shape[1]
    col_m = 3 * D // mw
    col_r = (3 * D + 4 * mw) // rw
    col_h = (3 * D + 4 * mw + 2 * rw) // hw

    if mem is not None:
        mem_k, mem_v, att_k, att_v = _mem_kv(mem, P, XA_HEADS)
    else:
        att_k, att_v = mem_k, mem_v

    outs = [[] for _ in range(6)]
    for l in range(depth):
        ls = l if state_has_layers else 0
        x = _ffn_sublayer(x, P, 'ffn1', l, tm_ffn, bf16_weights)
        z, gates = _mix_in(x, P['mix_norm_pre'], P['w_in_main'], P['w_in_gate'], l, tm_proj, 1024)
        ya, C1, n1, m1 = _mlstm(z, gates, P['gate_bias'], P['mlstm_norm'], C0, n0, m0, l, ls, B, T, col_m)
        yb, buf1, hr1 = _rglru(z, P, buf0, hr0, l, ls, B, T, col_r)
        yc, S1 = _hgrn(z, P['hg_lb_logits'], P['hg_norm'], S0, l, ls, B, T, col_h)
        x = _merge(x, ya, yb, yc, z, P, l, tm_merge)
        x = _xattn(x, att_k, att_v, P, l, l, B, T, tq)
        x = _ffn_sublayer(x, P, 'ffn2', l, tm_ffn, bf16_weights)
        for lst, s in zip(outs, (C1, n1, m1.reshape(B, -1), hr1.reshape(B, -1), buf1, S1)):
            lst.append(s)
    stacked = tuple(jnp.stack(lst) for lst in outs)
    return x.reshape(B, T, D), stacked, mem_k, mem_v


def kernel(x_prompt, x_sample, mem_prompt, cache_mem_k, cache_mem_v, state_mlstm_C, state_mlstm_n, state_mlstm_m, state_rglru_h, state_rglru_conv, state_hgrn_S, ffn1_norm_pre, ffn1_w_in, ffn1_w_out, ffn1_norm_post, mix_norm_pre, w_in, mlstm_bi, mlstm_bf, mlstm_norm, rg_conv_w, rg_conv_b, rg_wa, rg_ba, rg_wx, rg_bx, rg_lambda, hg_lb_logits, hg_norm, w_br_a, w_br_b, w_br_c, w_out, mix_norm_post, xa_norm_pre, xa_mem_norm, xa_wq, xa_wk, xa_wv, xa_wo, xa_norm_post, ffn2_norm_pre, ffn2_w_in, ffn2_w_out, ffn2_norm_post):
    R = dict(ffn1_norm_pre=ffn1_norm_pre, ffn1_w_in=ffn1_w_in, ffn1_w_out=ffn1_w_out, ffn1_norm_post=ffn1_norm_post,
             mix_norm_pre=mix_norm_pre, w_in=w_in, mlstm_bi=mlstm_bi, mlstm_bf=mlstm_bf, mlstm_norm=mlstm_norm,
             rg_conv_w=rg_conv_w, rg_conv_b=rg_conv_b, rg_wa=rg_wa, rg_ba=rg_ba, rg_wx=rg_wx, rg_bx=rg_bx,
             rg_lambda=rg_lambda, hg_lb_logits=hg_lb_logits, hg_norm=hg_norm,
             w_br_a=w_br_a, w_br_b=w_br_b, w_br_c=w_br_c, w_out=w_out, mix_norm_post=mix_norm_post,
             xa_norm_pre=xa_norm_pre, xa_mem_norm=xa_mem_norm, xa_wq=xa_wq, xa_wk=xa_wk, xa_wv=xa_wv,
             xa_wo=xa_wo, xa_norm_post=xa_norm_post,
             ffn2_norm_pre=ffn2_norm_pre, ffn2_w_in=ffn2_w_in, ffn2_w_out=ffn2_w_out, ffn2_norm_post=ffn2_norm_post)
    P = _prep_params(R)
    bf16_weights = {}
    B = x_prompt.shape[0]
    zeros_like_state = lambda s: jnp.zeros((1, B) + s.shape[2:], F32)
    init = tuple(zeros_like_state(s) for s in (state_mlstm_C, state_mlstm_n, state_mlstm_m, state_rglru_h,
                                               state_rglru_conv, state_hgrn_S))
    y_prompt, p_states, p_mem_k, p_mem_v = _run_trunk(x_prompt, init, False, mem_prompt, None, None, P, bf16_weights)
    s_init = (state_mlstm_C, state_mlstm_n, state_mlstm_m, state_rglru_h, state_rglru_conv, state_hgrn_S)
    y_sample, s_states, _, _ = _run_trunk(x_sample, s_init, True, None, cache_mem_k, cache_mem_v, P, bf16_weights)
    return (y_prompt, y_sample) + p_states + (p_mem_k, p_mem_v) + s_states
```

```python
import functools

import jax
import jax.numpy as jnp
import numpy as np
from jax import lax
from jax.experimental import pallas as pl
from jax.experimental.pallas import tpu as pltpu

F32 = jnp.float32
BF16 = jnp.bfloat16
EPS = 1e-6

V7X_VMEM_LIMIT_BYTES = 56 * 1024 * 1024
LANES = 128

XA_HEADS = 4
MLSTM_HEADS = 4
HG_HEADS = 4
RG_BLOCKS = 8
RG_C = 8.0
CONV_W = 4
D_FF_TILE = 512
FFN_ROWS = 512
XA_ROW_GROUPS = 2
MERGE_ROW_GROUPS = 2
MLSTM_CHUNK = 256
RG_CHUNK = 256
HG_CHUNK = 128
GATE_PAD = LANES


def _params(*sem):
    return pltpu.CompilerParams(dimension_semantics=sem, vmem_limit_bytes=V7X_VMEM_LIMIT_BYTES)


def _rms(x, w):
    return x * lax.rsqrt(jnp.mean(x * x, axis=-1, keepdims=True) + EPS) * w


def _sigmoid(x):
    return 0.5 * jnp.tanh(0.5 * x) + 0.5


def _log_sigmoid(x):
    return jnp.minimum(x, 0.0) - jnp.log1p(jnp.exp(-jnp.abs(x)))


def _softplus(x):
    return jnp.maximum(x, 0.0) + jnp.log1p(jnp.exp(-jnp.abs(x)))


def _dot(a, b):
    return jnp.dot(a, b, preferred_element_type=F32)


def _dot_nt(a, b):
    return lax.dot_general(a, b, (((1,), (1,)), ((), ())), preferred_element_type=F32)


def _dot_tn(a, b):
    return lax.dot_general(a, b, (((0,), (0,)), ((), ())), preferred_element_type=F32)


def _shift_rows(x, d, fill, ridx):
    return jnp.where(ridx >= d, pltpu.roll(x, d, 0), fill)


def _ffn_step(j, nj, x_ref, npre_ref, weights, npost_ref, o_ref, hn_ref, acc_ref):
    @pl.when(j == 0)
    def _():
        hn_ref[...] = _rms(x_ref[...], npre_ref[...]).astype(BF16)
        acc_ref[...] = jnp.zeros_like(acc_ref)

    wg, wu, wo = weights()
    hn = hn_ref[...]
    g = _dot(hn, wg)
    u = _dot(hn, wu)
    a = (g * _sigmoid(g) * u).astype(BF16)
    acc_ref[...] += _dot(a, wo)

    @pl.when(j == nj - 1)
    def _():
        o_ref[...] = x_ref[...] + 0.5 * _rms(acc_ref[...], npost_ref[...])


def _ffn_kernel(x_ref, npre_ref, wg_ref, wu_ref, wo_ref, npost_ref, *rest, has_first):
    o_ref, hn_ref, acc_ref = rest[-3:]
    i, j, nj = pl.program_id(0), pl.program_id(1), pl.num_programs(1)

    def compute():
        _ffn_step(j, nj, x_ref, npre_ref, lambda: (wg_ref[...], wu_ref[...], wo_ref[...]), npost_ref,
                  o_ref, hn_ref, acc_ref)

    if not has_first:
        compute()
        return
    pl.when(i > 0)(compute)

    @pl.when((i == 0) & (j == nj - 1))
    def _():
        o_ref[...] = rest[0][...]


def _ffn(x, npre, wg, wu, wo, npost, l, tm, first=None):
    M, D = x.shape
    Fp = wo.shape[0]
    tn = D_FF_TILE
    nj = Fp // tn
    has_first = first is not None
    tile = (lambda i, j: jnp.where(i == 0, 0, j)) if has_first else (lambda i, j: j)
    args = [x, npre, wg, wu, wo, npost]
    in_specs = [
        pl.BlockSpec((tm, D), lambda i, j: (i, 0)),
        pl.BlockSpec((None, 1, D), lambda i, j: (l, 0, 0)),
        pl.BlockSpec((D, tn), lambda i, j: (0, tile(i, j))),
        pl.BlockSpec((D, tn), lambda i, j: (0, tile(i, j))),
        pl.BlockSpec((tn, D), lambda i, j: (tile(i, j), 0)),
        pl.BlockSpec((None, 1, D), lambda i, j: (l, 0, 0)),
    ]
    if has_first:
        args.append(first)
        in_specs.append(pl.BlockSpec((tm, D), lambda i, j: (0, 0)))
    return pl.pallas_call(
        functools.partial(_ffn_kernel, has_first=has_first),
        grid=(M // tm, nj),
        in_specs=in_specs,
        out_specs=pl.BlockSpec((tm, D), lambda i, j: (i, 0)),
        out_shape=jax.ShapeDtypeStruct((M, D), F32),
        scratch_shapes=[pltpu.VMEM((tm, D), BF16), pltpu.VMEM((tm, D), F32)],
        compiler_params=_params("parallel", "arbitrary"),
        name="ffn",
    )(*args)


def _ffn_first_kernel(x_ref, npre_ref, g32_ref, u0_ref, u32_ref, wo32_ref, npost_ref,
                      o_ref, wg_ref, wu_ref, wo_ref, hn_ref, acc_ref, uprev_ref, *, F, shift):
    j = pl.program_id(0)
    tn = wg_ref.shape[-1]

    def weights():
        col_ok = lax.broadcasted_iota(jnp.int32, (1, tn), 1) + j * tn < F
        row_ok = lax.broadcasted_iota(jnp.int32, (tn, 1), 0) + j * tn < F
        if shift:
            @pl.when(j == 0)
            def _():
                uprev_ref[...] = u0_ref[...]

            u32 = jnp.concatenate([uprev_ref[:, shift:], u32_ref[:, :shift]], axis=1)
            uprev_ref[...] = u32_ref[...]
        else:
            u32 = u32_ref[...]
        wg = jnp.where(col_ok, g32_ref[...], 0.0).astype(BF16)
        wu = jnp.where(col_ok, u32, 0.0).astype(BF16)
        wo = jnp.where(row_ok, wo32_ref[...], 0.0).astype(BF16)
        wg_ref[...] = wg
        wu_ref[...] = wu
        wo_ref[...] = wo
        return wg, wu, wo

    _ffn_step(j, pl.num_programs(0), x_ref, npre_ref, weights, npost_ref, o_ref, hn_ref, acc_ref)


def _ffn_first(x, npre, w_in, w_out, npost, l, tm, tn=256):
    D = x.shape[1]
    F = w_out.shape[1]
    Fp = -(-F // D_FF_TILE) * D_FF_TILE
    nj = Fp // tn
    q, shift = divmod(F, tn)
    last_in = -(-2 * F // tn) - 1
    return pl.pallas_call(
        functools.partial(_ffn_first_kernel, F=F, shift=shift),
        grid=(nj,),
        in_specs=[
            pl.BlockSpec((tm, D), lambda j: (0, 0)),
            pl.BlockSpec((None, 1, D), lambda j: (l, 0, 0)),
            pl.BlockSpec((None, D, tn), lambda j: (l, 0, j)),
            pl.BlockSpec((None, D, tn), lambda j: (l, 0, q)),
            pl.BlockSpec((None, D, tn), lambda j: (l, 0, jnp.minimum(q + j + (1 if shift else 0), last_in))),
            pl.BlockSpec((None, tn, D), lambda j: (l, j, 0)),
            pl.BlockSpec((None, 1, D), lambda j: (l, 0, 0)),
        ],
        out_specs=[
            pl.BlockSpec((tm, D), lambda j: (0, 0)),
            pl.BlockSpec((D, tn), lambda j: (0, j)),
            pl.BlockSpec((D, tn), lambda j: (0, j)),
            pl.BlockSpec((tn, D), lambda j: (j, 0)),
        ],
        out_shape=[
            jax.ShapeDtypeStruct((tm, D), F32),
            jax.ShapeDtypeStruct((D, Fp), BF16),
            jax.ShapeDtypeStruct((D, Fp), BF16),
            jax.ShapeDtypeStruct((Fp, D), BF16),
        ],
        scratch_shapes=[pltpu.VMEM((tm, D), BF16), pltpu.VMEM((tm, D), F32), pltpu.VMEM((D, tn), F32)],
        compiler_params=_params("arbitrary"),
        name="ffn_first",
    )(x, npre, w_in, w_in, w_in, w_out, npost)


def _nmm_kernel(x_ref, nw_ref, w_ref, o_ref, hn_ref):
    @pl.when(pl.program_id(1) == 0)
    def _():
        hn_ref[...] = _rms(x_ref[...], nw_ref[...]).astype(BF16)

    o_ref[...] = _dot(hn_ref[...], w_ref[...]).astype(o_ref.dtype)


def _norm_matmul(x, nw, w, l, out_dtype, tm, tn, name):
    M, D = x.shape
    N = w.shape[-1]
    return pl.pallas_call(
        _nmm_kernel,
        grid=(M // tm, N // tn),
        in_specs=[
            pl.BlockSpec((tm, D), lambda i, j: (i, 0)),
            pl.BlockSpec((None, 1, D), lambda i, j: (l, 0, 0)),
            pl.BlockSpec((None, D, tn), lambda i, j: (l, 0, j)),
        ],
        out_specs=pl.BlockSpec((tm, tn), lambda i, j: (i, j)),
        out_shape=jax.ShapeDtypeStruct((M, N), out_dtype),
        scratch_shapes=[pltpu.VMEM((tm, D), BF16)],
        compiler_params=_params("parallel", "arbitrary"),
        name=name,
    )(x, nw, w)


def _mix_in_kernel(x_ref, nw_ref, w_ref, wg_ref, z_ref, g_ref, hn_ref):
    @pl.when(pl.program_id(1) == 0)
    def _():
        hn = _rms(x_ref[...], nw_ref[...]).astype(BF16)
        hn_ref[...] = hn
        g_ref[...] = _dot_nt(hn, wg_ref[...])

    z_ref[...] = _dot_nt(hn_ref[...], w_ref[...]).astype(z_ref.dtype)


def _mix_in(x, nw, w, wg, l, tm, tn):
    M, D = x.shape
    N = w.shape[-2]
    G = wg.shape[-2]
    return pl.pallas_call(
        _mix_in_kernel,
        grid=(M // tm, N // tn),
        in_specs=[
            pl.BlockSpec((tm, D), lambda i, j: (i, 0)),
            pl.BlockSpec((None, 1, D), lambda i, j: (l, 0, 0)),
            pl.BlockSpec((None, tn, D), lambda i, j: (l, j, 0)),
            pl.BlockSpec((None, G, D), lambda i, j: (l, 0, 0)),
        ],
        out_specs=[pl.BlockSpec((tm, tn), lambda i, j: (i, j)), pl.BlockSpec((tm, G), lambda i, j: (i, 0))],
        out_shape=[jax.ShapeDtypeStruct((M, N), BF16), jax.ShapeDtypeStruct((M, G), F32)],
        scratch_shapes=[pltpu.VMEM((tm, D), BF16)],
        compiler_params=_params("parallel", "arbitrary"),
        name="mix_in",
    )(x, nw, w, wg)


def _mlstm_kernel(q_ref, k_ref, v_ref, og_ref, g_ref, gb_ref, nw_ref, C0_ref, n0_ref, m0_ref,
                  hm_ref, C_ref, n_ref, m_ref, *, L, dh):
    H = MLSTM_HEADS

    @pl.when(pl.program_id(1) == 0)
    def _():
        C_ref[...] = C0_ref[...]
        n_ref[...] = n0_ref[...]
        m_ref[...] = m0_ref[...]

    ga = g_ref[...] + gb_ref[...]
    m_prev = m_ref[0]
    row = lax.broadcasted_iota(jnp.int32, (L, L), 0)
    col = lax.broadcasted_iota(jnp.int32, (L, L), 1)
    causal = row >= col
    eye = row == col
    gaT = ga.T if L % LANES == 0 else None

    def as_row(x_col):
        return jnp.sum(jnp.where(eye, x_col, 0.0), axis=0, keepdims=True)

    for h in range(H):
        hs = slice(h * dh, (h + 1) * dh)
        ig_c = ga[:, h:h + 1]
        lf_c = _log_sigmoid(ga[:, H + h:H + h + 1])
        if gaT is not None:
            ig_r = gaT[h:h + 1, :]
            lf_r = _log_sigmoid(gaT[H + h:H + h + 1, :])
        else:
            ig_r = as_row(ig_c)
            lf_r = as_row(lf_c)
        b_c = jnp.sum(jnp.where(causal, lf_r, 0.0), axis=1, keepdims=True)
        b_r = jnp.sum(jnp.where(row <= col, lf_c, 0.0), axis=0, keepdims=True)
        m0 = m_prev[:, h:h + 1]
        src_r = ig_r - b_r
        peak_c = jnp.maximum(m0, jnp.max(jnp.where(causal, src_r, -jnp.inf), axis=1, keepdims=True))
        m_c = b_c + peak_c
        w = jnp.exp(jnp.where(causal, src_r - peak_c, -jnp.inf))
        w_inter = jnp.exp(m0 - peak_c)

        q = q_ref[:, hs]
        k = k_ref[:, hs] * (dh ** -0.5)
        v = v_ref[:, hs]
        C0 = C_ref[0, h]
        n0 = n_ref[0, h:h + 1, :]
        wqk = w * _dot_nt(q, k)
        num = _dot(wqk.astype(BF16), v) + w_inter * _dot(q, C0.astype(BF16))
        den = (jnp.sum(wqk, axis=1, keepdims=True)
               + w_inter * jnp.sum(q.astype(F32) * n0, axis=1, keepdims=True))
        hh = num / jnp.maximum(jnp.abs(den), jnp.exp(-m_c))
        hn = hh * lax.rsqrt(jnp.mean(hh * hh, axis=-1, keepdims=True) + EPS) * nw_ref[:, hs]
        hm_ref[:, hs] = (_sigmoid(og_ref[:, hs].astype(F32)) * hn).astype(BF16)

        m_end = m_c[L - 1:L, :]
        w_end = jnp.exp(b_c[L - 1:L, :] - b_c + ig_c - m_end)
        s_end = w_inter[L - 1:L, :]
        kw = k.astype(F32) * w_end
        C_ref[0, h] = s_end * C0 + _dot_tn(kw.astype(BF16), v)
        n_ref[0, h:h + 1, :] = s_end * n0 + jnp.sum(kw, axis=0, keepdims=True)
        m_ref[0, :, h:h + 1] = m_end


def _mlstm(z, gates, gate_bias, norm_w, C0, n0, m0, l, ls, B, T, col0):
    H = MLSTM_HEADS
    dh = C0.shape[-1]
    W = H * dh
    L = min(MLSTM_CHUNK, T)
    nC = T // L
    M = B * T
    zspec = lambda cb: pl.BlockSpec((L, W), lambda b, c: (b * nC + c, cb))
    return pl.pallas_call(
        functools.partial(_mlstm_kernel, L=L, dh=dh),
        grid=(B, nC),
        in_specs=[
            zspec(col0), zspec(col0 + 1), zspec(col0 + 2), zspec(col0 + 3),
            pl.BlockSpec((L, GATE_PAD), lambda b, c: (b * nC + c, 0)),
            pl.BlockSpec((None, 1, GATE_PAD), lambda b, c: (l, 0, 0)),
            pl.BlockSpec((None, 1, W), lambda b, c: (l, 0, 0)),
            pl.BlockSpec((None, 1, H, dh, dh), lambda b, c: (ls, b, 0, 0, 0)),
            pl.BlockSpec((None, 1, H, dh), lambda b, c: (ls, b, 0, 0)),
            pl.BlockSpec((None, 1, 1, H), lambda b, c: (ls, b, 0, 0)),
        ],
        out_specs=[
            pl.BlockSpec((L, W), lambda b, c: (b * nC + c, 0)),
            pl.BlockSpec((1, H, dh, dh), lambda b, c: (b, 0, 0, 0)),
            pl.BlockSpec((1, H, dh), lambda b, c: (b, 0, 0)),
            pl.BlockSpec((1, 1, H), lambda b, c: (b, 0, 0)),
        ],
        out_shape=[
            jax.ShapeDtypeStruct((M, W), BF16),
            jax.ShapeDtypeStruct((B, H, dh, dh), F32),
            jax.ShapeDtypeStruct((B, H, dh), F32),
            jax.ShapeDtypeStruct((B, 1, H), F32),
        ],
        compiler_params=_params("parallel", "arbitrary"),
        name="mlstm",
    )(z, z, z, z, gates, gate_bias, norm_w, C0, n0, m0)


def _rglru_kernel(rx_ref, rg_ref, cw_ref, cbias_ref, wa_ref, ba_ref, wx_ref, bx_ref, lam_ref, buf0_ref, h0_ref,
                  y_ref, buf_ref, h_ref, cbuf_ref, *, L):
    c = pl.program_id(1)
    W = rx_ref.shape[-1]
    TAIL = CONV_W - 1

    @pl.when(c == 0)
    def _():
        cbuf_ref[0:8, :] = jnp.zeros((8, W), F32)
        cbuf_ref[8 - TAIL:8, :] = buf0_ref[0]
        h_ref[...] = h0_ref[...]

    cbuf_ref[8:8 + L, :] = rx_ref[...].astype(F32)
    xc = cbias_ref[...]
    for j in range(CONV_W):
        xc = xc + cbuf_ref[8 - TAIL + j:8 - TAIL + j + L, :] * cw_ref[j:j + 1, :]

    xcb = xc.astype(BF16)
    r = _sigmoid(_dot(xcb, wa_ref[...]) + ba_ref[...])
    i = _sigmoid(_dot(xcb, wx_ref[...]) + bx_ref[...])
    log_a = (-RG_C * _softplus(-lam_ref[...])) * r
    a = jnp.exp(log_a)
    th = jnp.tanh(log_a)
    u = jnp.sqrt(-2.0 * th / (1.0 - th)) * (i * xc)

    ridx = lax.broadcasted_iota(jnp.int32, (L, W), 0)
    d = 1
    while d < L:
        a_sh = _shift_rows(a, d, 1.0, ridx)
        u_sh = _shift_rows(u, d, 0.0, ridx)
        u = a * u_sh + u
        a = a * a_sh
        d *= 2
    h = a * h_ref[0] + u
    h_ref[0] = h[L - 1:L, :]
    y_ref[...] = (h * jax.nn.gelu(rg_ref[...].astype(F32))).astype(BF16)

    cbuf_ref[0:8, :] = cbuf_ref[L:L + 8, :]

    @pl.when(c == pl.num_programs(1) - 1)
    def _():
        buf_ref[0] = cbuf_ref[8 + L - TAIL:8 + L, :]


def _rglru(z, P, buf0, h0, l, ls, B, T, col0):
    W = h0.shape[-1]
    L = min(RG_CHUNK, T)
    nC = T // L
    M = B * T
    vec = lambda: pl.BlockSpec((None, 1, W), lambda b, c: (l, 0, 0))
    mat = lambda: pl.BlockSpec((None, W, W), lambda b, c: (l, 0, 0))
    return pl.pallas_call(
        functools.partial(_rglru_kernel, L=L),
        grid=(B, nC),
        in_specs=[
            pl.BlockSpec((L, W), lambda b, c: (b * nC + c, col0)),
            pl.BlockSpec((L, W), lambda b, c: (b * nC + c, col0 + 1)),
            pl.BlockSpec((None, CONV_W, W), lambda b, c: (l, 0, 0)),
            vec(), mat(), vec(), mat(), vec(), vec(),
            pl.BlockSpec((None, 1, CONV_W - 1, W), lambda b, c: (ls, b, 0, 0)),
            pl.BlockSpec((None, 1, 1, W), lambda b, c: (ls, b, 0, 0)),
        ],
        out_specs=[
            pl.BlockSpec((L, W), lambda b, c: (b * nC + c, 0)),
            pl.BlockSpec((1, CONV_W - 1, W), lambda b, c: (b, 0, 0)),
            pl.BlockSpec((1, 1, W), lambda b, c: (b, 0, 0)),
        ],
        out_shape=[
            jax.ShapeDtypeStruct((M, W), BF16),
            jax.ShapeDtypeStruct((B, CONV_W - 1, W), F32),
            jax.ShapeDtypeStruct((B, 1, W), F32),
        ],
        scratch_shapes=[pltpu.VMEM((L + 8, W), F32)],
        compiler_params=_params("parallel", "arbitrary"),
        name="rglru",
    )(z, z, P['rg_conv_w'], P['rg_conv_b'], P['rg_wa'], P['rg_ba'], P['rg_wx'], P['rg_bx'], P['rg_lambda'],
      buf0, h0)


def _block_ref_rows(cb, m, ridx):
    L, W = cb.shape
    n = 2 * m
    if m == 1:
        return jnp.where((ridx & 1) == 0, cb, pltpu.roll(cb, 1, 0))
    if m == 2:
        off = ridx & 3
        return jnp.where(off == 0, pltpu.roll(cb, L - 1, 0),
                         jnp.where(off == 1, cb, jnp.where(off == 2, pltpu.roll(cb, 1, 0), pltpu.roll(cb, 2, 0))))
    return jnp.concatenate(
        [jnp.broadcast_to(cb[j * n + m - 1:j * n + m, :], (n, W)) for j in range(L // n)], axis=0)


def _hgrn_kernel(q_ref, f_ref, i_ref, g_ref, lbl_ref, nw_ref, S0_ref, y_ref, S_ref, *, L, layer):
    H = HG_HEADS
    W = q_ref.shape[-1]
    dk = W // H

    @pl.when(pl.program_id(1) == 0)
    def _():
        S_ref[...] = S0_ref[...]

    lg = lbl_ref[...]
    e = jnp.exp(lg - jnp.max(lg, axis=0, keepdims=True))
    p = e / jnp.sum(e, axis=0, keepdims=True)
    lb = jnp.zeros((1, W), F32)
    for r in range(1, layer + 1):
        lb = lb + p[r:r + 1, :]

    f = lb + (1.0 - lb) * jax.nn.sigmoid(f_ref[...].astype(F32))
    cb = jnp.log(f)
    ridx = lax.broadcasted_iota(jnp.int32, (L, W), 0)
    d = 1
    while d < L:
        cb = cb + _shift_rows(cb, d, 0.0, ridx)
        d *= 2
    kk = 1.0 - f
    q = q_ref[...].astype(F32)
    v = i_ref[...].astype(F32)
    vb = v.astype(BF16)

    row = lax.broadcasted_iota(jnp.int32, (L, L), 0)
    col = lax.broadcasted_iota(jnp.int32, (L, L), 1)
    split = jnp.where(row > col, row ^ col, 0)
    a = [jnp.zeros((L, L), F32) for _ in range(H)]
    m = 1
    while m < L:
        dref = cb - _block_ref_rows(cb, m, ridx)
        dec = jnp.exp(-jnp.abs(dref))
        qt = (q * dec).astype(BF16)
        ks = (kk * dec).astype(BF16)
        level = (split // m) == 1
        for h in range(H):
            hs = slice(h * dk, (h + 1) * dk)
            a[h] = jnp.where(level, _dot_nt(qt[:, hs], ks[:, hs]), a[h])
        m *= 2

    cb_end = cb[L - 1:L, :]
    qe = (q * jnp.exp(cb)).astype(BF16)
    ke = (kk * jnp.exp(cb_end - cb)).astype(BF16)
    qk = q * kk
    eye = (lax.broadcasted_iota(jnp.int32, (dk, dk), 0) == lax.broadcasted_iota(jnp.int32, (dk, dk), 1))
    outs = []
    for h in range(H):
        hs = slice(h * dk, (h + 1) * dk)
        S = S_ref[0, h]
        o = (_dot(a[h].astype(BF16), vb[:, hs]) + jnp.sum(qk[:, hs], axis=1, keepdims=True) * v[:, hs]
             + _dot(qe[:, hs], S.astype(BF16)))
        dec_col = jnp.sum(jnp.where(eye, jnp.exp(cb_end[:, hs]), 0.0), axis=1, keepdims=True)
        S_ref[0, h] = dec_col * S + _dot_tn(ke[:, hs], vb[:, hs])
        outs.append(o * lax.rsqrt(jnp.mean(o * o, axis=-1, keepdims=True) + EPS))
    gate = g_ref[...].astype(F32)
    y_ref[...] = (jnp.concatenate(outs, axis=1) * nw_ref[...] * (gate * _sigmoid(gate))).astype(BF16)


def _hgrn(z, lb_logits, norm_w, S0, l, ls, B, T, col0):
    H, dk, dv = S0.shape[-3:]
    W = H * dk
    L = min(HG_CHUNK, T)
    nC = T // L
    M = B * T
    depth = lb_logits.shape[0]
    zspec = lambda cb: pl.BlockSpec((L, W), lambda b, c: (b * nC + c, cb))
    return pl.pallas_call(
        functools.partial(_hgrn_kernel, L=L, layer=l),
        grid=(B, nC),
        in_specs=[
            zspec(col0), zspec(col0 + 1), zspec(col0 + 2), zspec(col0 + 3),
            pl.BlockSpec((depth, W), lambda b, c: (0, 0)),
            pl.BlockSpec((None, 1, W), lambda b, c: (l, 0, 0)),
            pl.BlockSpec((None, 1, H, dk, dv), lambda b, c: (ls, b, 0, 0, 0)),
        ],
        out_specs=[
            pl.BlockSpec((L, W), lambda b, c: (b * nC + c, 0)),
            pl.BlockSpec((1, H, dk, dv), lambda b, c: (b, 0, 0, 0)),
        ],
        out_shape=[
            jax.ShapeDtypeStruct((M, W), BF16),
            jax.ShapeDtypeStruct((B, H, dk, dv), F32),
        ],
        compiler_params=_params("parallel", "arbitrary"),
        name="hgrn",
    )(z, z, z, z, lb_logits, norm_w, S0)


def _merge_kernel(a_ref, b_ref, c_ref, ga_ref, gb_ref, gc_ref, wa_ref, wb_ref, wc_ref, wo_ref, npost_ref, x_ref,
                  o_ref):
    tm = x_ref.shape[0]
    rows = tm // MERGE_ROW_GROUPS if tm % (16 * MERGE_ROW_GROUPS) == 0 else tm
    for r0 in range(0, tm, rows):
        rs = slice(r0, r0 + rows)
        sig = lambda r: _sigmoid(r[rs, :].astype(F32))
        merged = (sig(ga_ref) * _dot(a_ref[rs, :], wa_ref[...])
                  + sig(gb_ref) * _dot(b_ref[rs, :], wb_ref[...])
                  + sig(gc_ref) * _dot(c_ref[rs, :], wc_ref[...]))
        out = _dot(merged.astype(BF16), wo_ref[...])
        o_ref[rs, :] = x_ref[rs, :] + _rms(out, npost_ref[...])


def _merge(x, ya, yb, yc, z, P, l, tm):
    M, D = x.shape
    rows = lambda w: pl.BlockSpec((tm, w), lambda i: (i, 0))
    wspec = lambda k: pl.BlockSpec((None, k, D), lambda i: (l, 0, 0), pipeline_mode=pl.Buffered(1))
    return pl.pallas_call(
        _merge_kernel,
        grid=(M // tm,),
        in_specs=[
            rows(ya.shape[1]), rows(yb.shape[1]), rows(yc.shape[1]),
            pl.BlockSpec((tm, D), lambda i: (i, 0)),
            pl.BlockSpec((tm, D), lambda i: (i, 1)),
            pl.BlockSpec((tm, D), lambda i: (i, 2)),
            wspec(ya.shape[1]), wspec(yb.shape[1]), wspec(yc.shape[1]), wspec(D),
            pl.BlockSpec((None, 1, D), lambda i: (l, 0, 0)),
            rows(D),
        ],
        out_specs=rows(D),
        out_shape=jax.ShapeDtypeStruct((M, D), F32),
        compiler_params=_params("parallel"),
        name="merge",
    )(ya, yb, yc, z, z, z, P['w_br_a'], P['w_br_b'], P['w_br_c'], P['w_out'], P['mix_norm_post'], x)


def _xattn_kernel(x_ref, npre_ref, wq_ref, k_ref, v_ref, wo_ref, npost_ref, o_ref, *kv_scratch):
    D = x_ref.shape[-1]
    dh = D // XA_HEADS
    if kv_scratch:
        kb_ref, vb_ref = kv_scratch

        @pl.when(pl.program_id(1) == 0)
        def _():
            for h in range(XA_HEADS):
                kb_ref[:, h * dh:(h + 1) * dh] = k_ref[:, h, :].astype(BF16)
                vb_ref[:, h * dh:(h + 1) * dh] = v_ref[:, h, :].astype(BF16)
    else:
        kb_ref, vb_ref = k_ref, v_ref

    tq = x_ref.shape[0]
    rows = tq // XA_ROW_GROUPS if tq % (8 * XA_ROW_GROUPS) == 0 else tq
    for r0 in range(0, tq, rows):
        x = x_ref[r0:r0 + rows, :]
        q = _dot(_rms(x, npre_ref[...]).astype(BF16), wq_ref[...]).astype(BF16)
        outs = []
        for h in range(XA_HEADS):
            hs = slice(h * dh, (h + 1) * dh)
            s = _dot_nt(q[:, hs], kb_ref[:, hs]) * (dh ** -0.5)
            e = jnp.exp(s - jnp.max(s, axis=-1, keepdims=True))
            p = e / jnp.sum(e, axis=-1, keepdims=True)
            outs.append(_dot(p.astype(BF16), vb_ref[:, hs]).astype(BF16))
        out = _dot(jnp.concatenate(outs, axis=1), wo_ref[...])
        o_ref[r0:r0 + rows, :] = x + _rms(out, npost_ref[...])


def _xattn(x, mem_k, mem_v, P, l, lk, B, T, tq):
    M, D = x.shape
    n_mem = mem_k.shape[2]
    nT = T // tq
    wspec = lambda: pl.BlockSpec((None, D, D), lambda b, t: (l, 0, 0), pipeline_mode=pl.Buffered(1))
    nspec = lambda: pl.BlockSpec((None, 1, D), lambda b, t: (l, 0, 0))
    kv_block = (None, None) + mem_k.shape[2:]
    kv_index = lambda b, t: (lk, b) + (0,) * (mem_k.ndim - 2)
    kvspec = lambda: pl.BlockSpec(kv_block, kv_index)
    scratch = [] if mem_k.dtype == BF16 else [pltpu.VMEM((n_mem, D), BF16), pltpu.VMEM((n_mem, D), BF16)]
    return pl.pallas_call(
        _xattn_kernel,
        grid=(B, nT),
        in_specs=[
            pl.BlockSpec((tq, D), lambda b, t: (b * nT + t, 0)),
            nspec(), wspec(), kvspec(), kvspec(), wspec(), nspec(),
        ],
        out_specs=pl.BlockSpec((tq, D), lambda b, t: (b * nT + t, 0)),
        out_shape=jax.ShapeDtypeStruct((M, D), F32),
        scratch_shapes=scratch,
        compiler_params=_params("parallel", "arbitrary"),
        name="xattn",
    )(x, P['xa_norm_pre'], P['xa_wq'], mem_k, mem_v, P['xa_wo'], P['xa_norm_post'])


def _mem_kv_kernel(m_ref, nw_ref, wk_ref, wv_ref, k_ref, v_ref, kb_ref, vb_ref):
    H, dh = k_ref.shape[-2:]
    hn = _rms(m_ref[...], nw_ref[...]).astype(BF16)
    k = _dot(hn, wk_ref[...])
    v = _dot(hn, wv_ref[...])
    kb_ref[...] = k.astype(BF16)
    vb_ref[...] = v.astype(BF16)
    for h in range(H):
        k_ref[:, h, :] = k[:, h * dh:(h + 1) * dh]
        v_ref[:, h, :] = v[:, h * dh:(h + 1) * dh]


def _mem_kv(mem, P, H):
    B, n_mem, D = mem.shape
    depth = P['xa_wk'].shape[0]
    out = jax.ShapeDtypeStruct((depth, B, n_mem, H, D // H), F32)
    outb = jax.ShapeDtypeStruct((depth, B, n_mem, D), BF16)
    ospec = lambda: pl.BlockSpec((None, None, n_mem, H, D // H), lambda l, b: (l, b, 0, 0, 0))
    obspec = lambda: pl.BlockSpec((None, None, n_mem, D), lambda l, b: (l, b, 0, 0))
    wspec = lambda: pl.BlockSpec((None, D, D), lambda l, b: (l, 0, 0))
    return pl.pallas_call(
        _mem_kv_kernel,
        grid=(depth, B),
        in_specs=[
            pl.BlockSpec((None, n_mem, D), lambda l, b: (b, 0, 0)),
            pl.BlockSpec((None, 1, D), lambda l, b: (l, 0, 0)),
            wspec(), wspec(),
        ],
        out_specs=[ospec(), ospec(), obspec(), obspec()],
        out_shape=[out, out, outb, outb],
        compiler_params=_params("parallel", "arbitrary"),
        name="mem_kv",
    )(mem, P['xa_mem_norm'], P['xa_wk'], P['xa_wv'])


def _mix_w_in_cast_kernel(x_ref, o_ref, g_ref, *, segs, gate0, ngate):
    for s0, w, d0 in segs:
        o_ref[d0:d0 + w, :] = x_ref[s0:s0 + w, :].astype(BF16)
    r = lax.broadcasted_iota(jnp.int32, g_ref.shape, 0)
    g_ref[...] = jnp.where(r < ngate, x_ref[gate0:gate0 + GATE_PAD, :], 0.0).astype(BF16)


def _mix_w_in_cast(wt, segs, gate0, ngate, tc=256):
    depth, NW, D = wt.shape
    N = sum(s[1] for s in segs)
    return pl.pallas_call(
        functools.partial(_mix_w_in_cast_kernel, segs=segs, gate0=gate0, ngate=ngate),
        grid=(depth, D // tc),
        in_specs=[pl.BlockSpec((None, NW, tc), lambda l, i: (l, 0, i))],
        out_specs=[pl.BlockSpec((None, N, tc), lambda l, i: (l, 0, i)),
                   pl.BlockSpec((None, GATE_PAD, tc), lambda l, i: (l, 0, i))],
        out_shape=[jax.ShapeDtypeStruct((depth, N, D), BF16), jax.ShapeDtypeStruct((depth, GATE_PAD, D), BF16)],
        compiler_params=_params("parallel", "parallel"),
        name="cast_mix_w_in",
    )(wt)


def _prep_params(R):
    depth, D = R['ffn1_norm_pre'].shape
    P = {}
    row = lambda a: a.astype(F32).reshape(depth, 1, a.shape[-1])
    for name in ('ffn1_norm_pre', 'ffn1_norm_post', 'mix_norm_pre', 'mix_norm_post', 'xa_norm_pre', 'xa_mem_norm',
                 'xa_norm_post', 'ffn2_norm_pre', 'ffn2_norm_post', 'mlstm_norm', 'rg_conv_b', 'rg_ba', 'rg_bx',
                 'rg_lambda', 'hg_norm'):
        P[name] = row(R[name])
    P['rg_conv_w'] = R['rg_conv_w'].astype(F32)
    P['hg_lb_logits'] = R['hg_lb_logits'].astype(F32)

    for name in ('ffn1_w_in', 'ffn1_w_out', 'ffn2_w_in', 'ffn2_w_out'):
        P[name] = R[name]

    mw = R['w_br_a'].shape[1]
    rw = R['w_br_b'].shape[1]
    hw = R['w_br_c'].shape[1]
    nh = R['mlstm_bi'].shape[1]
    splits = (mw, mw, mw, mw, nh, nh, rw, rw, hw, hw, hw, hw, D, D, D)
    pts = [0] + [int(v) for v in np.cumsum(splits)]
    runs = ((12, 15), (0, 4), (6, 12))
    segs, d0 = [], 0
    for a, b in runs:
        segs.append((pts[a], pts[b] - pts[a], d0))
        d0 += pts[b] - pts[a]
    P['w_in_main'], P['w_in_gate'] = _mix_w_in_cast(jnp.swapaxes(R['w_in'], 1, 2), tuple(segs), pts[4], 2 * nh)
    P['gate_bias'] = jnp.pad(jnp.concatenate([R['mlstm_bi'], R['mlstm_bf']], axis=-1).astype(F32),
                             ((0, 0), (0, GATE_PAD - 2 * nh))).reshape(depth, 1, GATE_PAD)

    eye = jnp.eye(RG_BLOCKS, dtype=F32)
    dense = lambda w: jnp.einsum('lnde,nm->lndme', w, eye).reshape(depth, rw, rw).astype(BF16)
    P['rg_wa'] = dense(R['rg_wa'])
    P['rg_wx'] = dense(R['rg_wx'])

    for name in ('w_br_a', 'w_br_b', 'w_br_c', 'w_out', 'xa_wq', 'xa_wk', 'xa_wv', 'xa_wo'):
        P[name] = R[name].astype(BF16)
    return P


def _ffn_sublayer(x, P, name, l, bf16_weights):
    npre, npost = P[name + '_norm_pre'], P[name + '_norm_post']
    M = x.shape[0]
    tm = min(FFN_ROWS, M)
    if (name, l) in bf16_weights:
        return _ffn(x, npre, *bf16_weights[(name, l)], npost, l, tm)
    y, wg, wu, wo = _ffn_first(x, npre, P[name + '_w_in'], P[name + '_w_out'], npost, l, tm)
    bf16_weights[(name, l)] = (wg, wu, wo)
    return y if M == tm else _ffn(x, npre, wg, wu, wo, npost, l, tm, first=y)


def _run_trunk(x3, states, state_has_layers, mem, mem_k, mem_v, P, bf16_weights):
    B, T, D = x3.shape
    M = B * T
    depth = P['ffn1_norm_pre'].shape[0]
    x = x3.reshape(M, D)
    tm_proj = min(1024, M)
    tm_merge = min(512, M)
    tq = min(512, T)
    C0, n0, m0, hr0, buf0, S0 = states
    m0 = m0.reshape(m0.shape[0], m0.shape[1], 1, m0.shape[2])
    hr0 = hr0.reshape(hr0.shape[0], hr0.shape[1], 1, hr0.shape[2])
    mw = P['w_br_a'].shape[1]
    rw = P['w_br_b'].shape[1]
    hw = P['w_br_c'].shape[1]
    col_m = 3 * D // mw
    col_r = (3 * D + 4 * mw) // rw
    col_h = (3 * D + 4 * mw + 2 * rw) // hw

    if mem is not None:
        mem_k, mem_v, att_k, att_v = _mem_kv(mem, P, XA_HEADS)
    else:
        att_k, att_v = mem_k, mem_v

    outs = [[] for _ in range(6)]
    for l in range(depth):
        ls = l if state_has_layers else 0
        x = _ffn_sublayer(x, P, 'ffn1', l, bf16_weights)
        z, gates = _mix_in(x, P['mix_norm_pre'], P['w_in_main'], P['w_in_gate'], l, tm_proj, 1024)
        ya, C1, n1, m1 = _mlstm(z, gates, P['gate_bias'], P['mlstm_norm'], C0, n0, m0, l, ls, B, T, col_m)
        yb, buf1, hr1 = _rglru(z, P, buf0, hr0, l, ls, B, T, col_r)
        yc, S1 = _hgrn(z, P['hg_lb_logits'], P['hg_norm'], S0, l, ls, B, T, col_h)
        x = _merge(x, ya, yb, yc, z, P, l, tm_merge)
        x = _xattn(x, att_k, att_v, P, l, l, B, T, tq)
        x = _ffn_sublayer(x, P, 'ffn2', l, bf16_weights)
        for lst, s in zip(outs, (C1, n1, m1.reshape(B, -1), hr1.reshape(B, -1), buf1, S1)):
            lst.append(s)
    stacked = tuple(jnp.stack(lst) for lst in outs)
    return x.reshape(B, T, D), stacked, mem_k, mem_v


def kernel(x_prompt, x_sample, mem_prompt, cache_mem_k, cache_mem_v, state_mlstm_C, state_mlstm_n, state_mlstm_m, state_rglru_h, state_rglru_conv, state_hgrn_S, ffn1_norm_pre, ffn1_w_in, ffn1_w_out, ffn1_norm_post, mix_norm_pre, w_in, mlstm_bi, mlstm_bf, mlstm_norm, rg_conv_w, rg_conv_b, rg_wa, rg_ba, rg_wx, rg_bx, rg_lambda, hg_lb_logits, hg_norm, w_br_a, w_br_b, w_br_c, w_out, mix_norm_post, xa_norm_pre, xa_mem_norm, xa_wq, xa_wk, xa_wv, xa_wo, xa_norm_post, ffn2_norm_pre, ffn2_w_in, ffn2_w_out, ffn2_norm_post):
    R = dict(ffn1_norm_pre=ffn1_norm_pre, ffn1_w_in=ffn1_w_in, ffn1_w_out=ffn1_w_out, ffn1_norm_post=ffn1_norm_post,
             mix_norm_pre=mix_norm_pre, w_in=w_in, mlstm_bi=mlstm_bi, mlstm_bf=mlstm_bf, mlstm_norm=mlstm_norm,
             rg_conv_w=rg_conv_w, rg_conv_b=rg_conv_b, rg_wa=rg_wa, rg_ba=rg_ba, rg_wx=rg_wx, rg_bx=rg_bx,
             rg_lambda=rg_lambda, hg_lb_logits=hg_lb_logits, hg_norm=hg_norm,
             w_br_a=w_br_a, w_br_b=w_br_b, w_br_c=w_br_c, w_out=w_out, mix_norm_post=mix_norm_post,
             xa_norm_pre=xa_norm_pre, xa_mem_norm=xa_mem_norm, xa_wq=xa_wq, xa_wk=xa_wk, xa_wv=xa_wv,
             xa_wo=xa_wo, xa_norm_post=xa_norm_post,
             ffn2_norm_pre=ffn2_norm_pre, ffn2_w_in=ffn2_w_in, ffn2_w_out=ffn2_w_out, ffn2_norm_post=ffn2_norm_post)
    P = _prep_params(R)
    bf16_weights = {}
    B = x_prompt.shape[0]
    zeros_like_state = lambda s: jnp.zeros((1, B) + s.shape[2:], F32)
    init = tuple(zeros_like_state(s) for s in (state_mlstm_C, state_mlstm_n, state_mlstm_m, state_rglru_h,
                                               state_rglru_conv, state_hgrn_S))
    y_prompt, p_states, p_mem_k, p_mem_v = _run_trunk(x_prompt, init, False, mem_prompt, None, None, P, bf16_weights)
    s_init = (state_mlstm_C, state_mlstm_n, state_mlstm_m, state_rglru_h, state_rglru_conv, state_hgrn_S)
    y_sample, s_states, _, _ = _run_trunk(x_sample, s_init, True, None, cache_mem_k, cache_mem_v, P, bf16_weights)
    return (y_prompt, y_sample) + p_states + (p_mem_k, p_mem_v) + s_states
```

```python
import functools

import jax
import jax.numpy as jnp
import numpy as np
from jax import lax
from jax.experimental import pallas as pl
from jax.experimental.pallas import tpu as pltpu

F32 = jnp.float32
BF16 = jnp.bfloat16
EPS = 1e-6

V7X_VMEM_LIMIT_BYTES = 56 * 1024 * 1024
LANES = 128

XA_HEADS = 4
MLSTM_HEADS = 4
HG_HEADS = 4
RG_BLOCKS = 8
RG_C = 8.0
CONV_W = 4
D_FF_TILE = 512
FFN_ROWS = 512
MLSTM_CHUNK = 256
RG_CHUNK = 256
HG_CHUNK = 128
GATE_PAD = LANES


def _params(*sem):
    return pltpu.CompilerParams(dimension_semantics=sem, vmem_limit_bytes=V7X_VMEM_LIMIT_BYTES)


def _rms(x, w):
    return x * lax.rsqrt(jnp.mean(x * x, axis=-1, keepdims=True) + EPS) * w


def _sigmoid(x):
    return 0.5 * jnp.tanh(0.5 * x) + 0.5


def _log_sigmoid(x):
    return jnp.minimum(x, 0.0) - jnp.log1p(jnp.exp(-jnp.abs(x)))


def _softplus(x):
    return jnp.maximum(x, 0.0) + jnp.log1p(jnp.exp(-jnp.abs(x)))


def _dot(a, b):
    return jnp.dot(a, b, preferred_element_type=F32)


def _dot_nt(a, b):
    return lax.dot_general(a, b, (((1,), (1,)), ((), ())), preferred_element_type=F32)


def _dot_tn(a, b):
    return lax.dot_general(a, b, (((0,), (0,)), ((), ())), preferred_element_type=F32)


def _shift_rows(x, d, fill, ridx):
    return jnp.where(ridx >= d, pltpu.roll(x, d, 0), fill)


def _ffn_step(j, nj, x_ref, npre_ref, weights, npost_ref, o_ref, hn_ref, acc_ref):
    @pl.when(j == 0)
    def _():
        hn_ref[...] = _rms(x_ref[...], npre_ref[...]).astype(BF16)
        acc_ref[...] = jnp.zeros_like(acc_ref)

    wg, wu, wo = weights()
    hn = hn_ref[...]
    g = _dot(hn, wg)
    u = _dot(hn, wu)
    a = (g * _sigmoid(g) * u).astype(BF16)
    acc_ref[...] += _dot(a, wo)

    @pl.when(j == nj - 1)
    def _():
        o_ref[...] = x_ref[...] + 0.5 * _rms(acc_ref[...], npost_ref[...])


def _ffn_kernel(x_ref, npre_ref, wg_ref, wu_ref, wo_ref, npost_ref, *rest, has_first):
    o_ref, hn_ref, acc_ref = rest[-3:]
    i, j, nj = pl.program_id(0), pl.program_id(1), pl.num_programs(1)

    def compute():
        _ffn_step(j, nj, x_ref, npre_ref, lambda: (wg_ref[...], wu_ref[...], wo_ref[...]), npost_ref,
                  o_ref, hn_ref, acc_ref)

    if not has_first:
        compute()
        return
    pl.when(i > 0)(compute)

    @pl.when((i == 0) & (j == nj - 1))
    def _():
        o_ref[...] = rest[0][...]


def _ffn(x, npre, wg, wu, wo, npost, l, tm, first=None):
    M, D = x.shape
    Fp = wo.shape[0]
    tn = D_FF_TILE
    nj = Fp // tn
    has_first = first is not None
    tile = (lambda i, j: jnp.where(i == 0, 0, j)) if has_first else (lambda i, j: j)
    args = [x, npre, wg, wu, wo, npost]
    in_specs = [
        pl.BlockSpec((tm, D), lambda i, j: (i, 0)),
        pl.BlockSpec((None, 1, D), lambda i, j: (l, 0, 0)),
        pl.BlockSpec((D, tn), lambda i, j: (0, tile(i, j))),
        pl.BlockSpec((D, tn), lambda i, j: (0, tile(i, j))),
        pl.BlockSpec((tn, D), lambda i, j: (tile(i, j), 0)),
        pl.BlockSpec((None, 1, D), lambda i, j: (l, 0, 0)),
    ]
    if has_first:
        args.append(first)
        in_specs.append(pl.BlockSpec((tm, D), lambda i, j: (0, 0)))
    return pl.pallas_call(
        functools.partial(_ffn_kernel, has_first=has_first),
        grid=(M // tm, nj),
        in_specs=in_specs,
        out_specs=pl.BlockSpec((tm, D), lambda i, j: (i, 0)),
        out_shape=jax.ShapeDtypeStruct((M, D), F32),
        scratch_shapes=[pltpu.VMEM((tm, D), BF16), pltpu.VMEM((tm, D), F32)],
        compiler_params=_params("parallel", "arbitrary"),
        name="ffn",
    )(*args)


def _ffn_first_kernel(x_ref, npre_ref, g32_ref, u0_ref, u32_ref, wo32_ref, npost_ref,
                      o_ref, wg_ref, wu_ref, wo_ref, hn_ref, acc_ref, uprev_ref, *, F, shift):
    j = pl.program_id(0)
    tn = wg_ref.shape[-1]

    def weights():
        col_ok = lax.broadcasted_iota(jnp.int32, (1, tn), 1) + j * tn < F
        row_ok = lax.broadcasted_iota(jnp.int32, (tn, 1), 0) + j * tn < F
        if shift:
            @pl.when(j == 0)
            def _():
                uprev_ref[...] = u0_ref[...]

            u32 = jnp.concatenate([uprev_ref[:, shift:], u32_ref[:, :shift]], axis=1)
            uprev_ref[...] = u32_ref[...]
        else:
            u32 = u32_ref[...]
        wg = jnp.where(col_ok, g32_ref[...], 0.0).astype(BF16)
        wu = jnp.where(col_ok, u32, 0.0).astype(BF16)
        wo = jnp.where(row_ok, wo32_ref[...], 0.0).astype(BF16)
        wg_ref[...] = wg
        wu_ref[...] = wu
        wo_ref[...] = wo
        return wg, wu, wo

    _ffn_step(j, pl.num_programs(0), x_ref, npre_ref, weights, npost_ref, o_ref, hn_ref, acc_ref)


def _ffn_first(x, npre, w_in, w_out, npost, l, tm, tn=256):
    D = x.shape[1]
    F = w_out.shape[1]
    Fp = -(-F // D_FF_TILE) * D_FF_TILE
    nj = Fp // tn
    q, shift = divmod(F, tn)
    last_in = -(-2 * F // tn) - 1
    return pl.pallas_call(
        functools.partial(_ffn_first_kernel, F=F, shift=shift),
        grid=(nj,),
        in_specs=[
            pl.BlockSpec((tm, D), lambda j: (0, 0)),
            pl.BlockSpec((None, 1, D), lambda j: (l, 0, 0)),
            pl.BlockSpec((None, D, tn), lambda j: (l, 0, j)),
            pl.BlockSpec((None, D, tn), lambda j: (l, 0, q)),
            pl.BlockSpec((None, D, tn), lambda j: (l, 0, jnp.minimum(q + j + (1 if shift else 0), last_in))),
            pl.BlockSpec((None, tn, D), lambda j: (l, j, 0)),
            pl.BlockSpec((None, 1, D), lambda j: (l, 0, 0)),
        ],
        out_specs=[
            pl.BlockSpec((tm, D), lambda j: (0, 0)),
            pl.BlockSpec((D, tn), lambda j: (0, j)),
            pl.BlockSpec((D, tn), lambda j: (0, j)),
            pl.BlockSpec((tn, D), lambda j: (j, 0)),
        ],
        out_shape=[
            jax.ShapeDtypeStruct((tm, D), F32),
            jax.ShapeDtypeStruct((D, Fp), BF16),
            jax.ShapeDtypeStruct((D, Fp), BF16),
            jax.ShapeDtypeStruct((Fp, D), BF16),
        ],
        scratch_shapes=[pltpu.VMEM((tm, D), BF16), pltpu.VMEM((tm, D), F32), pltpu.VMEM((D, tn), F32)],
        compiler_params=_params("arbitrary"),
        name="ffn_first",
    )(x, npre, w_in, w_in, w_in, w_out, npost)


def _nmm_kernel(x_ref, nw_ref, w_ref, o_ref, hn_ref):
    @pl.when(pl.program_id(1) == 0)
    def _():
        hn_ref[...] = _rms(x_ref[...], nw_ref[...]).astype(BF16)

    o_ref[...] = _dot(hn_ref[...], w_ref[...]).astype(o_ref.dtype)


def _norm_matmul(x, nw, w, l, out_dtype, tm, tn, name):
    M, D = x.shape
    N = w.shape[-1]
    return pl.pallas_call(
        _nmm_kernel,
        grid=(M // tm, N // tn),
        in_specs=[
            pl.BlockSpec((tm, D), lambda i, j: (i, 0)),
            pl.BlockSpec((None, 1, D), lambda i, j: (l, 0, 0)),
            pl.BlockSpec((None, D, tn), lambda i, j: (l, 0, j)),
        ],
        out_specs=pl.BlockSpec((tm, tn), lambda i, j: (i, j)),
        out_shape=jax.ShapeDtypeStruct((M, N), out_dtype),
        scratch_shapes=[pltpu.VMEM((tm, D), BF16)],
        compiler_params=_params("parallel", "arbitrary"),
        name=name,
    )(x, nw, w)


def _mix_in_kernel(x_ref, nw_ref, w_ref, wg_ref, *rest, has_first):
    z_ref, g_ref, hn_ref = rest[-3:]
    i, j = pl.program_id(0), pl.program_id(1)

    def compute():
        @pl.when(j == 0)
        def _():
            hn = _rms(x_ref[...], nw_ref[...]).astype(BF16)
            hn_ref[...] = hn
            g_ref[...] = _dot_nt(hn, wg_ref[...])

        z_ref[...] = _dot_nt(hn_ref[...], w_ref[...]).astype(z_ref.dtype)

    if not has_first:
        compute()
        return
    pl.when(i > 0)(compute)

    @pl.when(i == 0)
    def _():
        z_ref[...] = rest[0][...]

    @pl.when((i == 0) & (j == 0))
    def _():
        g_ref[...] = rest[1][...]


def _mix_in(x, nw, w, wg, l, tm, tn, first=None):
    M, D = x.shape
    N, G = w.shape[0], wg.shape[0]
    has_first = first is not None
    tile = (lambda i, j: jnp.where(i == 0, 0, j)) if has_first else (lambda i, j: j)
    args = [x, nw, w, wg]
    in_specs = [
        pl.BlockSpec((tm, D), lambda i, j: (i, 0)),
        pl.BlockSpec((None, 1, D), lambda i, j: (l, 0, 0)),
        pl.BlockSpec((tn, D), lambda i, j: (tile(i, j), 0)),
        pl.BlockSpec((G, D), lambda i, j: (0, 0)),
    ]
    if has_first:
        args += list(first)
        in_specs += [pl.BlockSpec((tm, tn), lambda i, j: (0, jnp.where(i == 0, j, 0))),
                     pl.BlockSpec((tm, G), lambda i, j: (0, 0))]
    return pl.pallas_call(
        functools.partial(_mix_in_kernel, has_first=has_first),
        grid=(M // tm, N // tn),
        in_specs=in_specs,
        out_specs=[pl.BlockSpec((tm, tn), lambda i, j: (i, j)), pl.BlockSpec((tm, G), lambda i, j: (i, 0))],
        out_shape=[jax.ShapeDtypeStruct((M, N), BF16), jax.ShapeDtypeStruct((M, G), F32)],
        scratch_shapes=[pltpu.VMEM((tm, D), BF16)],
        compiler_params=_params("parallel", "arbitrary"),
        name="mix_in",
    )(*args)


def _mix_in_first_kernel(x_ref, nw_ref, a_ref, b_ref, g32_ref, z_ref, gout_ref, w_ref, wg_ref, hn_ref,
                         *, r, aligned, ngate):
    j = pl.program_id(0)

    @pl.when(j == 0)
    def _():
        hn = _rms(x_ref[...], nw_ref[...]).astype(BF16)
        hn_ref[...] = hn
        rowi = lax.broadcasted_iota(jnp.int32, g32_ref.shape, 0)
        wg = jnp.where(rowi < ngate, g32_ref[...], 0.0).astype(BF16)
        wg_ref[...] = wg
        gout_ref[...] = _dot_nt(hn, wg)

    a = a_ref[...]
    w32 = a
    if r:
        is_aligned = functools.reduce(jnp.logical_or, [(j >= lo) & (j < hi) for lo, hi in aligned], j < 0)
        w32 = jnp.where(is_aligned, a, jnp.concatenate([a[r:], b_ref[...]], axis=0))
    w = w32.astype(BF16)
    w_ref[...] = w
    z_ref[...] = _dot_nt(hn_ref[...], w).astype(z_ref.dtype)


def _mix_in_first(x, nw, wt, segs, gate0, ngate, l, tm, tn=512):
    D = x.shape[1]
    N = sum(s[1] for s in segs)
    r = max(s0 % tn for s0, _, _ in segs)
    assert all(s0 % tn in (0, r) and cnt % tn == 0 and d0 % tn == 0 for s0, cnt, d0 in segs)
    assert r % 8 == 0 and gate0 % GATE_PAD == 0
    tiles = [(d0 // tn, (d0 + cnt) // tn, s0 // tn) for s0, cnt, d0 in segs]
    aligned = tuple((t0, t1) for (t0, t1, _), (s0, _, _) in zip(tiles, segs) if s0 % tn == 0)

    def src_block(j):
        blk = 0
        for t0, t1, a0 in tiles:
            blk = jnp.where((j >= t0) & (j < t1), a0 + j - t0, blk)
        return blk

    rb = max(r, 8)
    return pl.pallas_call(
        functools.partial(_mix_in_first_kernel, r=r, aligned=aligned, ngate=ngate),
        grid=(N // tn,),
        in_specs=[
            pl.BlockSpec((tm, D), lambda j: (0, 0)),
            pl.BlockSpec((None, 1, D), lambda j: (l, 0, 0)),
            pl.BlockSpec((None, tn, D), lambda j: (l, src_block(j), 0)),
            pl.BlockSpec((None, rb, D), lambda j: (l, (src_block(j) + 1) * (tn // rb), 0)),
            pl.BlockSpec((None, GATE_PAD, D), lambda j: (l, gate0 // GATE_PAD, 0)),
        ],
        out_specs=[
            pl.BlockSpec((tm, tn), lambda j: (0, j)),
            pl.BlockSpec((tm, GATE_PAD), lambda j: (0, 0)),
            pl.BlockSpec((tn, D), lambda j: (j, 0)),
            pl.BlockSpec((GATE_PAD, D), lambda j: (0, 0)),
        ],
        out_shape=[
            jax.ShapeDtypeStruct((tm, N), BF16),
            jax.ShapeDtypeStruct((tm, GATE_PAD), F32),
            jax.ShapeDtypeStruct((N, D), BF16),
            jax.ShapeDtypeStruct((GATE_PAD, D), BF16),
        ],
        scratch_shapes=[pltpu.VMEM((tm, D), BF16)],
        compiler_params=_params("arbitrary"),
        name="mix_in_first",
    )(x, nw, wt, wt, wt)


def _mlstm_kernel(q_ref, k_ref, v_ref, og_ref, g_ref, gb_ref, nw_ref, C0_ref, n0_ref, m0_ref,
                  hm_ref, C_ref, n_ref, m_ref, *, L, dh):
    H = MLSTM_HEADS

    @pl.when(pl.program_id(1) == 0)
    def _():
        C_ref[...] = C0_ref[...]
        n_ref[...] = n0_ref[...]
        m_ref[...] = m0_ref[...]

    row = lax.broadcasted_iota(jnp.int32, (L, L), 0)
    col = lax.broadcasted_iota(jnp.int32, (L, L), 1)
    causal = row >= col
    eye = row == col

    def as_row(x_col):
        return jnp.sum(jnp.where(eye, x_col, 0.0), axis=0, keepdims=True)

    for bb in range(q_ref.shape[0]):
        ga = g_ref[bb] + gb_ref[...]
        m_prev = m_ref[bb]
        gaT = ga.T if L % LANES == 0 else None
        for h in range(H):
            hs = slice(h * dh, (h + 1) * dh)
            ig_c = ga[:, h:h + 1]
            lf_c = _log_sigmoid(ga[:, H + h:H + h + 1])
            if gaT is not None:
                ig_r = gaT[h:h + 1, :]
                lf_r = _log_sigmoid(gaT[H + h:H + h + 1, :])
            else:
                ig_r = as_row(ig_c)
                lf_r = as_row(lf_c)
            b_c = jnp.sum(jnp.where(causal, lf_r, 0.0), axis=1, keepdims=True)
            b_r = jnp.sum(jnp.where(row <= col, lf_c, 0.0), axis=0, keepdims=True)
            m0 = m_prev[:, h:h + 1]
            src_r = ig_r - b_r
            peak_c = jnp.maximum(m0, jnp.max(jnp.where(causal, src_r, -jnp.inf), axis=1, keepdims=True))
            m_c = b_c + peak_c
            w = jnp.exp(jnp.where(causal, src_r - peak_c, -jnp.inf))
            w_inter = jnp.exp(m0 - peak_c)

            q = q_ref[bb, :, hs]
            k = k_ref[bb, :, hs] * (dh ** -0.5)
            v = v_ref[bb, :, hs]
            C0 = C_ref[bb, h]
            n0 = n_ref[bb, h:h + 1, :]
            wqk = w * _dot_nt(q, k)
            num = _dot(wqk.astype(BF16), v) + w_inter * _dot(q, C0.astype(BF16))
            den = (jnp.sum(wqk, axis=1, keepdims=True)
                   + w_inter * jnp.sum(q.astype(F32) * n0, axis=1, keepdims=True))
            hh = num / jnp.maximum(jnp.abs(den), jnp.exp(-m_c))
            hn = hh * lax.rsqrt(jnp.mean(hh * hh, axis=-1, keepdims=True) + EPS) * nw_ref[:, hs]
            hm_ref[bb, :, hs] = (_sigmoid(og_ref[bb, :, hs].astype(F32)) * hn).astype(BF16)

            m_end = m_c[L - 1:L, :]
            w_end = jnp.exp(b_c[L - 1:L, :] - b_c + ig_c - m_end)
            s_end = w_inter[L - 1:L, :]
            kw = k.astype(F32) * w_end
            C_ref[bb, h] = s_end * C0 + _dot_tn(kw.astype(BF16), v)
            n_ref[bb, h:h + 1, :] = s_end * n0 + jnp.sum(kw, axis=0, keepdims=True)
            m_ref[bb, :, h:h + 1] = m_end


def _mlstm(z, gates, gate_bias, norm_w, C0, n0, m0, l, ls, B, T, col0):
    H = MLSTM_HEADS
    dh = C0.shape[-1]
    W = H * dh
    L = min(MLSTM_CHUNK, T)
    nC = T // L
    M = B * T
    nb = 1
    z3 = z.reshape(B, T, z.shape[-1])
    g3 = gates.reshape(B, T, GATE_PAD)
    zspec = lambda cb: pl.BlockSpec((nb, L, W), lambda b, c: (b, c, cb))
    hm, C1, n1, m1 = pl.pallas_call(
        functools.partial(_mlstm_kernel, L=L, dh=dh),
        grid=(B // nb, nC),
        in_specs=[
            zspec(col0), zspec(col0 + 1), zspec(col0 + 2), zspec(col0 + 3),
            pl.BlockSpec((nb, L, GATE_PAD), lambda b, c: (b, c, 0)),
            pl.BlockSpec((None, 1, GATE_PAD), lambda b, c: (l, 0, 0)),
            pl.BlockSpec((None, 1, W), lambda b, c: (l, 0, 0)),
            pl.BlockSpec((None, nb, H, dh, dh), lambda b, c: (ls, b, 0, 0, 0)),
            pl.BlockSpec((None, nb, H, dh), lambda b, c: (ls, b, 0, 0)),
            pl.BlockSpec((None, nb, 1, H), lambda b, c: (ls, b, 0, 0)),
        ],
        out_specs=[
            pl.BlockSpec((nb, L, W), lambda b, c: (b, c, 0)),
            pl.BlockSpec((nb, H, dh, dh), lambda b, c: (b, 0, 0, 0)),
            pl.BlockSpec((nb, H, dh), lambda b, c: (b, 0, 0)),
            pl.BlockSpec((nb, 1, H), lambda b, c: (b, 0, 0)),
        ],
        out_shape=[
            jax.ShapeDtypeStruct((B, T, W), BF16),
            jax.ShapeDtypeStruct((B, H, dh, dh), F32),
            jax.ShapeDtypeStruct((B, H, dh), F32),
            jax.ShapeDtypeStruct((B, 1, H), F32),
        ],
        compiler_params=_params("parallel", "arbitrary"),
        name="mlstm",
    )(z3, z3, z3, z3, g3, gate_bias, norm_w, C0, n0, m0)
    return hm.reshape(M, W), C1, n1, m1


def _rglru_kernel(rx_ref, rg_ref, cw_ref, cbias_ref, wa_ref, ba_ref, wx_ref, bx_ref, lam_ref, buf0_ref, h0_ref,
                  y_ref, buf_ref, h_ref, cbuf_ref, *, L):
    c = pl.program_id(1)
    W = rx_ref.shape[-1]
    TAIL = CONV_W - 1

    @pl.when(c == 0)
    def _():
        cbuf_ref[0:8, :] = jnp.zeros((8, W), F32)
        cbuf_ref[8 - TAIL:8, :] = buf0_ref[0]
        h_ref[...] = h0_ref[...]

    cbuf_ref[8:8 + L, :] = rx_ref[...].astype(F32)
    xc = cbias_ref[...]
    for j in range(CONV_W):
        xc = xc + cbuf_ref[8 - TAIL + j:8 - TAIL + j + L, :] * cw_ref[j:j + 1, :]

    xcb = xc.astype(BF16)
    r = _sigmoid(_dot(xcb, wa_ref[...]) + ba_ref[...])
    i = _sigmoid(_dot(xcb, wx_ref[...]) + bx_ref[...])
    log_a = (-RG_C * _softplus(-lam_ref[...])) * r
    a = jnp.exp(log_a)
    th = jnp.tanh(log_a)
    u = jnp.sqrt(-2.0 * th / (1.0 - th)) * (i * xc)

    ridx = lax.broadcasted_iota(jnp.int32, (L, W), 0)
    d = 1
    while d < L:
        a_sh = _shift_rows(a, d, 1.0, ridx)
        u_sh = _shift_rows(u, d, 0.0, ridx)
        u = a * u_sh + u
        a = a * a_sh
        d *= 2
    h = a * h_ref[0] + u
    h_ref[0] = h[L - 1:L, :]
    y_ref[...] = (h * jax.nn.gelu(rg_ref[...].astype(F32))).astype(BF16)

    cbuf_ref[0:8, :] = cbuf_ref[L:L + 8, :]

    @pl.when(c == pl.num_programs(1) - 1)
    def _():
        buf_ref[0] = cbuf_ref[8 + L - TAIL:8 + L, :]


def _rglru(z, P, buf0, h0, l, ls, B, T, col0):
    W = h0.shape[-1]
    L = min(RG_CHUNK, T)
    nC = T // L
    M = B * T
    vec = lambda: pl.BlockSpec((None, 1, W), lambda b, c: (l, 0, 0))
    mat = lambda: pl.BlockSpec((None, W, W), lambda b, c: (l, 0, 0))
    return pl.pallas_call(
        functools.partial(_rglru_kernel, L=L),
        grid=(B, nC),
        in_specs=[
            pl.BlockSpec((L, W), lambda b, c: (b * nC + c, col0)),
            pl.BlockSpec((L, W), lambda b, c: (b * nC + c, col0 + 1)),
            pl.BlockSpec((None, CONV_W, W), lambda b, c: (l, 0, 0)),
            vec(), mat(), vec(), mat(), vec(), vec(),
            pl.BlockSpec((None, 1, CONV_W - 1, W), lambda b, c: (ls, b, 0, 0)),
            pl.BlockSpec((None, 1, 1, W), lambda b, c: (ls, b, 0, 0)),
        ],
        out_specs=[
            pl.BlockSpec((L, W), lambda b, c: (b * nC + c, 0)),
            pl.BlockSpec((1, CONV_W - 1, W), lambda b, c: (b, 0, 0)),
            pl.BlockSpec((1, 1, W), lambda b, c: (b, 0, 0)),
        ],
        out_shape=[
            jax.ShapeDtypeStruct((M, W), BF16),
            jax.ShapeDtypeStruct((B, CONV_W - 1, W), F32),
            jax.ShapeDtypeStruct((B, 1, W), F32),
        ],
        scratch_shapes=[pltpu.VMEM((L + 8, W), F32)],
        compiler_params=_params("parallel", "arbitrary"),
        name="rglru",
    )(z, z, P['rg_conv_w'], P['rg_conv_b'], P['rg_wa'], P['rg_ba'], P['rg_wx'], P['rg_bx'], P['rg_lambda'],
      buf0, h0)


def _block_ref_rows(cb, m, ridx):
    L, W = cb.shape
    n = 2 * m
    if m == 1:
        return jnp.where((ridx & 1) == 0, cb, pltpu.roll(cb, 1, 0))
    if m == 2:
        off = ridx & 3
        return jnp.where(off == 0, pltpu.roll(cb, L - 1, 0),
                         jnp.where(off == 1, cb, jnp.where(off == 2, pltpu.roll(cb, 1, 0), pltpu.roll(cb, 2, 0))))
    return jnp.concatenate(
        [jnp.broadcast_to(cb[j * n + m - 1:j * n + m, :], (n, W)) for j in range(L // n)], axis=0)


def _hgrn_kernel(q_ref, f_ref, i_ref, g_ref, lbl_ref, nw_ref, S0_ref, y_ref, S_ref, *, L, layer):
    H = HG_HEADS
    W = q_ref.shape[-1]
    dk = W // H

    @pl.when(pl.program_id(1) == 0)
    def _():
        S_ref[...] = S0_ref[...]

    lg = lbl_ref[...]
    e = jnp.exp(lg - jnp.max(lg, axis=0, keepdims=True))
    p = e / jnp.sum(e, axis=0, keepdims=True)
    lb = jnp.zeros((1, W), F32)
    for r in range(1, layer + 1):
        lb = lb + p[r:r + 1, :]

    f = lb + (1.0 - lb) * jax.nn.sigmoid(f_ref[...].astype(F32))
    cb = jnp.log(f)
    ridx = lax.broadcasted_iota(jnp.int32, (L, W), 0)
    d = 1
    while d < L:
        cb = cb + _shift_rows(cb, d, 0.0, ridx)
        d *= 2
    kk = 1.0 - f
    q = q_ref[...].astype(F32)
    v = i_ref[...].astype(F32)
    vb = v.astype(BF16)

    row = lax.broadcasted_iota(jnp.int32, (L, L), 0)
    col = lax.broadcasted_iota(jnp.int32, (L, L), 1)
    split = jnp.where(row > col, row ^ col, 0)
    a = [jnp.zeros((L, L), F32) for _ in range(H)]
    m = 1
    while m < L:
        dref = cb - _block_ref_rows(cb, m, ridx)
        dec = jnp.exp(-jnp.abs(dref))
        qt = (q * dec).astype(BF16)
        ks = (kk * dec).astype(BF16)
        level = (split // m) == 1
        for h in range(H):
            hs = slice(h * dk, (h + 1) * dk)
            a[h] = jnp.where(level, _dot_nt(qt[:, hs], ks[:, hs]), a[h])
        m *= 2

    cb_end = cb[L - 1:L, :]
    qe = (q * jnp.exp(cb)).astype(BF16)
    ke = (kk * jnp.exp(cb_end - cb)).astype(BF16)
    qk = q * kk
    eye = (lax.broadcasted_iota(jnp.int32, (dk, dk), 0) == lax.broadcasted_iota(jnp.int32, (dk, dk), 1))
    outs = []
    for h in range(H):
        hs = slice(h * dk, (h + 1) * dk)
        S = S_ref[0, h]
        o = (_dot(a[h].astype(BF16), vb[:, hs]) + jnp.sum(qk[:, hs], axis=1, keepdims=True) * v[:, hs]
             + _dot(qe[:, hs], S.astype(BF16)))
        dec_col = jnp.sum(jnp.where(eye, jnp.exp(cb_end[:, hs]), 0.0), axis=1, keepdims=True)
        S_ref[0, h] = dec_col * S + _dot_tn(ke[:, hs], vb[:, hs])
        outs.append(o * lax.rsqrt(jnp.mean(o * o, axis=-1, keepdims=True) + EPS))
    gate = g_ref[...].astype(F32)
    y_ref[...] = (jnp.concatenate(outs, axis=1) * nw_ref[...] * (gate * _sigmoid(gate))).astype(BF16)


def _hgrn(z, lb_logits, norm_w, S0, l, ls, B, T, col0):
    H, dk, dv = S0.shape[-3:]
    W = H * dk
    L = min(HG_CHUNK, T)
    nC = T // L
    M = B * T
    depth = lb_logits.shape[0]
    zspec = lambda cb: pl.BlockSpec((L, W), lambda b, c: (b * nC + c, cb))
    return pl.pallas_call(
        functools.partial(_hgrn_kernel, L=L, layer=l),
        grid=(B, nC),
        in_specs=[
            zspec(col0), zspec(col0 + 1), zspec(col0 + 2), zspec(col0 + 3),
            pl.BlockSpec((depth, W), lambda b, c: (0, 0)),
            pl.BlockSpec((None, 1, W), lambda b, c: (l, 0, 0)),
            pl.BlockSpec((None, 1, H, dk, dv), lambda b, c: (ls, b, 0, 0, 0)),
        ],
        out_specs=[
            pl.BlockSpec((L, W), lambda b, c: (b * nC + c, 0)),
            pl.BlockSpec((1, H, dk, dv), lambda b, c: (b, 0, 0, 0)),
        ],
        out_shape=[
            jax.ShapeDtypeStruct((M, W), BF16),
            jax.ShapeDtypeStruct((B, H, dk, dv), F32),
        ],
        compiler_params=_params("parallel", "arbitrary"),
        name="hgrn",
    )(z, z, z, z, lb_logits, norm_w, S0)


def _merge_kernel(a_ref, b_ref, c_ref, ga_ref, gb_ref, gc_ref, wa_ref, wb_ref, wc_ref, wo_ref, npost_ref, x_ref,
                  o_ref):
    sig = lambda r: _sigmoid(r[...].astype(F32))
    merged = (sig(ga_ref) * _dot(a_ref[...], wa_ref[...])
              + sig(gb_ref) * _dot(b_ref[...], wb_ref[...])
              + sig(gc_ref) * _dot(c_ref[...], wc_ref[...]))
    out = _dot(merged.astype(BF16), wo_ref[...])
    o_ref[...] = x_ref[...] + _rms(out, npost_ref[...])


def _merge(x, ya, yb, yc, z, P, l, tm):
    M, D = x.shape
    rows = lambda w: pl.BlockSpec((tm, w), lambda i: (i, 0))
    wspec = lambda k: pl.BlockSpec((None, k, D), lambda i: (l, 0, 0), pipeline_mode=pl.Buffered(1))
    return pl.pallas_call(
        _merge_kernel,
        grid=(M // tm,),
        in_specs=[
            rows(ya.shape[1]), rows(yb.shape[1]), rows(yc.shape[1]),
            pl.BlockSpec((tm, D), lambda i: (i, 0)),
            pl.BlockSpec((tm, D), lambda i: (i, 1)),
            pl.BlockSpec((tm, D), lambda i: (i, 2)),
            wspec(ya.shape[1]), wspec(yb.shape[1]), wspec(yc.shape[1]), wspec(D),
            pl.BlockSpec((None, 1, D), lambda i: (l, 0, 0)),
            rows(D),
        ],
        out_specs=rows(D),
        out_shape=jax.ShapeDtypeStruct((M, D), F32),
        compiler_params=_params("parallel"),
        name="merge",
    )(ya, yb, yc, z, z, z, P['w_br_a'], P['w_br_b'], P['w_br_c'], P['w_out'], P['mix_norm_post'], x)


def _xattn_kernel(x_ref, npre_ref, wq_ref, k_ref, v_ref, wo_ref, npost_ref, o_ref, *kv_scratch):
    D = x_ref.shape[-1]
    dh = D // XA_HEADS
    if kv_scratch:
        kb_ref, vb_ref = kv_scratch

        @pl.when(pl.program_id(1) == 0)
        def _():
            for h in range(XA_HEADS):
                kb_ref[:, h * dh:(h + 1) * dh] = k_ref[:, h, :].astype(BF16)
                vb_ref[:, h * dh:(h + 1) * dh] = v_ref[:, h, :].astype(BF16)
    else:
        kb_ref, vb_ref = k_ref, v_ref

    x = x_ref[...]
    q = _dot(_rms(x, npre_ref[...]).astype(BF16), wq_ref[...]).astype(BF16)
    outs = []
    for h in range(XA_HEADS):
        hs = slice(h * dh, (h + 1) * dh)
        s = _dot_nt(q[:, hs], kb_ref[:, hs]) * (dh ** -0.5)
        e = jnp.exp(s - jnp.max(s, axis=-1, keepdims=True))
        p = e / jnp.sum(e, axis=-1, keepdims=True)
        outs.append(_dot(p.astype(BF16), vb_ref[:, hs]).astype(BF16))
    out = _dot(jnp.concatenate(outs, axis=1), wo_ref[...])
    o_ref[...] = x + _rms(out, npost_ref[...])


def _xattn(x, mem_k, mem_v, P, l, lk, B, T, tq):
    M, D = x.shape
    n_mem = mem_k.shape[2]
    nT = T // tq
    wspec = lambda: pl.BlockSpec((None, D, D), lambda b, t: (l, 0, 0), pipeline_mode=pl.Buffered(1))
    nspec = lambda: pl.BlockSpec((None, 1, D), lambda b, t: (l, 0, 0))
    kv_block = (None, None) + mem_k.shape[2:]
    kv_index = lambda b, t: (lk, b) + (0,) * (mem_k.ndim - 2)
    kvspec = lambda: pl.BlockSpec(kv_block, kv_index)
    scratch = [] if mem_k.dtype == BF16 else [pltpu.VMEM((n_mem, D), BF16), pltpu.VMEM((n_mem, D), BF16)]
    return pl.pallas_call(
        _xattn_kernel,
        grid=(B, nT),
        in_specs=[
            pl.BlockSpec((tq, D), lambda b, t: (b * nT + t, 0)),
            nspec(), wspec(), kvspec(), kvspec(), wspec(), nspec(),
        ],
        out_specs=pl.BlockSpec((tq, D), lambda b, t: (b * nT + t, 0)),
        out_shape=jax.ShapeDtypeStruct((M, D), F32),
        scratch_shapes=scratch,
        compiler_params=_params("parallel", "arbitrary"),
        name="xattn",
    )(x, P['xa_norm_pre'], P['xa_wq'], mem_k, mem_v, P['xa_wo'], P['xa_norm_post'])


def _mem_kv_kernel(m_ref, nw_ref, wk_ref, wv_ref, k_ref, v_ref, kb_ref, vb_ref):
    H, dh = k_ref.shape[-2:]
    hn = _rms(m_ref[...], nw_ref[...]).astype(BF16)
    k = _dot(hn, wk_ref[...])
    v = _dot(hn, wv_ref[...])
    kb_ref[...] = k.astype(BF16)
    vb_ref[...] = v.astype(BF16)
    for h in range(H):
        k_ref[:, h, :] = k[:, h * dh:(h + 1) * dh]
        v_ref[:, h, :] = v[:, h * dh:(h + 1) * dh]


def _mem_kv(mem, P, H):
    B, n_mem, D = mem.shape
    depth = P['xa_wk'].shape[0]
    out = jax.ShapeDtypeStruct((depth, B, n_mem, H, D // H), F32)
    outb = jax.ShapeDtypeStruct((depth, B, n_mem, D), BF16)
    ospec = lambda: pl.BlockSpec((None, None, n_mem, H, D // H), lambda l, b: (l, b, 0, 0, 0))
    obspec = lambda: pl.BlockSpec((None, None, n_mem, D), lambda l, b: (l, b, 0, 0))
    wspec = lambda: pl.BlockSpec((None, D, D), lambda l, b: (l, 0, 0))
    return pl.pallas_call(
        _mem_kv_kernel,
        grid=(depth, B),
        in_specs=[
            pl.BlockSpec((None, n_mem, D), lambda l, b: (b, 0, 0)),
            pl.BlockSpec((None, 1, D), lambda l, b: (l, 0, 0)),
            wspec(), wspec(),
        ],
        out_specs=[ospec(), ospec(), obspec(), obspec()],
        out_shape=[out, out, outb, outb],
        compiler_params=_params("parallel", "arbitrary"),
        name="mem_kv",
    )(mem, P['xa_mem_norm'], P['xa_wk'], P['xa_wv'])


def _prep_params(R):
    depth, D = R['ffn1_norm_pre'].shape
    P = {}
    row = lambda a: a.astype(F32).reshape(depth, 1, a.shape[-1])
    for name in ('ffn1_norm_pre', 'ffn1_norm_post', 'mix_norm_pre', 'mix_norm_post', 'xa_norm_pre', 'xa_mem_norm',
                 'xa_norm_post', 'ffn2_norm_pre', 'ffn2_norm_post', 'mlstm_norm', 'rg_conv_b', 'rg_ba', 'rg_bx',
                 'rg_lambda', 'hg_norm'):
        P[name] = row(R[name])
    P['rg_conv_w'] = R['rg_conv_w'].astype(F32)
    P['hg_lb_logits'] = R['hg_lb_logits'].astype(F32)

    for name in ('ffn1_w_in', 'ffn1_w_out', 'ffn2_w_in', 'ffn2_w_out'):
        P[name] = R[name]

    mw = R['w_br_a'].shape[1]
    rw = R['w_br_b'].shape[1]
    hw = R['w_br_c'].shape[1]
    nh = R['mlstm_bi'].shape[1]
    splits = (mw, mw, mw, mw, nh, nh, rw, rw, hw, hw, hw, hw, D, D, D)
    pts = [0] + [int(v) for v in np.cumsum(splits)]
    runs = ((12, 15), (0, 4), (6, 12))
    segs, d0 = [], 0
    for a, b in runs:
        segs.append((pts[a], pts[b] - pts[a], d0))
        d0 += pts[b] - pts[a]
    P['w_in_t'] = jnp.swapaxes(R['w_in'], 1, 2)
    P['w_in_regroup'] = (tuple(segs), pts[4], 2 * nh)
    P['gate_bias'] = jnp.pad(jnp.concatenate([R['mlstm_bi'], R['mlstm_bf']], axis=-1).astype(F32),
                             ((0, 0), (0, GATE_PAD - 2 * nh))).reshape(depth, 1, GATE_PAD)

    eye = jnp.eye(RG_BLOCKS, dtype=F32)
    dense = lambda w: jnp.einsum('lnde,nm->lndme', w, eye).reshape(depth, rw, rw).astype(BF16)
    P['rg_wa'] = dense(R['rg_wa'])
    P['rg_wx'] = dense(R['rg_wx'])

    for name in ('w_br_a', 'w_br_b', 'w_br_c', 'w_out', 'xa_wq', 'xa_wk', 'xa_wv', 'xa_wo'):
        P[name] = R[name].astype(BF16)
    return P


def _ffn_sublayer(x, P, name, l, bf16_weights):
    npre, npost = P[name + '_norm_pre'], P[name + '_norm_post']
    M = x.shape[0]
    tm = min(FFN_ROWS, M)
    if (name, l) in bf16_weights:
        return _ffn(x, npre, *bf16_weights[(name, l)], npost, l, tm)
    y, wg, wu, wo = _ffn_first(x, npre, P[name + '_w_in'], P[name + '_w_out'], npost, l, tm)
    bf16_weights[(name, l)] = (wg, wu, wo)
    return y if M == tm else _ffn(x, npre, wg, wu, wo, npost, l, tm, first=y)


def _run_trunk(x3, states, state_has_layers, mem, mem_k, mem_v, P, bf16_weights):
    B, T, D = x3.shape
    M = B * T
    depth = P['ffn1_norm_pre'].shape[0]
    x = x3.reshape(M, D)
    tm_proj = min(1024, M)
    tm_merge = min(512, M)
    tq = min(512, T)
    C0, n0, m0, hr0, buf0, S0 = states
    m0 = m0.reshape(m0.shape[0], m0.shape[1], 1, m0.shape[2])
    hr0 = hr0.reshape(hr0.shape[0], hr0.shape[1], 1, hr0.shape[2])
    mw = P['w_br_a'].shape[1]
    rw = P['w_br_b'].shape[1]
    hw = P['w_br_c'].shape[1]
    col_m = 3 * D // mw
    col_r = (3 * D + 4 * mw) // rw
    col_h = (3 * D + 4 * mw + 2 * rw) // hw

    if mem is not None:
        mem_k, mem_v, att_k, att_v = _mem_kv(mem, P, XA_HEADS)
    else:
        att_k, att_v = mem_k, mem_v

    outs = [[] for _ in range(6)]
    for l in range(depth):
        ls = l if state_has_layers else 0
        x = _ffn_sublayer(x, P, 'ffn1', l, bf16_weights)
        if ('mix', l) in bf16_weights:
            z, gates = _mix_in(x, P['mix_norm_pre'], *bf16_weights[('mix', l)], l, tm_proj, 1024)
        else:
            z, gates, w_mix, w_gate = _mix_in_first(x, P['mix_norm_pre'], P['w_in_t'], *P['w_in_regroup'], l, tm_proj)
            bf16_weights[('mix', l)] = (w_mix, w_gate)
            if M > tm_proj:
                z, gates = _mix_in(x, P['mix_norm_pre'], w_mix, w_gate, l, tm_proj, 1024, first=(z, gates))
        ya, C1, n1, m1 = _mlstm(z, gates, P['gate_bias'], P['mlstm_norm'], C0, n0, m0, l, ls, B, T, col_m)
        yb, buf1, hr1 = _rglru(z, P, buf0, hr0, l, ls, B, T, col_r)
        yc, S1 = _hgrn(z, P['hg_lb_logits'], P['hg_norm'], S0, l, ls, B, T, col_h)
        x = _merge(x, ya, yb, yc, z, P, l, tm_merge)
        x = _xattn(x, att_k, att_v, P, l, l, B, T, tq)
        x = _ffn_sublayer(x, P, 'ffn2', l, bf16_weights)
        for lst, s in zip(outs, (C1, n1, m1.reshape(B, -1), hr1.reshape(B, -1), buf1, S1)):
            lst.append(s)
    stacked = tuple(jnp.stack(lst) for lst in outs)
    return x.reshape(B, T, D), stacked, mem_k, mem_v


def kernel(x_prompt, x_sample, mem_prompt, cache_mem_k, cache_mem_v, state_mlstm_C, state_mlstm_n, state_mlstm_m, state_rglru_h, state_rglru_conv, state_hgrn_S, ffn1_norm_pre, ffn1_w_in, ffn1_w_out, ffn1_norm_post, mix_norm_pre, w_in, mlstm_bi, mlstm_bf, mlstm_norm, rg_conv_w, rg_conv_b, rg_wa, rg_ba, rg_wx, rg_bx, rg_lambda, hg_lb_logits, hg_norm, w_br_a, w_br_b, w_br_c, w_out, mix_norm_post, xa_norm_pre, xa_mem_norm, xa_wq, xa_wk, xa_wv, xa_wo, xa_norm_post, ffn2_norm_pre, ffn2_w_in, ffn2_w_out, ffn2_norm_post):
    R = dict(ffn1_norm_pre=ffn1_norm_pre, ffn1_w_in=ffn1_w_in, ffn1_w_out=ffn1_w_out, ffn1_norm_post=ffn1_norm_post,
             mix_norm_pre=mix_norm_pre, w_in=w_in, mlstm_bi=mlstm_bi, mlstm_bf=mlstm_bf, mlstm_norm=mlstm_norm,
             rg_conv_w=rg_conv_w, rg_conv_b=rg_conv_b, rg_wa=rg_wa, rg_ba=rg_ba, rg_wx=rg_wx, rg_bx=rg_bx,
             rg_lambda=rg_lambda, hg_lb_logits=hg_lb_logits, hg_norm=hg_norm,
             w_br_a=w_br_a, w_br_b=w_br_b, w_br_c=w_br_c, w_out=w_out, mix_norm_post=mix_norm_post,
             xa_norm_pre=xa_norm_pre, xa_mem_norm=xa_mem_norm, xa_wq=xa_wq, xa_wk=xa_wk, xa_wv=xa_wv,
             xa_wo=xa_wo, xa_norm_post=xa_norm_post,
             ffn2_norm_pre=ffn2_norm_pre, ffn2_w_in=ffn2_w_in, ffn2_w_out=ffn2_w_out, ffn2_norm_post=ffn2_norm_post)
    P = _prep_params(R)
    bf16_weights = {}
    B = x_prompt.shape[0]
    zeros_like_state = lambda s: jnp.zeros((1, B) + s.shape[2:], F32)
    init = tuple(zeros_like_state(s) for s in (state_mlstm_C, state_mlstm_n, state_mlstm_m, state_rglru_h,
                                               state_rglru_conv, state_hgrn_S))
    y_prompt, p_states, p_mem_k, p_mem_v = _run_trunk(x_prompt, init, False, mem_prompt, None, None, P, bf16_weights)
    s_init = (state_mlstm_C, state_mlstm_n, state_mlstm_m, state_rglru_h, state_rglru_conv, state_hgrn_S)
    y_sample, s_states, _, _ = _run_trunk(x_sample, s_init, True, None, cache_mem_k, cache_mem_v, P, bf16_weights)
    return (y_prompt, y_sample) + p_states + (p_mem_k, p_mem_v) + s_states
```

```python
import functools

import jax
import jax.numpy as jnp
import numpy as np
from jax import lax
from jax.experimental import pallas as pl
from jax.experimental.pallas import tpu as pltpu

F32 = jnp.float32
BF16 = jnp.bfloat16
EPS = 1e-6

V7X_VMEM_LIMIT_BYTES = 56 * 1024 * 1024
LANES = 128

XA_HEADS = 4
MLSTM_HEADS = 4
HG_HEADS = 4
RG_BLOCKS = 8
RG_C = 8.0
CONV_W = 4
D_FF_TILE = 512
FFN_ROWS = 512
XA_SHORT_ROWS = 512
MLSTM_CHUNK = 256
RG_CHUNK = 256
HG_CHUNK = 128
GATE_PAD = LANES


def _params(*sem):
    return pltpu.CompilerParams(dimension_semantics=sem, vmem_limit_bytes=V7X_VMEM_LIMIT_BYTES)


def _rms(x, w):
    return x * lax.rsqrt(jnp.mean(x * x, axis=-1, keepdims=True) + EPS) * w


def _sigmoid(x):
    return 0.5 * jnp.tanh(0.5 * x) + 0.5


def _log_sigmoid(x):
    return jnp.minimum(x, 0.0) - jnp.log1p(jnp.exp(-jnp.abs(x)))


def _softplus(x):
    return jnp.maximum(x, 0.0) + jnp.log1p(jnp.exp(-jnp.abs(x)))


def _dot(a, b):
    return jnp.dot(a, b, preferred_element_type=F32)


def _dot_nt(a, b):
    return lax.dot_general(a, b, (((1,), (1,)), ((), ())), preferred_element_type=F32)


def _dot_tn(a, b):
    return lax.dot_general(a, b, (((0,), (0,)), ((), ())), preferred_element_type=F32)


def _shift_rows(x, d, fill, ridx):
    return jnp.where(ridx >= d, pltpu.roll(x, d, 0), fill)


def _ffn_step(j, nj, x_ref, npre_ref, weights, npost_ref, o_ref, hn_ref, acc_ref):
    @pl.when(j == 0)
    def _():
        hn_ref[...] = _rms(x_ref[...], npre_ref[...]).astype(BF16)
        acc_ref[...] = jnp.zeros_like(acc_ref)

    wg, wu, wo = weights()
    hn = hn_ref[...]
    g = _dot(hn, wg)
    u = _dot(hn, wu)
    a = (g * _sigmoid(g) * u).astype(BF16)
    acc_ref[...] += _dot(a, wo)

    @pl.when(j == nj - 1)
    def _():
        o_ref[...] = x_ref[...] + 0.5 * _rms(acc_ref[...], npost_ref[...])


def _ffn_kernel(x_ref, npre_ref, wg_ref, wu_ref, wo_ref, npost_ref, *rest, has_first):
    o_ref, hn_ref, acc_ref = rest[-3:]
    i, j, nj = pl.program_id(0), pl.program_id(1), pl.num_programs(1)

    def compute():
        _ffn_step(j, nj, x_ref, npre_ref, lambda: (wg_ref[...], wu_ref[...], wo_ref[...]), npost_ref,
                  o_ref, hn_ref, acc_ref)

    if not has_first:
        compute()
        return
    pl.when(i > 0)(compute)

    @pl.when((i == 0) & (j == nj - 1))
    def _():
        o_ref[...] = rest[0][...]


def _ffn(x, npre, wg, wu, wo, npost, l, tm, first=None):
    M, D = x.shape
    Fp = wo.shape[0]
    tn = D_FF_TILE
    nj = Fp // tn
    has_first = first is not None
    tile = (lambda i, j: jnp.where(i == 0, 0, j)) if has_first else (lambda i, j: j)
    args = [x, npre, wg, wu, wo, npost]
    in_specs = [
        pl.BlockSpec((tm, D), lambda i, j: (i, 0)),
        pl.BlockSpec((None, 1, D), lambda i, j: (l, 0, 0)),
        pl.BlockSpec((None, D, tn), lambda i, j: (tile(i, j), 0, 0)),
        pl.BlockSpec((None, D, tn), lambda i, j: (tile(i, j), 0, 0)),
        pl.BlockSpec((tn, D), lambda i, j: (tile(i, j), 0)),
        pl.BlockSpec((None, 1, D), lambda i, j: (l, 0, 0)),
    ]
    if has_first:
        args.append(first)
        in_specs.append(pl.BlockSpec((tm, D), lambda i, j: (0, 0)))
    return pl.pallas_call(
        functools.partial(_ffn_kernel, has_first=has_first),
        grid=(M // tm, nj),
        in_specs=in_specs,
        out_specs=pl.BlockSpec((tm, D), lambda i, j: (i, 0)),
        out_shape=jax.ShapeDtypeStruct((M, D), F32),
        scratch_shapes=[pltpu.VMEM((tm, D), BF16), pltpu.VMEM((tm, D), F32)],
        compiler_params=_params("parallel", "arbitrary"),
        name="ffn",
    )(*args)


def _ffn_first_kernel(x_ref, npre_ref, g32_ref, u0_ref, u32_ref, wo32_ref, npost_ref,
                      o_ref, wg_ref, wu_ref, wo_ref, hn_ref, acc_ref, uprev_ref, *, F, shift):
    j = pl.program_id(0)
    tn = wg_ref.shape[-1]

    def weights():
        col_ok = lax.broadcasted_iota(jnp.int32, (1, tn), 1) + j * tn < F
        row_ok = lax.broadcasted_iota(jnp.int32, (tn, 1), 0) + j * tn < F
        if shift:
            @pl.when(j == 0)
            def _():
                uprev_ref[...] = u0_ref[...]

            u32 = jnp.concatenate([uprev_ref[:, shift:], u32_ref[:, :shift]], axis=1)
            uprev_ref[...] = u32_ref[...]
        else:
            u32 = u32_ref[...]
        wg = jnp.where(col_ok, g32_ref[...], 0.0).astype(BF16)
        wu = jnp.where(col_ok, u32, 0.0).astype(BF16)
        wo = jnp.where(row_ok, wo32_ref[...], 0.0).astype(BF16)
        wg_ref[...] = wg
        wu_ref[...] = wu
        wo_ref[...] = wo
        return wg, wu, wo

    _ffn_step(j, pl.num_programs(0), x_ref, npre_ref, weights, npost_ref, o_ref, hn_ref, acc_ref)


def _ffn_first(x, npre, w_in, w_out, npost, l, tm, tn=256):
    D = x.shape[1]
    F = w_out.shape[1]
    Fp = -(-F // D_FF_TILE) * D_FF_TILE
    nj = Fp // tn
    per = D_FF_TILE // tn
    q, shift = divmod(F, tn)
    last_in = -(-2 * F // tn) - 1
    return pl.pallas_call(
        functools.partial(_ffn_first_kernel, F=F, shift=shift),
        grid=(nj,),
        in_specs=[
            pl.BlockSpec((tm, D), lambda j: (0, 0)),
            pl.BlockSpec((None, 1, D), lambda j: (l, 0, 0)),
            pl.BlockSpec((None, D, tn), lambda j: (l, 0, j)),
            pl.BlockSpec((None, D, tn), lambda j: (l, 0, q)),
            pl.BlockSpec((None, D, tn), lambda j: (l, 0, jnp.minimum(q + j + (1 if shift else 0), last_in))),
            pl.BlockSpec((None, tn, D), lambda j: (l, j, 0)),
            pl.BlockSpec((None, 1, D), lambda j: (l, 0, 0)),
        ],
        out_specs=[
            pl.BlockSpec((tm, D), lambda j: (0, 0)),
            pl.BlockSpec((None, D, tn), lambda j: (j // per, 0, j % per)),
            pl.BlockSpec((None, D, tn), lambda j: (j // per, 0, j % per)),
            pl.BlockSpec((tn, D), lambda j: (j, 0)),
        ],
        out_shape=[
            jax.ShapeDtypeStruct((tm, D), F32),
            jax.ShapeDtypeStruct((Fp // D_FF_TILE, D, D_FF_TILE), BF16),
            jax.ShapeDtypeStruct((Fp // D_FF_TILE, D, D_FF_TILE), BF16),
            jax.ShapeDtypeStruct((Fp, D), BF16),
        ],
        scratch_shapes=[pltpu.VMEM((tm, D), BF16), pltpu.VMEM((tm, D), F32), pltpu.VMEM((D, tn), F32)],
        compiler_params=_params("arbitrary"),
        name="ffn_first",
    )(x, npre, w_in, w_in, w_in, w_out, npost)


def _nmm_kernel(x_ref, nw_ref, w_ref, o_ref, hn_ref):
    @pl.when(pl.program_id(1) == 0)
    def _():
        hn_ref[...] = _rms(x_ref[...], nw_ref[...]).astype(BF16)

    o_ref[...] = _dot(hn_ref[...], w_ref[...]).astype(o_ref.dtype)


def _norm_matmul(x, nw, w, l, out_dtype, tm, tn, name):
    M, D = x.shape
    N = w.shape[-1]
    return pl.pallas_call(
        _nmm_kernel,
        grid=(M // tm, N // tn),
        in_specs=[
            pl.BlockSpec((tm, D), lambda i, j: (i, 0)),
            pl.BlockSpec((None, 1, D), lambda i, j: (l, 0, 0)),
            pl.BlockSpec((None, D, tn), lambda i, j: (l, 0, j)),
        ],
        out_specs=pl.BlockSpec((tm, tn), lambda i, j: (i, j)),
        out_shape=jax.ShapeDtypeStruct((M, N), out_dtype),
        scratch_shapes=[pltpu.VMEM((tm, D), BF16)],
        compiler_params=_params("parallel", "arbitrary"),
        name=name,
    )(x, nw, w)


def _mix_in_kernel(x_ref, nw_ref, w_ref, wg_ref, *rest, has_first):
    z_ref, g_ref, hn_ref = rest[-3:]
    i, j = pl.program_id(0), pl.program_id(1)

    def compute():
        @pl.when(j == 0)
        def _():
            hn = _rms(x_ref[...], nw_ref[...]).astype(BF16)
            hn_ref[...] = hn
            g_ref[...] = _dot_nt(hn, wg_ref[...])

        z_ref[...] = _dot_nt(hn_ref[...], w_ref[...]).astype(z_ref.dtype)

    if not has_first:
        compute()
        return
    pl.when(i > 0)(compute)

    @pl.when(i == 0)
    def _():
        z_ref[...] = rest[0][...]

    @pl.when((i == 0) & (j == 0))
    def _():
        g_ref[...] = rest[1][...]


def _mix_in(x, nw, w, wg, l, tm, tn, first=None):
    M, D = x.shape
    N, G = w.shape[0], wg.shape[0]
    has_first = first is not None
    tile = (lambda i, j: jnp.where(i == 0, 0, j)) if has_first else (lambda i, j: j)
    args = [x, nw, w, wg]
    in_specs = [
        pl.BlockSpec((tm, D), lambda i, j: (i, 0)),
        pl.BlockSpec((None, 1, D), lambda i, j: (l, 0, 0)),
        pl.BlockSpec((tn, D), lambda i, j: (tile(i, j), 0)),
        pl.BlockSpec((G, D), lambda i, j: (0, 0)),
    ]
    if has_first:
        args += list(first)
        in_specs += [pl.BlockSpec((tm, tn), lambda i, j: (0, jnp.where(i == 0, j, 0))),
                     pl.BlockSpec((tm, G), lambda i, j: (0, 0))]
    return pl.pallas_call(
        functools.partial(_mix_in_kernel, has_first=has_first),
        grid=(M // tm, N // tn),
        in_specs=in_specs,
        out_specs=[pl.BlockSpec((tm, tn), lambda i, j: (i, j)), pl.BlockSpec((tm, G), lambda i, j: (i, 0))],
        out_shape=[jax.ShapeDtypeStruct((M, N), BF16), jax.ShapeDtypeStruct((M, G), F32)],
        scratch_shapes=[pltpu.VMEM((tm, D), BF16)],
        compiler_params=_params("parallel", "arbitrary"),
        name="mix_in",
    )(*args)


def _mix_in_first_kernel(x_ref, nw_ref, a_ref, b_ref, g32_ref, z_ref, gout_ref, w_ref, wg_ref, hn_ref,
                         *, r, aligned, ngate):
    j = pl.program_id(0)

    @pl.when(j == 0)
    def _():
        hn = _rms(x_ref[...], nw_ref[...]).astype(BF16)
        hn_ref[...] = hn
        rowi = lax.broadcasted_iota(jnp.int32, g32_ref.shape, 0)
        wg = jnp.where(rowi < ngate, g32_ref[...], 0.0).astype(BF16)
        wg_ref[...] = wg
        gout_ref[...] = _dot_nt(hn, wg)

    a = a_ref[...]
    w32 = a
    if r:
        is_aligned = functools.reduce(jnp.logical_or, [(j >= lo) & (j < hi) for lo, hi in aligned], j < 0)
        w32 = jnp.where(is_aligned, a, jnp.concatenate([a[r:], b_ref[...]], axis=0))
    w = w32.astype(BF16)
    w_ref[...] = w
    z_ref[...] = _dot_nt(hn_ref[...], w).astype(z_ref.dtype)


def _mix_in_first(x, nw, wt, segs, gate0, ngate, l, tm, tn=512):
    D = x.shape[1]
    N = sum(s[1] for s in segs)
    r = max(s0 % tn for s0, _, _ in segs)
    assert all(s0 % tn in (0, r) and cnt % tn == 0 and d0 % tn == 0 for s0, cnt, d0 in segs)
    assert r % 8 == 0 and gate0 % GATE_PAD == 0
    tiles = [(d0 // tn, (d0 + cnt) // tn, s0 // tn) for s0, cnt, d0 in segs]
    aligned = tuple((t0, t1) for (t0, t1, _), (s0, _, _) in zip(tiles, segs) if s0 % tn == 0)

    def src_block(j):
        blk = 0
        for t0, t1, a0 in tiles:
            blk = jnp.where((j >= t0) & (j < t1), a0 + j - t0, blk)
        return blk

    rb = max(r, 8)
    return pl.pallas_call(
        functools.partial(_mix_in_first_kernel, r=r, aligned=aligned, ngate=ngate),
        grid=(N // tn,),
        in_specs=[
            pl.BlockSpec((tm, D), lambda j: (0, 0)),
            pl.BlockSpec((None, 1, D), lambda j: (l, 0, 0)),
            pl.BlockSpec((None, tn, D), lambda j: (l, src_block(j), 0)),
            pl.BlockSpec((None, rb, D), lambda j: (l, (src_block(j) + 1) * (tn // rb), 0)),
            pl.BlockSpec((None, GATE_PAD, D), lambda j: (l, gate0 // GATE_PAD, 0)),
        ],
        out_specs=[
            pl.BlockSpec((tm, tn), lambda j: (0, j)),
            pl.BlockSpec((tm, GATE_PAD), lambda j: (0, 0)),
            pl.BlockSpec((tn, D), lambda j: (j, 0)),
            pl.BlockSpec((GATE_PAD, D), lambda j: (0, 0)),
        ],
        out_shape=[
            jax.ShapeDtypeStruct((tm, N), BF16),
            jax.ShapeDtypeStruct((tm, GATE_PAD), F32),
            jax.ShapeDtypeStruct((N, D), BF16),
            jax.ShapeDtypeStruct((GATE_PAD, D), BF16),
        ],
        scratch_shapes=[pltpu.VMEM((tm, D), BF16)],
        compiler_params=_params("arbitrary"),
        name="mix_in_first",
    )(x, nw, wt, wt, wt)


def _mlstm_kernel(q_ref, k_ref, v_ref, og_ref, g_ref, gb_ref, nw_ref, C0_ref, n0_ref, m0_ref,
                  hm_ref, C_ref, n_ref, m_ref, *, L, dh):
    H = MLSTM_HEADS

    @pl.when(pl.program_id(1) == 0)
    def _():
        C_ref[...] = C0_ref[...]
        n_ref[...] = n0_ref[...]
        m_ref[...] = m0_ref[...]

    row = lax.broadcasted_iota(jnp.int32, (L, L), 0)
    col = lax.broadcasted_iota(jnp.int32, (L, L), 1)
    causal = row >= col
    eye = row == col

    def as_row(x_col):
        return jnp.sum(jnp.where(eye, x_col, 0.0), axis=0, keepdims=True)

    for bb in range(q_ref.shape[0]):
        ga = g_ref[bb] + gb_ref[...]
        m_prev = m_ref[bb]
        gaT = ga.T if L % LANES == 0 else None
        for h in range(H):
            hs = slice(h * dh, (h + 1) * dh)
            ig_c = ga[:, h:h + 1]
            lf_c = _log_sigmoid(ga[:, H + h:H + h + 1])
            if gaT is not None:
                ig_r = gaT[h:h + 1, :]
                lf_r = _log_sigmoid(gaT[H + h:H + h + 1, :])
            else:
                ig_r = as_row(ig_c)
                lf_r = as_row(lf_c)
            b_c = jnp.sum(jnp.where(causal, lf_r, 0.0), axis=1, keepdims=True)
            b_r = jnp.sum(jnp.where(row <= col, lf_c, 0.0), axis=0, keepdims=True)
            m0 = m_prev[:, h:h + 1]
            src_r = ig_r - b_r
            peak_c = jnp.maximum(m0, jnp.max(jnp.where(causal, src_r, -jnp.inf), axis=1, keepdims=True))
            m_c = b_c + peak_c
            w = jnp.exp(jnp.where(causal, src_r - peak_c, -jnp.inf))
            w_inter = jnp.exp(m0 - peak_c)

            q = q_ref[bb, :, hs]
            k = k_ref[bb, :, hs] * (dh ** -0.5)
            v = v_ref[bb, :, hs]
            C0 = C_ref[bb, h]
            n0 = n_ref[bb, h:h + 1, :]
            wqk = w * _dot_nt(q, k)
            num = _dot(wqk.astype(BF16), v) + w_inter * _dot(q, C0.astype(BF16))
            den = (jnp.sum(wqk, axis=1, keepdims=True)
                   + w_inter * jnp.sum(q.astype(F32) * n0, axis=1, keepdims=True))
            hh = num / jnp.maximum(jnp.abs(den), jnp.exp(-m_c))
            hn = hh * lax.rsqrt(jnp.mean(hh * hh, axis=-1, keepdims=True) + EPS) * nw_ref[:, hs]
            hm_ref[bb, :, hs] = (_sigmoid(og_ref[bb, :, hs].astype(F32)) * hn).astype(BF16)

            m_end = m_c[L - 1:L, :]
            w_end = jnp.exp(b_c[L - 1:L, :] - b_c + ig_c - m_end)
            s_end = w_inter[L - 1:L, :]
            kw = k.astype(F32) * w_end
            C_ref[bb, h] = s_end * C0 + _dot_tn(kw.astype(BF16), v)
            n_ref[bb, h:h + 1, :] = s_end * n0 + jnp.sum(kw, axis=0, keepdims=True)
            m_ref[bb, :, h:h + 1] = m_end


def _mlstm(z, gates, gate_bias, norm_w, C0, n0, m0, l, ls, B, T, col0):
    H = MLSTM_HEADS
    dh = C0.shape[-1]
    W = H * dh
    L = min(MLSTM_CHUNK, T)
    nC = T // L
    M = B * T
    nb = 1
    z3 = z.reshape(B, T, z.shape[-1])
    g3 = gates.reshape(B, T, GATE_PAD)
    zspec = lambda cb: pl.BlockSpec((nb, L, W), lambda b, c: (b, c, cb))
    hm, C1, n1, m1 = pl.pallas_call(
        functools.partial(_mlstm_kernel, L=L, dh=dh),
        grid=(B // nb, nC),
        in_specs=[
            zspec(col0), zspec(col0 + 1), zspec(col0 + 2), zspec(col0 + 3),
            pl.BlockSpec((nb, L, GATE_PAD), lambda b, c: (b, c, 0)),
            pl.BlockSpec((None, 1, GATE_PAD), lambda b, c: (l, 0, 0)),
            pl.BlockSpec((None, 1, W), lambda b, c: (l, 0, 0)),
            pl.BlockSpec((None, nb, H, dh, dh), lambda b, c: (ls, b, 0, 0, 0)),
            pl.BlockSpec((None, nb, H, dh), lambda b, c: (ls, b, 0, 0)),
            pl.BlockSpec((None, nb, 1, H), lambda b, c: (ls, b, 0, 0)),
        ],
        out_specs=[
            pl.BlockSpec((nb, L, W), lambda b, c: (b, c, 0)),
            pl.BlockSpec((nb, H, dh, dh), lambda b, c: (b, 0, 0, 0)),
            pl.BlockSpec((nb, H, dh), lambda b, c: (b, 0, 0)),
            pl.BlockSpec((nb, 1, H), lambda b, c: (b, 0, 0)),
        ],
        out_shape=[
            jax.ShapeDtypeStruct((B, T, W), BF16),
            jax.ShapeDtypeStruct((B, H, dh, dh), F32),
            jax.ShapeDtypeStruct((B, H, dh), F32),
            jax.ShapeDtypeStruct((B, 1, H), F32),
        ],
        compiler_params=_params("parallel", "arbitrary"),
        name="mlstm",
    )(z3, z3, z3, z3, g3, gate_bias, norm_w, C0, n0, m0)
    return hm.reshape(M, W), C1, n1, m1


def _rglru_kernel(rx_ref, rg_ref, cw_ref, cbias_ref, wa_ref, ba_ref, wx_ref, bx_ref, lam_ref, buf0_ref, h0_ref,
                  y_ref, buf_ref, h_ref, cbuf_ref, *, L):
    c = pl.program_id(1)
    W = rx_ref.shape[-1]
    TAIL = CONV_W - 1

    @pl.when(c == 0)
    def _():
        cbuf_ref[0:8, :] = jnp.zeros((8, W), F32)
        cbuf_ref[8 - TAIL:8, :] = buf0_ref[0]
        h_ref[...] = h0_ref[...]

    cbuf_ref[8:8 + L, :] = rx_ref[...].astype(F32)
    xc = cbias_ref[...]
    for j in range(CONV_W):
        xc = xc + cbuf_ref[8 - TAIL + j:8 - TAIL + j + L, :] * cw_ref[j:j + 1, :]

    xcb = xc.astype(BF16)
    r = _sigmoid(_dot(xcb, wa_ref[...]) + ba_ref[...])
    i = _sigmoid(_dot(xcb, wx_ref[...]) + bx_ref[...])
    log_a = (-RG_C * _softplus(-lam_ref[...])) * r
    a = jnp.exp(log_a)
    th = jnp.tanh(log_a)
    u = jnp.sqrt(-2.0 * th / (1.0 - th)) * (i * xc)

    ridx = lax.broadcasted_iota(jnp.int32, (L, W), 0)
    d = 1
    while d < L:
        a_sh = _shift_rows(a, d, 1.0, ridx)
        u_sh = _shift_rows(u, d, 0.0, ridx)
        u = a * u_sh + u
        a = a * a_sh
        d *= 2
    h = a * h_ref[0] + u
    h_ref[0] = h[L - 1:L, :]
    y_ref[...] = (h * jax.nn.gelu(rg_ref[...].astype(F32))).astype(BF16)

    cbuf_ref[0:8, :] = cbuf_ref[L:L + 8, :]

    @pl.when(c == pl.num_programs(1) - 1)
    def _():
        buf_ref[0] = cbuf_ref[8 + L - TAIL:8 + L, :]


def _rglru(z, P, buf0, h0, l, ls, B, T, col0):
    W = h0.shape[-1]
    L = min(RG_CHUNK, T)
    nC = T // L
    M = B * T
    vec = lambda: pl.BlockSpec((None, 1, W), lambda b, c: (l, 0, 0))
    mat = lambda: pl.BlockSpec((None, W, W), lambda b, c: (l, 0, 0))
    return pl.pallas_call(
        functools.partial(_rglru_kernel, L=L),
        grid=(B, nC),
        in_specs=[
            pl.BlockSpec((L, W), lambda b, c: (b * nC + c, col0)),
            pl.BlockSpec((L, W), lambda b, c: (b * nC + c, col0 + 1)),
            pl.BlockSpec((None, CONV_W, W), lambda b, c: (l, 0, 0)),
            vec(), mat(), vec(), mat(), vec(), vec(),
            pl.BlockSpec((None, 1, CONV_W - 1, W), lambda b, c: (ls, b, 0, 0)),
            pl.BlockSpec((None, 1, 1, W), lambda b, c: (ls, b, 0, 0)),
        ],
        out_specs=[
            pl.BlockSpec((L, W), lambda b, c: (b * nC + c, 0)),
            pl.BlockSpec((1, CONV_W - 1, W), lambda b, c: (b, 0, 0)),
            pl.BlockSpec((1, 1, W), lambda b, c: (b, 0, 0)),
        ],
        out_shape=[
            jax.ShapeDtypeStruct((M, W), BF16),
            jax.ShapeDtypeStruct((B, CONV_W - 1, W), F32),
            jax.ShapeDtypeStruct((B, 1, W), F32),
        ],
        scratch_shapes=[pltpu.VMEM((L + 8, W), F32)],
        compiler_params=_params("parallel", "arbitrary"),
        name="rglru",
    )(z, z, P['rg_conv_w'], P['rg_conv_b'], P['rg_wa'], P['rg_ba'], P['rg_wx'], P['rg_bx'], P['rg_lambda'],
      buf0, h0)


def _block_ref_rows(cb, m, ridx):
    L, W = cb.shape
    n = 2 * m
    if m == 1:
        return jnp.where((ridx & 1) == 0, cb, pltpu.roll(cb, 1, 0))
    if m == 2:
        off = ridx & 3
        return jnp.where(off == 0, pltpu.roll(cb, L - 1, 0),
                         jnp.where(off == 1, cb, jnp.where(off == 2, pltpu.roll(cb, 1, 0), pltpu.roll(cb, 2, 0))))
    return jnp.concatenate(
        [jnp.broadcast_to(cb[j * n + m - 1:j * n + m, :], (n, W)) for j in range(L // n)], axis=0)


def _hgrn_kernel(q_ref, f_ref, i_ref, g_ref, lbl_ref, nw_ref, S0_ref, y_ref, S_ref, *, L, layer):
    H = HG_HEADS
    W = q_ref.shape[-1]
    dk = W // H

    @pl.when(pl.program_id(1) == 0)
    def _():
        S_ref[...] = S0_ref[...]

    lg = lbl_ref[...]
    e = jnp.exp(lg - jnp.max(lg, axis=0, keepdims=True))
    p = e / jnp.sum(e, axis=0, keepdims=True)
    lb = jnp.zeros((1, W), F32)
    for r in range(1, layer + 1):
        lb = lb + p[r:r + 1, :]

    f = lb + (1.0 - lb) * jax.nn.sigmoid(f_ref[...].astype(F32))
    cb = jnp.log(f)
    ridx = lax.broadcasted_iota(jnp.int32, (L, W), 0)
    d = 1
    while d < L:
        cb = cb + _shift_rows(cb, d, 0.0, ridx)
        d *= 2
    kk = 1.0 - f
    q = q_ref[...].astype(F32)
    v = i_ref[...].astype(F32)
    vb = v.astype(BF16)

    row = lax.broadcasted_iota(jnp.int32, (L, L), 0)
    col = lax.broadcasted_iota(jnp.int32, (L, L), 1)
    split = jnp.where(row > col, row ^ col, 0)
    a = [jnp.zeros((L, L), F32) for _ in range(H)]
    m = 1
    while m < L:
        dref = cb - _block_ref_rows(cb, m, ridx)
        dec = jnp.exp(-jnp.abs(dref))
        qt = (q * dec).astype(BF16)
        ks = (kk * dec).astype(BF16)
        level = (split // m) == 1
        for h in range(H):
            hs = slice(h * dk, (h + 1) * dk)
            a[h] = jnp.where(level, _dot_nt(qt[:, hs], ks[:, hs]), a[h])
        m *= 2

    cb_end = cb[L - 1:L, :]
    qe = (q * jnp.exp(cb)).astype(BF16)
    ke = (kk * jnp.exp(cb_end - cb)).astype(BF16)
    qk = q * kk
    eye = (lax.broadcasted_iota(jnp.int32, (dk, dk), 0) == lax.broadcasted_iota(jnp.int32, (dk, dk), 1))
    outs = []
    for h in range(H):
        hs = slice(h * dk, (h + 1) * dk)
        S = S_ref[0, h]
        o = (_dot(a[h].astype(BF16), vb[:, hs]) + jnp.sum(qk[:, hs], axis=1, keepdims=True) * v[:, hs]
             + _dot(qe[:, hs], S.astype(BF16)))
        dec_col = jnp.sum(jnp.where(eye, jnp.exp(cb_end[:, hs]), 0.0), axis=1, keepdims=True)
        S_ref[0, h] = dec_col * S + _dot_tn(ke[:, hs], vb[:, hs])
        outs.append(o * lax.rsqrt(jnp.mean(o * o, axis=-1, keepdims=True) + EPS))
    gate = g_ref[...].astype(F32)
    y_ref[...] = (jnp.concatenate(outs, axis=1) * nw_ref[...] * (gate * _sigmoid(gate))).astype(BF16)


def _hgrn(z, lb_logits, norm_w, S0, l, ls, B, T, col0):
    H, dk, dv = S0.shape[-3:]
    W = H * dk
    L = min(HG_CHUNK, T)
    nC = T // L
    M = B * T
    depth = lb_logits.shape[0]
    zspec = lambda cb: pl.BlockSpec((L, W), lambda b, c: (b * nC + c, cb))
    return pl.pallas_call(
        functools.partial(_hgrn_kernel, L=L, layer=l),
        grid=(B, nC),
        in_specs=[
            zspec(col0), zspec(col0 + 1), zspec(col0 + 2), zspec(col0 + 3),
            pl.BlockSpec((depth, W), lambda b, c: (0, 0)),
            pl.BlockSpec((None, 1, W), lambda b, c: (l, 0, 0)),
            pl.BlockSpec((None, 1, H, dk, dv), lambda b, c: (ls, b, 0, 0, 0)),
        ],
        out_specs=[
            pl.BlockSpec((L, W), lambda b, c: (b * nC + c, 0)),
            pl.BlockSpec((1, H, dk, dv), lambda b, c: (b, 0, 0, 0)),
        ],
        out_shape=[
            jax.ShapeDtypeStruct((M, W), BF16),
            jax.ShapeDtypeStruct((B, H, dk, dv), F32),
        ],
        compiler_params=_params("parallel", "arbitrary"),
        name="hgrn",
    )(z, z, z, z, lb_logits, norm_w, S0)


def _merge_kernel(a_ref, b_ref, c_ref, ga_ref, gb_ref, gc_ref, wa_ref, wb_ref, wc_ref, wo_ref, npost_ref, x_ref,
                  o_ref):
    sig = lambda r: _sigmoid(r[...].astype(F32))
    merged = (sig(ga_ref) * _dot(a_ref[...], wa_ref[...])
              + sig(gb_ref) * _dot(b_ref[...], wb_ref[...])
              + sig(gc_ref) * _dot(c_ref[...], wc_ref[...]))
    out = _dot(merged.astype(BF16), wo_ref[...])
    o_ref[...] = x_ref[...] + _rms(out, npost_ref[...])


def _merge(x, ya, yb, yc, z, P, l, tm):
    M, D = x.shape
    rows = lambda w: pl.BlockSpec((tm, w), lambda i: (i, 0))
    wspec = lambda k: pl.BlockSpec((None, k, D), lambda i: (l, 0, 0), pipeline_mode=pl.Buffered(1))
    return pl.pallas_call(
        _merge_kernel,
        grid=(M // tm,),
        in_specs=[
            rows(ya.shape[1]), rows(yb.shape[1]), rows(yc.shape[1]),
            pl.BlockSpec((tm, D), lambda i: (i, 0)),
            pl.BlockSpec((tm, D), lambda i: (i, 1)),
            pl.BlockSpec((tm, D), lambda i: (i, 2)),
            wspec(ya.shape[1]), wspec(yb.shape[1]), wspec(yc.shape[1]), wspec(D),
            pl.BlockSpec((None, 1, D), lambda i: (l, 0, 0)),
            rows(D),
        ],
        out_specs=rows(D),
        out_shape=jax.ShapeDtypeStruct((M, D), F32),
        compiler_params=_params("parallel"),
        name="merge",
    )(ya, yb, yc, z, z, z, P['w_br_a'], P['w_br_b'], P['w_br_c'], P['w_out'], P['mix_norm_post'], x)


def _xattn_kernel(x_ref, npre_ref, wq_ref, k_ref, v_ref, wo_ref, npost_ref, o_ref, *kv_scratch):
    D = x_ref.shape[-1]
    dh = D // XA_HEADS
    if kv_scratch:
        kb_ref, vb_ref = kv_scratch

        @pl.when(pl.program_id(1) == 0)
        def _():
            for h in range(XA_HEADS):
                kb_ref[:, h * dh:(h + 1) * dh] = k_ref[:, h, :].astype(BF16)
                vb_ref[:, h * dh:(h + 1) * dh] = v_ref[:, h, :].astype(BF16)
    else:
        kb_ref, vb_ref = k_ref, v_ref

    x = x_ref[...]
    q = _dot(_rms(x, npre_ref[...]).astype(BF16), wq_ref[...]).astype(BF16)
    outs = []
    for h in range(XA_HEADS):
        hs = slice(h * dh, (h + 1) * dh)
        s = _dot_nt(q[:, hs], kb_ref[:, hs]) * (dh ** -0.5)
        e = jnp.exp(s - jnp.max(s, axis=-1, keepdims=True))
        p = e / jnp.sum(e, axis=-1, keepdims=True)
        outs.append(_dot(p.astype(BF16), vb_ref[:, hs]).astype(BF16))
    out = _dot(jnp.concatenate(outs, axis=1), wo_ref[...])
    o_ref[...] = x + _rms(out, npost_ref[...])


def _xattn(x, mem_k, mem_v, P, l, lk, B, T, tq):
    M, D = x.shape
    n_mem = mem_k.shape[2]
    nT = T // tq
    wspec = lambda: pl.BlockSpec((None, D, D), lambda b, t: (l, 0, 0), pipeline_mode=pl.Buffered(1))
    nspec = lambda: pl.BlockSpec((None, 1, D), lambda b, t: (l, 0, 0))
    kv_block = (None, None) + mem_k.shape[2:]
    kv_index = lambda b, t: (lk, b) + (0,) * (mem_k.ndim - 2)
    kvspec = lambda: pl.BlockSpec(kv_block, kv_index)
    scratch = [] if mem_k.dtype == BF16 else [pltpu.VMEM((n_mem, D), BF16), pltpu.VMEM((n_mem, D), BF16)]
    return pl.pallas_call(
        _xattn_kernel,
        grid=(B, nT),
        in_specs=[
            pl.BlockSpec((tq, D), lambda b, t: (b * nT + t, 0)),
            nspec(), wspec(), kvspec(), kvspec(), wspec(), nspec(),
        ],
        out_specs=pl.BlockSpec((tq, D), lambda b, t: (b * nT + t, 0)),
        out_shape=jax.ShapeDtypeStruct((M, D), F32),
        scratch_shapes=scratch,
        compiler_params=_params("parallel", "arbitrary"),
        name="xattn",
    )(x, P['xa_norm_pre'], P['xa_wq'], mem_k, mem_v, P['xa_wo'], P['xa_norm_post'])


def _xattn_short_kernel(x_ref, npre_ref, wq_ref, k_ref, v_ref, wo_ref, npost_ref, o_ref, q_ref, a_ref, *, T):
    b = pl.program_id(0)
    D = x_ref.shape[-1]
    dh = D // XA_HEADS

    @pl.when(b == 0)
    def _():
        q_ref[...] = _dot(_rms(x_ref[...], npre_ref[...]).astype(BF16), wq_ref[...]).astype(BF16)

    r0 = pl.multiple_of(b * T, T)
    q = q_ref[pl.ds(r0, T), :]
    outs = []
    for h in range(XA_HEADS):
        hs = slice(h * dh, (h + 1) * dh)
        kh = k_ref[:, hs] if k_ref.dtype == BF16 else k_ref[:, h, :].astype(BF16)
        vh = v_ref[:, hs] if v_ref.dtype == BF16 else v_ref[:, h, :].astype(BF16)
        s = _dot_nt(q[:, hs], kh) * (dh ** -0.5)
        e = jnp.exp(s - jnp.max(s, axis=-1, keepdims=True))
        p = e / jnp.sum(e, axis=-1, keepdims=True)
        outs.append(_dot(p.astype(BF16), vh).astype(BF16))
    a_ref[pl.ds(r0, T), :] = jnp.concatenate(outs, axis=1)

    @pl.when(b == pl.num_programs(0) - 1)
    def _():
        o_ref[...] = x_ref[...] + _rms(_dot(a_ref[...], wo_ref[...]), npost_ref[...])


def _xattn_short(x, mem_k, mem_v, P, l, lk, B, T):
    M, D = x.shape
    wspec = lambda: pl.BlockSpec((None, D, D), lambda b: (l, 0, 0), pipeline_mode=pl.Buffered(1))
    nspec = lambda: pl.BlockSpec((None, 1, D), lambda b: (l, 0, 0))
    kvspec = lambda: pl.BlockSpec((None, None) + mem_k.shape[2:], lambda b: (lk, b) + (0,) * (mem_k.ndim - 2))
    return pl.pallas_call(
        functools.partial(_xattn_short_kernel, T=T),
        grid=(B,),
        in_specs=[pl.BlockSpec((M, D), lambda b: (0, 0)), nspec(), wspec(), kvspec(), kvspec(), wspec(), nspec()],
        out_specs=pl.BlockSpec((M, D), lambda b: (0, 0)),
        out_shape=jax.ShapeDtypeStruct((M, D), F32),
        scratch_shapes=[pltpu.VMEM((M, D), BF16), pltpu.VMEM((M, D), BF16)],
        compiler_params=_params("arbitrary"),
        name="xattn_short",
    )(x, P['xa_norm_pre'], P['xa_wq'], mem_k, mem_v, P['xa_wo'], P['xa_norm_post'])


def _mem_kv_kernel(m_ref, nw_ref, wk_ref, wv_ref, k_ref, v_ref, kb_ref, vb_ref):
    H, dh = k_ref.shape[-2:]
    hn = _rms(m_ref[...], nw_ref[...]).astype(BF16)
    k = _dot(hn, wk_ref[...])
    v = _dot(hn, wv_ref[...])
    kb_ref[...] = k.astype(BF16)
    vb_ref[...] = v.astype(BF16)
    for h in range(H):
        k_ref[:, h, :] = k[:, h * dh:(h + 1) * dh]
        v_ref[:, h, :] = v[:, h * dh:(h + 1) * dh]


def _mem_kv(mem, P, H):
    B, n_mem, D = mem.shape
    depth = P['xa_wk'].shape[0]
    out = jax.ShapeDtypeStruct((depth, B, n_mem, H, D // H), F32)
    outb = jax.ShapeDtypeStruct((depth, B, n_mem, D), BF16)
    ospec = lambda: pl.BlockSpec((None, None, n_mem, H, D // H), lambda l, b: (l, b, 0, 0, 0))
    obspec = lambda: pl.BlockSpec((None, None, n_mem, D), lambda l, b: (l, b, 0, 0))
    wspec = lambda: pl.BlockSpec((None, D, D), lambda l, b: (l, 0, 0))
    return pl.pallas_call(
        _mem_kv_kernel,
        grid=(depth, B),
        in_specs=[
            pl.BlockSpec((None, n_mem, D), lambda l, b: (b, 0, 0)),
            pl.BlockSpec((None, 1, D), lambda l, b: (l, 0, 0)),
            wspec(), wspec(),
        ],
        out_specs=[ospec(), ospec(), obspec(), obspec()],
        out_shape=[out, out, outb, outb],
        compiler_params=_params("parallel", "arbitrary"),
        name="mem_kv",
    )(mem, P['xa_mem_norm'], P['xa_wk'], P['xa_wv'])


def _prep_params(R):
    depth, D = R['ffn1_norm_pre'].shape
    P = {}
    row = lambda a: a.astype(F32).reshape(depth, 1, a.shape[-1])
    for name in ('ffn1_norm_pre', 'ffn1_norm_post', 'mix_norm_pre', 'mix_norm_post', 'xa_norm_pre', 'xa_mem_norm',
                 'xa_norm_post', 'ffn2_norm_pre', 'ffn2_norm_post', 'mlstm_norm', 'rg_conv_b', 'rg_ba', 'rg_bx',
                 'rg_lambda', 'hg_norm'):
        P[name] = row(R[name])
    P['rg_conv_w'] = R['rg_conv_w'].astype(F32)
    P['hg_lb_logits'] = R['hg_lb_logits'].astype(F32)

    for name in ('ffn1_w_in', 'ffn1_w_out', 'ffn2_w_in', 'ffn2_w_out'):
        P[name] = R[name]

    mw = R['w_br_a'].shape[1]
    rw = R['w_br_b'].shape[1]
    hw = R['w_br_c'].shape[1]
    nh = R['mlstm_bi'].shape[1]
    splits = (mw, mw, mw, mw, nh, nh, rw, rw, hw, hw, hw, hw, D, D, D)
    pts = [0] + [int(v) for v in np.cumsum(splits)]
    runs = ((12, 15), (0, 4), (6, 12))
    segs, d0 = [], 0
    for a, b in runs:
        segs.append((pts[a], pts[b] - pts[a], d0))
        d0 += pts[b] - pts[a]
    P['w_in_t'] = jnp.swapaxes(R['w_in'], 1, 2)
    P['w_in_regroup'] = (tuple(segs), pts[4], 2 * nh)
    P['gate_bias'] = jnp.pad(jnp.concatenate([R['mlstm_bi'], R['mlstm_bf']], axis=-1).astype(F32),
                             ((0, 0), (0, GATE_PAD - 2 * nh))).reshape(depth, 1, GATE_PAD)

    eye = jnp.eye(RG_BLOCKS, dtype=F32)
    dense = lambda w: jnp.einsum('lnde,nm->lndme', w, eye).reshape(depth, rw, rw).astype(BF16)
    P['rg_wa'] = dense(R['rg_wa'])
    P['rg_wx'] = dense(R['rg_wx'])

    for name in ('w_br_a', 'w_br_b', 'w_br_c', 'w_out', 'xa_wq', 'xa_wk', 'xa_wv', 'xa_wo'):
        P[name] = R[name].astype(BF16)
    return P


def _ffn_sublayer(x, P, name, l, bf16_weights):
    npre, npost = P[name + '_norm_pre'], P[name + '_norm_post']
    M = x.shape[0]
    tm = min(FFN_ROWS, M)
    if (name, l) in bf16_weights:
        return _ffn(x, npre, *bf16_weights[(name, l)], npost, l, tm)
    y, wg, wu, wo = _ffn_first(x, npre, P[name + '_w_in'], P[name + '_w_out'], npost, l, tm)
    bf16_weights[(name, l)] = (wg, wu, wo)
    return y if M == tm else _ffn(x, npre, wg, wu, wo, npost, l, tm, first=y)


def _run_trunk(x3, states, state_has_layers, mem, mem_k, mem_v, P, bf16_weights):
    B, T, D = x3.shape
    M = B * T
    depth = P['ffn1_norm_pre'].shape[0]
    x = x3.reshape(M, D)
    tm_proj = min(1024, M)
    tm_merge = min(512, M)
    tq = min(512, T)
    C0, n0, m0, hr0, buf0, S0 = states
    m0 = m0.reshape(m0.shape[0], m0.shape[1], 1, m0.shape[2])
    hr0 = hr0.reshape(hr0.shape[0], hr0.shape[1], 1, hr0.shape[2])
    mw = P['w_br_a'].shape[1]
    rw = P['w_br_b'].shape[1]
    hw = P['w_br_c'].shape[1]
    col_m = 3 * D // mw
    col_r = (3 * D + 4 * mw) // rw
    col_h = (3 * D + 4 * mw + 2 * rw) // hw

    if mem is not None:
        mem_k, mem_v, att_k, att_v = _mem_kv(mem, P, XA_HEADS)
    else:
        att_k, att_v = mem_k, mem_v

    outs = [[] for _ in range(6)]
    for l in range(depth):
        ls = l if state_has_layers else 0
        x = _ffn_sublayer(x, P, 'ffn1', l, bf16_weights)
        if ('mix', l) in bf16_weights:
            z, gates = _mix_in(x, P['mix_norm_pre'], *bf16_weights[('mix', l)], l, tm_proj, 1024)
        else:
            z, gates, w_mix, w_gate = _mix_in_first(x, P['mix_norm_pre'], P['w_in_t'], *P['w_in_regroup'], l, tm_proj)
            bf16_weights[('mix', l)] = (w_mix, w_gate)
            if M > tm_proj:
                z, gates = _mix_in(x, P['mix_norm_pre'], w_mix, w_gate, l, tm_proj, 1024, first=(z, gates))
        ya, C1, n1, m1 = _mlstm(z, gates, P['gate_bias'], P['mlstm_norm'], C0, n0, m0, l, ls, B, T, col_m)
        yb, buf1, hr1 = _rglru(z, P, buf0, hr0, l, ls, B, T, col_r)
        yc, S1 = _hgrn(z, P['hg_lb_logits'], P['hg_norm'], S0, l, ls, B, T, col_h)
        x = _merge(x, ya, yb, yc, z, P, l, tm_merge)
        if T % 16 == 0 and M <= XA_SHORT_ROWS:
            x = _xattn_short(x, att_k, att_v, P, l, l, B, T)
        else:
            x = _xattn(x, att_k, att_v, P, l, l, B, T, tq)
        x = _ffn_sublayer(x, P, 'ffn2', l, bf16_weights)
        for lst, s in zip(outs, (C1, n1, m1.reshape(B, -1), hr1.reshape(B, -1), buf1, S1)):
            lst.append(s)
    stacked = tuple(jnp.stack(lst) for lst in outs)
    return x.reshape(B, T, D), stacked, mem_k, mem_v


def kernel(x_prompt, x_sample, mem_prompt, cache_mem_k, cache_mem_v, state_mlstm_C, state_mlstm_n, state_mlstm_m, state_rglru_h, state_rglru_conv, state_hgrn_S, ffn1_norm_pre, ffn1_w_in, ffn1_w_out, ffn1_norm_post, mix_norm_pre, w_in, mlstm_bi, mlstm_bf, mlstm_norm, rg_conv_w, rg_conv_b, rg_wa, rg_ba, rg_wx, rg_bx, rg_lambda, hg_lb_logits, hg_norm, w_br_a, w_br_b, w_br_c, w_out, mix_norm_post, xa_norm_pre, xa_mem_norm, xa_wq, xa_wk, xa_wv, xa_wo, xa_norm_post, ffn2_norm_pre, ffn2_w_in, ffn2_w_out, ffn2_norm_post):
    R = dict(ffn1_norm_pre=ffn1_norm_pre, ffn1_w_in=ffn1_w_in, ffn1_w_out=ffn1_w_out, ffn1_norm_post=ffn1_norm_post,
             mix_norm_pre=mix_norm_pre, w_in=w_in, mlstm_bi=mlstm_bi, mlstm_bf=mlstm_bf, mlstm_norm=mlstm_norm,
             rg_conv_w=rg_conv_w, rg_conv_b=rg_conv_b, rg_wa=rg_wa, rg_ba=rg_ba, rg_wx=rg_wx, rg_bx=rg_bx,
             rg_lambda=rg_lambda, hg_lb_logits=hg_lb_logits, hg_norm=hg_norm,
             w_br_a=w_br_a, w_br_b=w_br_b, w_br_c=w_br_c, w_out=w_out, mix_norm_post=mix_norm_post,
             xa_norm_pre=xa_norm_pre, xa_mem_norm=xa_mem_norm, xa_wq=xa_wq, xa_wk=xa_wk, xa_wv=xa_wv,
             xa_wo=xa_wo, xa_norm_post=xa_norm_post,
             ffn2_norm_pre=ffn2_norm_pre, ffn2_w_in=ffn2_w_in, ffn2_w_out=ffn2_w_out, ffn2_norm_post=ffn2_norm_post)
    P = _prep_params(R)
    bf16_weights = {}
    B = x_prompt.shape[0]
    zeros_like_state = lambda s: jnp.zeros((1, B) + s.shape[2:], F32)
    init = tuple(zeros_like_state(s) for s in (state_mlstm_C, state_mlstm_n, state_mlstm_m, state_rglru_h,
                                               state_rglru_conv, state_hgrn_S))
    y_prompt, p_states, p_mem_k, p_mem_v = _run_trunk(x_prompt, init, False, mem_prompt, None, None, P, bf16_weights)
    s_init = (state_mlstm_C, state_mlstm_n, state_mlstm_m, state_rglru_h, state_rglru_conv, state_hgrn_S)
    y_sample, s_states, _, _ = _run_trunk(x_sample, s_init, True, None, cache_mem_k, cache_mem_v, P, bf16_weights)
    return (y_prompt, y_sample) + p_states + (p_mem_k, p_mem_v) + s_states
```

```python
import functools

import jax
import jax.numpy as jnp
import numpy as np
from jax import lax
from jax.experimental import pallas as pl
from jax.experimental.pallas import tpu as pltpu

F32 = jnp.float32
BF16 = jnp.bfloat16
EPS = 1e-6
LOG2_E = 1.4426950408889634

V7X_VMEM_LIMIT_BYTES = 56 * 1024 * 1024
LANES = 128

XA_HEADS = 4
MLSTM_HEADS = 4
HG_HEADS = 4
RG_BLOCKS = 8
RG_C = 8.0
CONV_W = 4
D_FF_TILE = 512
FFN_ROWS = 512
XA_SHORT_ROWS = 512
MLSTM_CHUNK = 256
RG_CHUNK = 512
HG_CHUNK = 256
GATE_PAD = LANES


def _params(*sem):
    return pltpu.CompilerParams(dimension_semantics=sem, vmem_limit_bytes=V7X_VMEM_LIMIT_BYTES)


def _rms(x, w):
    return x * lax.rsqrt(jnp.mean(x * x, axis=-1, keepdims=True) + EPS) * w


def _sigmoid(x):
    return 0.5 * jnp.tanh(0.5 * x) + 0.5


def _log_sigmoid(x):
    return jnp.minimum(x, 0.0) - jnp.log1p(jnp.exp(-jnp.abs(x)))


def _softplus(x):
    return jnp.maximum(x, 0.0) + jnp.log1p(jnp.exp(-jnp.abs(x)))


def _dot(a, b):
    return jnp.dot(a, b, preferred_element_type=F32)


def _dot_nt(a, b):
    return lax.dot_general(a, b, (((1,), (1,)), ((), ())), preferred_element_type=F32)


def _dot_tn(a, b):
    return lax.dot_general(a, b, (((0,), (0,)), ((), ())), preferred_element_type=F32)


def _shift_rows(x, d, fill, ridx):
    return jnp.where(ridx >= d, pltpu.roll(x, d, 0), fill)


def _ffn_step(j, nj, x_ref, npre_ref, weights, npost_ref, o_ref, hn_ref, acc_ref):
    @pl.when(j == 0)
    def _():
        hn_ref[...] = _rms(x_ref[...], npre_ref[...]).astype(BF16)
        acc_ref[...] = jnp.zeros_like(acc_ref)

    wg, wu, wo = weights()
    hn = hn_ref[...]
    g = _dot(hn, wg)
    u = _dot(hn, wu)
    a = (g * _sigmoid(g) * u).astype(BF16)
    acc_ref[...] += _dot(a, wo)

    @pl.when(j == nj - 1)
    def _():
        o_ref[...] = x_ref[...] + 0.5 * _rms(acc_ref[...], npost_ref[...])


def _ffn_kernel(x_ref, npre_ref, wg_ref, wu_ref, wo_ref, npost_ref, *rest, has_first):
    o_ref, hn_ref, acc_ref = rest[-3:]
    i, j, nj = pl.program_id(0), pl.program_id(1), pl.num_programs(1)

    def compute():
        _ffn_step(j, nj, x_ref, npre_ref, lambda: (wg_ref[...], wu_ref[...], wo_ref[...]), npost_ref,
                  o_ref, hn_ref, acc_ref)

    if not has_first:
        compute()
        return
    pl.when(i > 0)(compute)

    @pl.when((i == 0) & (j == nj - 1))
    def _():
        o_ref[...] = rest[0][...]


def _ffn(x, npre, wg, wu, wo, npost, l, tm, first=None):
    M, D = x.shape
    Fp = wo.shape[0]
    tn = D_FF_TILE
    nj = Fp // tn
    has_first = first is not None
    tile = (lambda i, j: jnp.where(i == 0, 0, j)) if has_first else (lambda i, j: j)
    args = [x, npre, wg, wu, wo, npost]
    in_specs = [
        pl.BlockSpec((tm, D), lambda i, j: (jnp.maximum(i, int(has_first)), 0)),
        pl.BlockSpec((None, 1, D), lambda i, j: (l, 0, 0)),
        pl.BlockSpec((None, D, tn), lambda i, j: (tile(i, j), 0, 0)),
        pl.BlockSpec((None, D, tn), lambda i, j: (tile(i, j), 0, 0)),
        pl.BlockSpec((tn, D), lambda i, j: (tile(i, j), 0)),
        pl.BlockSpec((None, 1, D), lambda i, j: (l, 0, 0)),
    ]
    if has_first:
        args.append(first)
        in_specs.append(pl.BlockSpec((tm, D), lambda i, j: (0, 0)))
    return pl.pallas_call(
        functools.partial(_ffn_kernel, has_first=has_first),
        grid=(M // tm, nj),
        in_specs=in_specs,
        out_specs=pl.BlockSpec((tm, D), lambda i, j: (i, 0)),
        out_shape=jax.ShapeDtypeStruct((M, D), F32),
        scratch_shapes=[pltpu.VMEM((tm, D), BF16), pltpu.VMEM((tm, D), F32)],
        compiler_params=_params("parallel", "arbitrary"),
        name="ffn",
    )(*args)


def _ffn_first_kernel(x_ref, npre_ref, g32_ref, u0_ref, u32_ref, wo32_ref, npost_ref,
                      o_ref, wg_ref, wu_ref, wo_ref, hn_ref, acc_ref, uprev_ref, *, F, shift):
    j = pl.program_id(0)
    tn = wg_ref.shape[-1]

    def weights():
        col_ok = lax.broadcasted_iota(jnp.int32, (1, tn), 1) + j * tn < F
        row_ok = lax.broadcasted_iota(jnp.int32, (tn, 1), 0) + j * tn < F
        if shift:
            @pl.when(j == 0)
            def _():
                uprev_ref[...] = u0_ref[...]

            u32 = jnp.concatenate([uprev_ref[:, shift:], u32_ref[:, :shift]], axis=1)
            uprev_ref[...] = u32_ref[...]
        else:
            u32 = u32_ref[...]
        wg = jnp.where(col_ok, g32_ref[...], 0.0).astype(BF16)
        wu = jnp.where(col_ok, u32, 0.0).astype(BF16)
        wo = jnp.where(row_ok, wo32_ref[...], 0.0).astype(BF16)
        wg_ref[...] = wg
        wu_ref[...] = wu
        wo_ref[...] = wo
        return wg, wu, wo

    _ffn_step(j, pl.num_programs(0), x_ref, npre_ref, weights, npost_ref, o_ref, hn_ref, acc_ref)


def _ffn_first(x, npre, w_in, w_out, npost, l, tm, tn=256):
    D = x.shape[1]
    F = w_out.shape[1]
    Fp = -(-F // D_FF_TILE) * D_FF_TILE
    nj = Fp // tn
    per = D_FF_TILE // tn
    q, shift = divmod(F, tn)
    last_in = -(-2 * F // tn) - 1
    return pl.pallas_call(
        functools.partial(_ffn_first_kernel, F=F, shift=shift),
        grid=(nj,),
        in_specs=[
            pl.BlockSpec((tm, D), lambda j: (0, 0)),
            pl.BlockSpec((None, 1, D), lambda j: (l, 0, 0)),
            pl.BlockSpec((None, D, tn), lambda j: (l, 0, j)),
            pl.BlockSpec((None, D, tn), lambda j: (l, 0, q)),
            pl.BlockSpec((None, D, tn), lambda j: (l, 0, jnp.minimum(q + j + (1 if shift else 0), last_in))),
            pl.BlockSpec((None, tn, D), lambda j: (l, j, 0)),
            pl.BlockSpec((None, 1, D), lambda j: (l, 0, 0)),
        ],
        out_specs=[
            pl.BlockSpec((tm, D), lambda j: (0, 0)),
            pl.BlockSpec((None, D, tn), lambda j: (j // per, 0, j % per)),
            pl.BlockSpec((None, D, tn), lambda j: (j // per, 0, j % per)),
            pl.BlockSpec((tn, D), lambda j: (j, 0)),
        ],
        out_shape=[
            jax.ShapeDtypeStruct((tm, D), F32),
            jax.ShapeDtypeStruct((Fp // D_FF_TILE, D, D_FF_TILE), BF16),
            jax.ShapeDtypeStruct((Fp // D_FF_TILE, D, D_FF_TILE), BF16),
            jax.ShapeDtypeStruct((Fp, D), BF16),
        ],
        scratch_shapes=[pltpu.VMEM((tm, D), BF16), pltpu.VMEM((tm, D), F32), pltpu.VMEM((D, tn), F32)],
        compiler_params=_params("arbitrary"),
        name="ffn_first",
    )(x, npre, w_in, w_in, w_in, w_out, npost)


def _nmm_kernel(x_ref, nw_ref, w_ref, o_ref, hn_ref):
    @pl.when(pl.program_id(1) == 0)
    def _():
        hn_ref[...] = _rms(x_ref[...], nw_ref[...]).astype(BF16)

    o_ref[...] = _dot(hn_ref[...], w_ref[...]).astype(o_ref.dtype)


def _norm_matmul(x, nw, w, l, out_dtype, tm, tn, name):
    M, D = x.shape
    N = w.shape[-1]
    return pl.pallas_call(
        _nmm_kernel,
        grid=(M // tm, N // tn),
        in_specs=[
            pl.BlockSpec((tm, D), lambda i, j: (i, 0)),
            pl.BlockSpec((None, 1, D), lambda i, j: (l, 0, 0)),
            pl.BlockSpec((None, D, tn), lambda i, j: (l, 0, j)),
        ],
        out_specs=pl.BlockSpec((tm, tn), lambda i, j: (i, j)),
        out_shape=jax.ShapeDtypeStruct((M, N), out_dtype),
        scratch_shapes=[pltpu.VMEM((tm, D), BF16)],
        compiler_params=_params("parallel", "arbitrary"),
        name=name,
    )(x, nw, w)


def _mix_in_kernel(x_ref, nw_ref, w_ref, wg_ref, *rest, has_first):
    z_ref, g_ref, hn_ref = rest[-3:]
    i, j = pl.program_id(0), pl.program_id(1)

    def compute():
        @pl.when(j == 0)
        def _():
            hn = _rms(x_ref[...], nw_ref[...]).astype(BF16)
            hn_ref[...] = hn
            g_ref[...] = _dot_nt(hn, wg_ref[...])

        z_ref[...] = _dot_nt(hn_ref[...], w_ref[...]).astype(z_ref.dtype)

    if not has_first:
        compute()
        return
    pl.when(i > 0)(compute)

    @pl.when(i == 0)
    def _():
        z_ref[...] = rest[0][...]

    @pl.when((i == 0) & (j == 0))
    def _():
        g_ref[...] = rest[1][...]


def _mix_in(x, nw, w, wg, l, tm, tn, first=None):
    M, D = x.shape
    N, G = w.shape[0], wg.shape[0]
    has_first = first is not None
    tile = (lambda i, j: jnp.where(i == 0, 0, j)) if has_first else (lambda i, j: j)
    args = [x, nw, w, wg]
    in_specs = [
        pl.BlockSpec((tm, D), lambda i, j: (jnp.maximum(i, int(has_first)), 0)),
        pl.BlockSpec((None, 1, D), lambda i, j: (l, 0, 0)),
        pl.BlockSpec((tn, D), lambda i, j: (tile(i, j), 0)),
        pl.BlockSpec((G, D), lambda i, j: (0, 0)),
    ]
    if has_first:
        args += list(first)
        in_specs += [pl.BlockSpec((tm, tn), lambda i, j: (0, jnp.where(i == 0, j, 0))),
                     pl.BlockSpec((tm, G), lambda i, j: (0, 0))]
    return pl.pallas_call(
        functools.partial(_mix_in_kernel, has_first=has_first),
        grid=(M // tm, N // tn),
        in_specs=in_specs,
        out_specs=[pl.BlockSpec((tm, tn), lambda i, j: (i, j)), pl.BlockSpec((tm, G), lambda i, j: (i, 0))],
        out_shape=[jax.ShapeDtypeStruct((M, N), BF16), jax.ShapeDtypeStruct((M, G), F32)],
        scratch_shapes=[pltpu.VMEM((tm, D), BF16)],
        compiler_params=_params("parallel", "arbitrary"),
        name="mix_in",
    )(*args)


def _mix_in_first_kernel(x_ref, nw_ref, a_ref, b_ref, g32_ref, z_ref, gout_ref, w_ref, wg_ref, hn_ref,
                         *, r, aligned, ngate):
    j = pl.program_id(0)

    @pl.when(j == 0)
    def _():
        hn = _rms(x_ref[...], nw_ref[...]).astype(BF16)
        hn_ref[...] = hn
        rowi = lax.broadcasted_iota(jnp.int32, g32_ref.shape, 0)
        wg = jnp.where(rowi < ngate, g32_ref[...], 0.0).astype(BF16)
        wg_ref[...] = wg
        gout_ref[...] = _dot_nt(hn, wg)

    a = a_ref[...]
    w32 = a
    if r:
        is_aligned = functools.reduce(jnp.logical_or, [(j >= lo) & (j < hi) for lo, hi in aligned], j < 0)
        w32 = jnp.where(is_aligned, a, jnp.concatenate([a[r:], b_ref[...]], axis=0))
    w = w32.astype(BF16)
    w_ref[...] = w
    z_ref[...] = _dot_nt(hn_ref[...], w).astype(z_ref.dtype)


def _mix_in_first(x, nw, wt, segs, gate0, ngate, l, tm, tn=512):
    D = x.shape[1]
    N = sum(s[1] for s in segs)
    r = max(s0 % tn for s0, _, _ in segs)
    assert all(s0 % tn in (0, r) and cnt % tn == 0 and d0 % tn == 0 for s0, cnt, d0 in segs)
    assert r % 8 == 0 and gate0 % GATE_PAD == 0
    tiles = [(d0 // tn, (d0 + cnt) // tn, s0 // tn) for s0, cnt, d0 in segs]
    aligned = tuple((t0, t1) for (t0, t1, _), (s0, _, _) in zip(tiles, segs) if s0 % tn == 0)

    def src_block(j):
        blk = 0
        for t0, t1, a0 in tiles:
            blk = jnp.where((j >= t0) & (j < t1), a0 + j - t0, blk)
        return blk

    rb = max(r, 8)
    return pl.pallas_call(
        functools.partial(_mix_in_first_kernel, r=r, aligned=aligned, ngate=ngate),
        grid=(N // tn,),
        in_specs=[
            pl.BlockSpec((tm, D), lambda j: (0, 0)),
            pl.BlockSpec((None, 1, D), lambda j: (l, 0, 0)),
            pl.BlockSpec((None, tn, D), lambda j: (l, src_block(j), 0)),
            pl.BlockSpec((None, rb, D), lambda j: (l, (src_block(j) + 1) * (tn // rb), 0)),
            pl.BlockSpec((None, GATE_PAD, D), lambda j: (l, gate0 // GATE_PAD, 0)),
        ],
        out_specs=[
            pl.BlockSpec((tm, tn), lambda j: (0, j)),
            pl.BlockSpec((tm, GATE_PAD), lambda j: (0, 0)),
            pl.BlockSpec((tn, D), lambda j: (j, 0)),
            pl.BlockSpec((GATE_PAD, D), lambda j: (0, 0)),
        ],
        out_shape=[
            jax.ShapeDtypeStruct((tm, N), BF16),
            jax.ShapeDtypeStruct((tm, GATE_PAD), F32),
            jax.ShapeDtypeStruct((N, D), BF16),
            jax.ShapeDtypeStruct((GATE_PAD, D), BF16),
        ],
        scratch_shapes=[pltpu.VMEM((tm, D), BF16)],
        compiler_params=_params("arbitrary"),
        name="mix_in_first",
    )(x, nw, wt, wt, wt)


def _mlstm_kernel(q_ref, k_ref, v_ref, og_ref, g_ref, gb_ref, nw_ref, C0_ref, n0_ref, m0_ref,
                  hm_ref, C_ref, n_ref, m_ref, *, L, dh):
    H = MLSTM_HEADS

    @pl.when(pl.program_id(1) == 0)
    def _():
        C_ref[...] = C0_ref[...]
        n_ref[...] = n0_ref[...]
        m_ref[...] = m0_ref[...]

    row = lax.broadcasted_iota(jnp.int32, (L, L), 0)
    col = lax.broadcasted_iota(jnp.int32, (L, L), 1)
    causal = row >= col
    eye = row == col

    def as_row(x_col):
        return jnp.sum(jnp.where(eye, x_col, 0.0), axis=0, keepdims=True)

    for bb in range(q_ref.shape[0]):
        ga = g_ref[bb] + gb_ref[...]
        m_prev = m_ref[bb]
        gaT = ga.T if L % LANES == 0 else None
        for h in range(H):
            hs = slice(h * dh, (h + 1) * dh)
            ig_c = ga[:, h:h + 1]
            lf_c = _log_sigmoid(ga[:, H + h:H + h + 1])
            if gaT is not None:
                ig_r = gaT[h:h + 1, :]
                lf_r = _log_sigmoid(gaT[H + h:H + h + 1, :])
            else:
                ig_r = as_row(ig_c)
                lf_r = as_row(lf_c)
            b_c = jnp.sum(jnp.where(causal, lf_r, 0.0), axis=1, keepdims=True)
            b_r = jnp.sum(jnp.where(row <= col, lf_c, 0.0), axis=0, keepdims=True)
            m0 = m_prev[:, h:h + 1]
            src_r = ig_r - b_r
            peak_c = jnp.maximum(m0, jnp.max(jnp.where(causal, src_r, -jnp.inf), axis=1, keepdims=True))
            m_c = b_c + peak_c
            w = jnp.exp(jnp.where(causal, src_r - peak_c, -jnp.inf))
            w_inter = jnp.exp(m0 - peak_c)

            q = q_ref[bb, :, hs]
            k = k_ref[bb, :, hs] * (dh ** -0.5)
            v = v_ref[bb, :, hs]
            C0 = C_ref[bb, h]
            n0 = n_ref[bb, h:h + 1, :]
            wqk = w * _dot_nt(q, k)
            num = _dot(wqk.astype(BF16), v) + w_inter * _dot(q, C0.astype(BF16))
            den = (jnp.sum(wqk, axis=1, keepdims=True)
                   + w_inter * jnp.sum(q.astype(F32) * n0, axis=1, keepdims=True))
            hh = num / jnp.maximum(jnp.abs(den), jnp.exp(-m_c))
            hn = hh * lax.rsqrt(jnp.mean(hh * hh, axis=-1, keepdims=True) + EPS) * nw_ref[:, hs]
            hm_ref[bb, :, hs] = (_sigmoid(og_ref[bb, :, hs].astype(F32)) * hn).astype(BF16)

            m_end = m_c[L - 1:L, :]
            w_end = jnp.exp(b_c[L - 1:L, :] - b_c + ig_c - m_end)
            s_end = w_inter[L - 1:L, :]
            kw = k.astype(F32) * w_end
            C_ref[bb, h] = s_end * C0 + _dot_tn(kw.astype(BF16), v)
            n_ref[bb, h:h + 1, :] = s_end * n0 + jnp.sum(kw, axis=0, keepdims=True)
            m_ref[bb, :, h:h + 1] = m_end


def _mlstm(z, gates, gate_bias, norm_w, C0, n0, m0, l, ls, B, T, col0):
    H = MLSTM_HEADS
    dh = C0.shape[-1]
    W = H * dh
    L = min(MLSTM_CHUNK, T)
    nC = T // L
    M = B * T
    nb = 1
    z3 = z.reshape(B, T, z.shape[-1])
    g3 = gates.reshape(B, T, GATE_PAD)
    zspec = lambda cb: pl.BlockSpec((nb, L, W), lambda b, c: (b, c, cb))
    hm, C1, n1, m1 = pl.pallas_call(
        functools.partial(_mlstm_kernel, L=L, dh=dh),
        grid=(B // nb, nC),
        in_specs=[
            zspec(col0), zspec(col0 + 1), zspec(col0 + 2), zspec(col0 + 3),
            pl.BlockSpec((nb, L, GATE_PAD), lambda b, c: (b, c, 0)),
            pl.BlockSpec((None, 1, GATE_PAD), lambda b, c: (l, 0, 0)),
            pl.BlockSpec((None, 1, W), lambda b, c: (l, 0, 0)),
            pl.BlockSpec((None, nb, H, dh, dh), lambda b, c: (ls, b, 0, 0, 0)),
            pl.BlockSpec((None, nb, H, dh), lambda b, c: (ls, b, 0, 0)),
            pl.BlockSpec((None, nb, 1, H), lambda b, c: (ls, b, 0, 0)),
        ],
        out_specs=[
            pl.BlockSpec((nb, L, W), lambda b, c: (b, c, 0)),
            pl.BlockSpec((nb, H, dh, dh), lambda b, c: (b, 0, 0, 0)),
            pl.BlockSpec((nb, H, dh), lambda b, c: (b, 0, 0)),
            pl.BlockSpec((nb, 1, H), lambda b, c: (b, 0, 0)),
        ],
        out_shape=[
            jax.ShapeDtypeStruct((B, T, W), BF16),
            jax.ShapeDtypeStruct((B, H, dh, dh), F32),
            jax.ShapeDtypeStruct((B, H, dh), F32),
            jax.ShapeDtypeStruct((B, 1, H), F32),
        ],
        compiler_params=_params("parallel", "arbitrary"),
        name="mlstm",
    )(z3, z3, z3, z3, g3, gate_bias, norm_w, C0, n0, m0)
    return hm.reshape(M, W), C1, n1, m1


def _rglru_kernel(rx_ref, rg_ref, cw_ref, cbias_ref, wa_ref, ba_ref, wx_ref, bx_ref, lam_ref, buf0_ref, h0_ref,
                  y_ref, buf_ref, h_ref, cbuf_ref, *, L):
    c = pl.program_id(1)
    W = rx_ref.shape[-1]
    TAIL = CONV_W - 1

    @pl.when(c == 0)
    def _():
        cbuf_ref[0:8, :] = jnp.zeros((8, W), F32)
        cbuf_ref[8 - TAIL:8, :] = buf0_ref[0]
        h_ref[...] = h0_ref[...]

    cbuf_ref[8:8 + L, :] = rx_ref[...].astype(F32)
    xc = cbias_ref[...]
    for j in range(CONV_W):
        xc = xc + cbuf_ref[8 - TAIL + j:8 - TAIL + j + L, :] * cw_ref[j:j + 1, :]

    xcb = xc.astype(BF16)
    r = _sigmoid(_dot(xcb, wa_ref[...]) + ba_ref[...])
    i = _sigmoid(_dot(xcb, wx_ref[...]) + bx_ref[...])
    log_a = (-RG_C * _softplus(-lam_ref[...])) * r
    a = jnp.exp(log_a)
    th = jnp.tanh(log_a)
    u = jnp.sqrt(-2.0 * th / (1.0 - th)) * (i * xc)

    ridx = lax.broadcasted_iota(jnp.int32, (L, W), 0)
    d = 1
    while d < L:
        a_sh = _shift_rows(a, d, 1.0, ridx)
        u_sh = _shift_rows(u, d, 0.0, ridx)
        u = a * u_sh + u
        a = a * a_sh
        d *= 2
    h = a * h_ref[0] + u
    h_ref[0] = h[L - 1:L, :]
    y_ref[...] = (h * jax.nn.gelu(rg_ref[...].astype(F32))).astype(BF16)

    cbuf_ref[0:8, :] = cbuf_ref[L:L + 8, :]

    @pl.when(c == pl.num_programs(1) - 1)
    def _():
        buf_ref[0] = cbuf_ref[8 + L - TAIL:8 + L, :]


def _rglru(z, P, buf0, h0, l, ls, B, T, col0):
    W = h0.shape[-1]
    L = min(RG_CHUNK, T)
    nC = T // L
    M = B * T
    vec = lambda: pl.BlockSpec((None, 1, W), lambda b, c: (l, 0, 0))
    mat = lambda: pl.BlockSpec((None, W, W), lambda b, c: (l, 0, 0))
    return pl.pallas_call(
        functools.partial(_rglru_kernel, L=L),
        grid=(B, nC),
        in_specs=[
            pl.BlockSpec((L, W), lambda b, c: (b * nC + c, col0)),
            pl.BlockSpec((L, W), lambda b, c: (b * nC + c, col0 + 1)),
            pl.BlockSpec((None, CONV_W, W), lambda b, c: (l, 0, 0)),
            vec(), mat(), vec(), mat(), vec(), vec(),
            pl.BlockSpec((None, 1, CONV_W - 1, W), lambda b, c: (ls, b, 0, 0)),
            pl.BlockSpec((None, 1, 1, W), lambda b, c: (ls, b, 0, 0)),
        ],
        out_specs=[
            pl.BlockSpec((L, W), lambda b, c: (b * nC + c, 0)),
            pl.BlockSpec((1, CONV_W - 1, W), lambda b, c: (b, 0, 0)),
            pl.BlockSpec((1, 1, W), lambda b, c: (b, 0, 0)),
        ],
        out_shape=[
            jax.ShapeDtypeStruct((M, W), BF16),
            jax.ShapeDtypeStruct((B, CONV_W - 1, W), F32),
            jax.ShapeDtypeStruct((B, 1, W), F32),
        ],
        scratch_shapes=[pltpu.VMEM((L + 8, W), F32)],
        compiler_params=_params("parallel", "arbitrary"),
        name="rglru",
    )(z, z, P['rg_conv_w'], P['rg_conv_b'], P['rg_wa'], P['rg_ba'], P['rg_wx'], P['rg_bx'], P['rg_lambda'],
      buf0, h0)


def _block_ref_rows(cb, m, ridx):
    L, W = cb.shape
    n = 2 * m
    if m == 1:
        return jnp.where((ridx & 1) == 0, cb, pltpu.roll(cb, 1, 0))
    if m == 2:
        off = ridx & 3
        return jnp.where(off == 0, pltpu.roll(cb, L - 1, 0),
                         jnp.where(off == 1, cb, jnp.where(off == 2, pltpu.roll(cb, 1, 0), pltpu.roll(cb, 2, 0))))
    return jnp.concatenate(
        [jnp.broadcast_to(cb[j * n + m - 1:j * n + m, :], (n, W)) for j in range(L // n)], axis=0)


def _hgrn_kernel(q_ref, f_ref, i_ref, g_ref, lbl_ref, nw_ref, S0_ref, y_ref, S_ref, *, L, layer):
    H = HG_HEADS
    W = q_ref.shape[-1]
    dk = W // H

    @pl.when(pl.program_id(1) == 0)
    def _():
        S_ref[...] = S0_ref[...]

    lg = lbl_ref[...]
    e = jnp.exp(lg - jnp.max(lg, axis=0, keepdims=True))
    p = e / jnp.sum(e, axis=0, keepdims=True)
    lb = jnp.zeros((1, W), F32)
    for r in range(1, layer + 1):
        lb = lb + p[r:r + 1, :]

    f = lb + (1.0 - lb) * jax.nn.sigmoid(f_ref[...].astype(F32))
    cb = jnp.log(f)
    ridx = lax.broadcasted_iota(jnp.int32, (L, W), 0)
    d = 1
    while d < L:
        cb = cb + _shift_rows(cb, d, 0.0, ridx)
        d *= 2
    kk = 1.0 - f
    q = q_ref[...].astype(F32)
    v = i_ref[...].astype(F32)
    vb = v.astype(BF16)

    row = lax.broadcasted_iota(jnp.int32, (L, L), 0)
    col = lax.broadcasted_iota(jnp.int32, (L, L), 1)
    split = jnp.where(row > col, row ^ col, 0)
    a = [jnp.zeros((L, L), F32) for _ in range(H)]
    m = 1
    cb2 = cb * LOG2_E
    while m < L:
        dref = cb2 - _block_ref_rows(cb2, m, ridx)
        dec = jnp.exp2(-jnp.abs(dref))
        qt = (q * dec).astype(BF16)
        ks = (kk * dec).astype(BF16)
        level = (split // m) == 1
        for h in range(H):
            hs = slice(h * dk, (h + 1) * dk)
            a[h] = jnp.where(level, _dot_nt(qt[:, hs], ks[:, hs]), a[h])
        m *= 2

    cb_end = cb[L - 1:L, :]
    qe = (q * jnp.exp(cb)).astype(BF16)
    ke = (kk * jnp.exp(cb_end - cb)).astype(BF16)
    qk = q * kk
    eye = (lax.broadcasted_iota(jnp.int32, (dk, dk), 0) == lax.broadcasted_iota(jnp.int32, (dk, dk), 1))
    outs = []
    for h in range(H):
        hs = slice(h * dk, (h + 1) * dk)
        S = S_ref[0, h]
        o = (_dot(a[h].astype(BF16), vb[:, hs]) + jnp.sum(qk[:, hs], axis=1, keepdims=True) * v[:, hs]
             + _dot(qe[:, hs], S.astype(BF16)))
        dec_col = jnp.sum(jnp.where(eye, jnp.exp(cb_end[:, hs]), 0.0), axis=1, keepdims=True)
        S_ref[0, h] = dec_col * S + _dot_tn(ke[:, hs], vb[:, hs])
        outs.append(o * lax.rsqrt(jnp.mean(o * o, axis=-1, keepdims=True) + EPS))
    gate = g_ref[...].astype(F32)
    y_ref[...] = (jnp.concatenate(outs, axis=1) * nw_ref[...] * (gate * _sigmoid(gate))).astype(BF16)


def _hgrn(z, lb_logits, norm_w, S0, l, ls, B, T, col0):
    H, dk, dv = S0.shape[-3:]
    W = H * dk
    L = min(HG_CHUNK, T)
    nC = T // L
    M = B * T
    depth = lb_logits.shape[0]
    zspec = lambda cb: pl.BlockSpec((L, W), lambda b, c: (b * nC + c, cb))
    return pl.pallas_call(
        functools.partial(_hgrn_kernel, L=L, layer=l),
        grid=(B, nC),
        in_specs=[
            zspec(col0), zspec(col0 + 1), zspec(col0 + 2), zspec(col0 + 3),
            pl.BlockSpec((depth, W), lambda b, c: (0, 0)),
            pl.BlockSpec((None, 1, W), lambda b, c: (l, 0, 0)),
            pl.BlockSpec((None, 1, H, dk, dv), lambda b, c: (ls, b, 0, 0, 0)),
        ],
        out_specs=[
            pl.BlockSpec((L, W), lambda b, c: (b * nC + c, 0)),
            pl.BlockSpec((1, H, dk, dv), lambda b, c: (b, 0, 0, 0)),
        ],
        out_shape=[
            jax.ShapeDtypeStruct((M, W), BF16),
            jax.ShapeDtypeStruct((B, H, dk, dv), F32),
        ],
        compiler_params=_params("parallel", "arbitrary"),
        name="hgrn",
    )(z, z, z, z, lb_logits, norm_w, S0)


def _merge_kernel(a_ref, b_ref, c_ref, ga_ref, gb_ref, gc_ref, wa_ref, wb_ref, wc_ref, wo_ref, npost_ref, x_ref,
                  o_ref):
    sig = lambda r: _sigmoid(r[...].astype(F32))
    merged = (sig(ga_ref) * _dot(a_ref[...], wa_ref[...])
              + sig(gb_ref) * _dot(b_ref[...], wb_ref[...])
              + sig(gc_ref) * _dot(c_ref[...], wc_ref[...]))
    out = _dot(merged.astype(BF16), wo_ref[...])
    o_ref[...] = x_ref[...] + _rms(out, npost_ref[...])


def _merge(x, ya, yb, yc, z, P, l, tm):
    M, D = x.shape
    rows = lambda w: pl.BlockSpec((tm, w), lambda i: (i, 0))
    wspec = lambda k: pl.BlockSpec((None, k, D), lambda i: (l, 0, 0), pipeline_mode=pl.Buffered(1))
    return pl.pallas_call(
        _merge_kernel,
        grid=(M // tm,),
        in_specs=[
            rows(ya.shape[1]), rows(yb.shape[1]), rows(yc.shape[1]),
            pl.BlockSpec((tm, D), lambda i: (i, 0)),
            pl.BlockSpec((tm, D), lambda i: (i, 1)),
            pl.BlockSpec((tm, D), lambda i: (i, 2)),
            wspec(ya.shape[1]), wspec(yb.shape[1]), wspec(yc.shape[1]), wspec(D),
            pl.BlockSpec((None, 1, D), lambda i: (l, 0, 0)),
            rows(D),
        ],
        out_specs=rows(D),
        out_shape=jax.ShapeDtypeStruct((M, D), F32),
        compiler_params=_params("parallel"),
        name="merge",
    )(ya, yb, yc, z, z, z, P['w_br_a'], P['w_br_b'], P['w_br_c'], P['w_out'], P['mix_norm_post'], x)


def _xattn_kernel(x_ref, npre_ref, wq_ref, k_ref, v_ref, wo_ref, npost_ref, o_ref, *kv_scratch):
    D = x_ref.shape[-1]
    dh = D // XA_HEADS
    if kv_scratch:
        kb_ref, vb_ref = kv_scratch

        @pl.when(pl.program_id(1) == 0)
        def _():
            for h in range(XA_HEADS):
                kb_ref[:, h * dh:(h + 1) * dh] = k_ref[:, h, :].astype(BF16)
                vb_ref[:, h * dh:(h + 1) * dh] = v_ref[:, h, :].astype(BF16)
    else:
        kb_ref, vb_ref = k_ref, v_ref

    x = x_ref[...]
    q = _dot(_rms(x, npre_ref[...]).astype(BF16), wq_ref[...]).astype(BF16)
    outs = []
    for h in range(XA_HEADS):
        hs = slice(h * dh, (h + 1) * dh)
        s = _dot_nt(q[:, hs], kb_ref[:, hs]) * (dh ** -0.5)
        e = jnp.exp(s - jnp.max(s, axis=-1, keepdims=True))
        p = e / jnp.sum(e, axis=-1, keepdims=True)
        outs.append(_dot(p.astype(BF16), vb_ref[:, hs]).astype(BF16))
    out = _dot(jnp.concatenate(outs, axis=1), wo_ref[...])
    o_ref[...] = x + _rms(out, npost_ref[...])


def _xattn(x, mem_k, mem_v, P, l, lk, B, T, tq):
    M, D = x.shape
    n_mem = mem_k.shape[2]
    nT = T // tq
    wspec = lambda: pl.BlockSpec((None, D, D), lambda b, t: (l, 0, 0), pipeline_mode=pl.Buffered(1))
    nspec = lambda: pl.BlockSpec((None, 1, D), lambda b, t: (l, 0, 0))
    kv_block = (None, None) + mem_k.shape[2:]
    kv_index = lambda b, t: (lk, b) + (0,) * (mem_k.ndim - 2)
    kvspec = lambda: pl.BlockSpec(kv_block, kv_index)
    scratch = [] if mem_k.dtype == BF16 else [pltpu.VMEM((n_mem, D), BF16), pltpu.VMEM((n_mem, D), BF16)]
    return pl.pallas_call(
        _xattn_kernel,
        grid=(B, nT),
        in_specs=[
            pl.BlockSpec((tq, D), lambda b, t: (b * nT + t, 0)),
            nspec(), wspec(), kvspec(), kvspec(), wspec(), nspec(),
        ],
        out_specs=pl.BlockSpec((tq, D), lambda b, t: (b * nT + t, 0)),
        out_shape=jax.ShapeDtypeStruct((M, D), F32),
        scratch_shapes=scratch,
        compiler_params=_params("parallel", "arbitrary"),
        name="xattn",
    )(x, P['xa_norm_pre'], P['xa_wq'], mem_k, mem_v, P['xa_wo'], P['xa_norm_post'])


def _xattn_short_kernel(x_ref, npre_ref, wq_ref, k_ref, v_ref, wo_ref, npost_ref, o_ref, q_ref, a_ref, *, T):
    b = pl.program_id(0)
    D = x_ref.shape[-1]
    dh = D // XA_HEADS

    @pl.when(b == 0)
    def _():
        q_ref[...] = _dot(_rms(x_ref[...], npre_ref[...]).astype(BF16), wq_ref[...]).astype(BF16)

    r0 = pl.multiple_of(b * T, T)
    q = q_ref[pl.ds(r0, T), :]
    outs = []
    for h in range(XA_HEADS):
        hs = slice(h * dh, (h + 1) * dh)
        kh = k_ref[:, hs] if k_ref.dtype == BF16 else k_ref[:, h, :].astype(BF16)
        vh = v_ref[:, hs] if v_ref.dtype == BF16 else v_ref[:, h, :].astype(BF16)
        s = _dot_nt(q[:, hs], kh) * (dh ** -0.5)
        e = jnp.exp(s - jnp.max(s, axis=-1, keepdims=True))
        p = e / jnp.sum(e, axis=-1, keepdims=True)
        outs.append(_dot(p.astype(BF16), vh).astype(BF16))
    a_ref[pl.ds(r0, T), :] = jnp.concatenate(outs, axis=1)

    @pl.when(b == pl.num_programs(0) - 1)
    def _():
        o_ref[...] = x_ref[...] + _rms(_dot(a_ref[...], wo_ref[...]), npost_ref[...])


def _xattn_short(x, mem_k, mem_v, P, l, lk, B, T):
    M, D = x.shape
    wspec = lambda: pl.BlockSpec((None, D, D), lambda b: (l, 0, 0), pipeline_mode=pl.Buffered(1))
    nspec = lambda: pl.BlockSpec((None, 1, D), lambda b: (l, 0, 0))
    kvspec = lambda: pl.BlockSpec((None, None) + mem_k.shape[2:], lambda b: (lk, b) + (0,) * (mem_k.ndim - 2))
    return pl.pallas_call(
        functools.partial(_xattn_short_kernel, T=T),
        grid=(B,),
        in_specs=[pl.BlockSpec((M, D), lambda b: (0, 0)), nspec(), wspec(), kvspec(), kvspec(), wspec(), nspec()],
        out_specs=pl.BlockSpec((M, D), lambda b: (0, 0)),
        out_shape=jax.ShapeDtypeStruct((M, D), F32),
        scratch_shapes=[pltpu.VMEM((M, D), BF16), pltpu.VMEM((M, D), BF16)],
        compiler_params=_params("arbitrary"),
        name="xattn_short",
    )(x, P['xa_norm_pre'], P['xa_wq'], mem_k, mem_v, P['xa_wo'], P['xa_norm_post'])


def _mem_kv_kernel(m_ref, nw_ref, wk_ref, wv_ref, k_ref, v_ref, kb_ref, vb_ref):
    H, dh = k_ref.shape[-2:]
    hn = _rms(m_ref[...], nw_ref[...]).astype(BF16)
    k = _dot(hn, wk_ref[...])
    v = _dot(hn, wv_ref[...])
    kb_ref[...] = k.astype(BF16)
    vb_ref[...] = v.astype(BF16)
    for h in range(H):
        k_ref[:, h, :] = k[:, h * dh:(h + 1) * dh]
        v_ref[:, h, :] = v[:, h * dh:(h + 1) * dh]


def _mem_kv(mem, P, H):
    B, n_mem, D = mem.shape
    depth = P['xa_wk'].shape[0]
    out = jax.ShapeDtypeStruct((depth, B, n_mem, H, D // H), F32)
    outb = jax.ShapeDtypeStruct((depth, B, n_mem, D), BF16)
    ospec = lambda: pl.BlockSpec((None, None, n_mem, H, D // H), lambda l, b: (l, b, 0, 0, 0))
    obspec = lambda: pl.BlockSpec((None, None, n_mem, D), lambda l, b: (l, b, 0, 0))
    wspec = lambda: pl.BlockSpec((None, D, D), lambda l, b: (l, 0, 0))
    return pl.pallas_call(
        _mem_kv_kernel,
        grid=(depth, B),
        in_specs=[
            pl.BlockSpec((None, n_mem, D), lambda l, b: (b, 0, 0)),
            pl.BlockSpec((None, 1, D), lambda l, b: (l, 0, 0)),
            wspec(), wspec(),
        ],
        out_specs=[ospec(), ospec(), obspec(), obspec()],
        out_shape=[out, out, outb, outb],
        compiler_params=_params("parallel", "arbitrary"),
        name="mem_kv",
    )(mem, P['xa_mem_norm'], P['xa_wk'], P['xa_wv'])


def _prep_params(R):
    depth, D = R['ffn1_norm_pre'].shape
    P = {}
    row = lambda a: a.astype(F32).reshape(depth, 1, a.shape[-1])
    for name in ('ffn1_norm_pre', 'ffn1_norm_post', 'mix_norm_pre', 'mix_norm_post', 'xa_norm_pre', 'xa_mem_norm',
                 'xa_norm_post', 'ffn2_norm_pre', 'ffn2_norm_post', 'mlstm_norm', 'rg_conv_b', 'rg_ba', 'rg_bx',
                 'rg_lambda', 'hg_norm'):
        P[name] = row(R[name])
    P['rg_conv_w'] = R['rg_conv_w'].astype(F32)
    P['hg_lb_logits'] = R['hg_lb_logits'].astype(F32)

    for name in ('ffn1_w_in', 'ffn1_w_out', 'ffn2_w_in', 'ffn2_w_out'):
        P[name] = R[name]

    mw = R['w_br_a'].shape[1]
    rw = R['w_br_b'].shape[1]
    hw = R['w_br_c'].shape[1]
    nh = R['mlstm_bi'].shape[1]
    splits = (mw, mw, mw, mw, nh, nh, rw, rw, hw, hw, hw, hw, D, D, D)
    pts = [0] + [int(v) for v in np.cumsum(splits)]
    runs = ((12, 15), (0, 4), (6, 12))
    segs, d0 = [], 0
    for a, b in runs:
        segs.append((pts[a], pts[b] - pts[a], d0))
        d0 += pts[b] - pts[a]
    P['w_in_t'] = jnp.swapaxes(R['w_in'], 1, 2)
    P['w_in_regroup'] = (tuple(segs), pts[4], 2 * nh)
    P['gate_bias'] = jnp.pad(jnp.concatenate([R['mlstm_bi'], R['mlstm_bf']], axis=-1).astype(F32),
                             ((0, 0), (0, GATE_PAD - 2 * nh))).reshape(depth, 1, GATE_PAD)

    eye = jnp.eye(RG_BLOCKS, dtype=F32)
    dense = lambda w: jnp.einsum('lnde,nm->lndme', w, eye).reshape(depth, rw, rw).astype(BF16)
    P['rg_wa'] = dense(R['rg_wa'])
    P['rg_wx'] = dense(R['rg_wx'])

    for name in ('w_br_a', 'w_br_b', 'w_br_c', 'w_out', 'xa_wq', 'xa_wk', 'xa_wv', 'xa_wo'):
        P[name] = R[name].astype(BF16)
    return P


def _ffn_sublayer(x, P, name, l, bf16_weights):
    npre, npost = P[name + '_norm_pre'], P[name + '_norm_post']
    M = x.shape[0]
    tm = min(FFN_ROWS, M)
    if (name, l) in bf16_weights:
        return _ffn(x, npre, *bf16_weights[(name, l)], npost, l, tm)
    y, wg, wu, wo = _ffn_first(x, npre, P[name + '_w_in'], P[name + '_w_out'], npost, l, tm)
    bf16_weights[(name, l)] = (wg, wu, wo)
    return y if M == tm else _ffn(x, npre, wg, wu, wo, npost, l, tm, first=y)


def _run_trunk(x3, states, state_has_layers, mem, mem_k, mem_v, P, bf16_weights):
    B, T, D = x3.shape
    M = B * T
    depth = P['ffn1_norm_pre'].shape[0]
    x = x3.reshape(M, D)
    tm_proj = min(1024, M)
    tm_merge = min(512, M)
    tq = min(512, T)
    C0, n0, m0, hr0, buf0, S0 = states
    m0 = m0.reshape(m0.shape[0], m0.shape[1], 1, m0.shape[2])
    hr0 = hr0.reshape(hr0.shape[0], hr0.shape[1], 1, hr0.shape[2])
    mw = P['w_br_a'].shape[1]
    rw = P['w_br_b'].shape[1]
    hw = P['w_br_c'].shape[1]
    col_m = 3 * D // mw
    col_r = (3 * D + 4 * mw) // rw
    col_h = (3 * D + 4 * mw + 2 * rw) // hw

    if mem is not None:
        mem_k, mem_v, att_k, att_v = _mem_kv(mem, P, XA_HEADS)
    else:
        att_k, att_v = mem_k, mem_v

    outs = [[] for _ in range(6)]
    for l in range(depth):
        ls = l if state_has_layers else 0
        x = _ffn_sublayer(x, P, 'ffn1', l, bf16_weights)
        if ('mix', l) in bf16_weights:
            z, gates = _mix_in(x, P['mix_norm_pre'], *bf16_weights[('mix', l)], l, tm_proj, 1024)
        else:
            z, gates, w_mix, w_gate = _mix_in_first(x, P['mix_norm_pre'], P['w_in_t'], *P['w_in_regroup'], l, tm_proj)
            bf16_weights[('mix', l)] = (w_mix, w_gate)
            if M > tm_proj:
                z, gates = _mix_in(x, P['mix_norm_pre'], w_mix, w_gate, l, tm_proj, 1024, first=(z, gates))
        ya, C1, n1, m1 = _mlstm(z, gates, P['gate_bias'], P['mlstm_norm'], C0, n0, m0, l, ls, B, T, col_m)
        yb, buf1, hr1 = _rglru(z, P, buf0, hr0, l, ls, B, T, col_r)
        yc, S1 = _hgrn(z, P['hg_lb_logits'], P['hg_norm'], S0, l, ls, B, T, col_h)
        x = _merge(x, ya, yb, yc, z, P, l, tm_merge)
        if T % 16 == 0 and M <= XA_SHORT_ROWS:
            x = _xattn_short(x, att_k, att_v, P, l, l, B, T)
        else:
            x = _xattn(x, att_k, att_v, P, l, l, B, T, tq)
        x = _ffn_sublayer(x, P, 'ffn2', l, bf16_weights)
        for lst, s in zip(outs, (C1, n1, m1.reshape(B, -1), hr1.reshape(B, -1), buf1, S1)):
            lst.append(s)
    stacked = tuple(jnp.stack(lst) for lst in outs)
    return x.reshape(B, T, D), stacked, mem_k, mem_v


def kernel(x_prompt, x_sample, mem_prompt, cache_mem_k, cache_mem_v, state_mlstm_C, state_mlstm_n, state_mlstm_m, state_rglru_h, state_rglru_conv, state_hgrn_S, ffn1_norm_pre, ffn1_w_in, ffn1_w_out, ffn1_norm_post, mix_norm_pre, w_in, mlstm_bi, mlstm_bf, mlstm_norm, rg_conv_w, rg_conv_b, rg_wa, rg_ba, rg_wx, rg_bx, rg_lambda, hg_lb_logits, hg_norm, w_br_a, w_br_b, w_br_c, w_out, mix_norm_post, xa_norm_pre, xa_mem_norm, xa_wq, xa_wk, xa_wv, xa_wo, xa_norm_post, ffn2_norm_pre, ffn2_w_in, ffn2_w_out, ffn2_norm_post):
    R = dict(ffn1_norm_pre=ffn1_norm_pre, ffn1_w_in=ffn1_w_in, ffn1_w_out=ffn1_w_out, ffn1_norm_post=ffn1_norm_post,
             mix_norm_pre=mix_norm_pre, w_in=w_in, mlstm_bi=mlstm_bi, mlstm_bf=mlstm_bf, mlstm_norm=mlstm_norm,
             rg_conv_w=rg_conv_w, rg_conv_b=rg_conv_b, rg_wa=rg_wa, rg_ba=rg_ba, rg_wx=rg_wx, rg_bx=rg_bx,
             rg_lambda=rg_lambda, hg_lb_logits=hg_lb_logits, hg_norm=hg_norm,
             w_br_a=w_br_a, w_br_b=w_br_b, w_br_c=w_br_c, w_out=w_out, mix_norm_post=mix_norm_post,
             xa_norm_pre=xa_norm_pre, xa_mem_norm=xa_mem_norm, xa_wq=xa_wq, xa_wk=xa_wk, xa_wv=xa_wv,
             xa_wo=xa_wo, xa_norm_post=xa_norm_post,
             ffn2_norm_pre=ffn2_norm_pre, ffn2_w_in=ffn2_w_in, ffn2_w_out=ffn2_w_out, ffn2_norm_post=ffn2_norm_post)
    P = _prep_params(R)
    bf16_weights = {}
    B = x_prompt.shape[0]
    zeros_like_state = lambda s: jnp.zeros((1, B) + s.shape[2:], F32)
    init = tuple(zeros_like_state(s) for s in (state_mlstm_C, state_mlstm_n, state_mlstm_m, state_rglru_h,
                                               state_rglru_conv, state_hgrn_S))
    y_prompt, p_states, p_mem_k, p_mem_v = _run_trunk(x_prompt, init, False, mem_prompt, None, None, P, bf16_weights)
    s_init = (state_mlstm_C, state_mlstm_n, state_mlstm_m, state_rglru_h, state_rglru_conv, state_hgrn_S)
    y_sample, s_states, _, _ = _run_trunk(x_sample, s_init, True, None, cache_mem_k, cache_mem_v, P, bf16_weights)
    return (y_prompt, y_sample) + p_states + (p_mem_k, p_mem_v) + s_states
```

```python
import functools

import jax
import jax.numpy as jnp
import numpy as np
from jax import lax
from jax.experimental import pallas as pl
from jax.experimental.pallas import tpu as pltpu

F32 = jnp.float32
BF16 = jnp.bfloat16
EPS = 1e-6
LOG2_E = 1.4426950408889634

V7X_VMEM_LIMIT_BYTES = 56 * 1024 * 1024
LANES = 128

XA_HEADS = 4
MLSTM_HEADS = 4
HG_HEADS = 4
RG_BLOCKS = 8
RG_C = 8.0
CONV_W = 4
D_FF_TILE = 512
FFN_ROWS = 512
MIX_TILE = 1664
XA_SHORT_ROWS = 512
MLSTM_CHUNK = 256
RG_CHUNK = 256
HG_CHUNK = 256
GATE_PAD = LANES


def _params(*sem):
    return pltpu.CompilerParams(dimension_semantics=sem, vmem_limit_bytes=V7X_VMEM_LIMIT_BYTES)


def _rms(x, w):
    return x * lax.rsqrt(jnp.mean(x * x, axis=-1, keepdims=True) + EPS) * w


def _sigmoid(x):
    return 0.5 * jnp.tanh(0.5 * x) + 0.5


def _log_sigmoid(x):
    return jnp.minimum(x, 0.0) - jnp.log1p(jnp.exp(-jnp.abs(x)))


def _softplus(x):
    return jnp.maximum(x, 0.0) + jnp.log1p(jnp.exp(-jnp.abs(x)))


def _dot(a, b):
    return jnp.dot(a, b, preferred_element_type=F32)


def _dot_nt(a, b):
    return lax.dot_general(a, b, (((1,), (1,)), ((), ())), preferred_element_type=F32)


def _dot_tn(a, b):
    return lax.dot_general(a, b, (((0,), (0,)), ((), ())), preferred_element_type=F32)


def _shift_rows(x, d, fill, ridx):
    return jnp.where(ridx >= d, pltpu.roll(x, d, 0), fill)


def _ffn_step(j, nj, x_ref, npre_ref, weights, npost_ref, o_ref, hn_ref, acc_ref, last_cols=None):
    @pl.when(j == 0)
    def _():
        hn_ref[...] = _rms(x_ref[...], npre_ref[...]).astype(BF16)
        acc_ref[...] = jnp.zeros_like(acc_ref)

    def tile(cols):
        wg, wu, wo = weights(cols)
        hn = hn_ref[...]
        g = _dot(hn, wg)
        u = _dot(hn, wu)
        a = (g * _sigmoid(g) * u).astype(BF16)
        acc_ref[...] += _dot(a, wo)

    if last_cols is None:
        tile(None)
    else:
        pl.when(j < nj - 1)(functools.partial(tile, None))
        pl.when(j == nj - 1)(functools.partial(tile, last_cols))

    @pl.when(j == nj - 1)
    def _():
        o_ref[...] = x_ref[...] + 0.5 * _rms(acc_ref[...], npost_ref[...])


def _ffn_kernel(x_ref, npre_ref, wg_ref, wu_ref, wo_ref, npost_ref, *rest, has_first, last_cols):
    o_ref, hn_ref, acc_ref = rest[-3:]
    i, j, nj = pl.program_id(0), pl.program_id(1), pl.num_programs(1)

    def compute():
        _ffn_step(j, nj, x_ref, npre_ref, lambda c: (wg_ref[:, :c], wu_ref[:, :c], wo_ref[:c, :]), npost_ref,
                  o_ref, hn_ref, acc_ref, last_cols)

    if not has_first:
        compute()
        return
    pl.when(i > 0)(compute)

    @pl.when((i == 0) & (j == nj - 1))
    def _():
        o_ref[...] = rest[0][...]


def _ffn(x, npre, wg, wu, wo, npost, l, tm, hidden, first=None):
    M, D = x.shape
    Fp = wo.shape[0]
    tn = D_FF_TILE
    nj = Fp // tn
    last_cols = hidden - (nj - 1) * tn
    last_cols = None if last_cols == tn else last_cols
    has_first = first is not None
    tile = (lambda i, j: jnp.where(i == 0, 0, j)) if has_first else (lambda i, j: j)
    args = [x, npre, wg, wu, wo, npost]
    in_specs = [
        pl.BlockSpec((tm, D), lambda i, j: (jnp.maximum(i, int(has_first)), 0)),
        pl.BlockSpec((None, 1, D), lambda i, j: (l, 0, 0)),
        pl.BlockSpec((None, D, tn), lambda i, j: (tile(i, j), 0, 0)),
        pl.BlockSpec((None, D, tn), lambda i, j: (tile(i, j), 0, 0)),
        pl.BlockSpec((tn, D), lambda i, j: (tile(i, j), 0)),
        pl.BlockSpec((None, 1, D), lambda i, j: (l, 0, 0)),
    ]
    if has_first:
        args.append(first)
        in_specs.append(pl.BlockSpec((tm, D), lambda i, j: (0, 0)))
    return pl.pallas_call(
        functools.partial(_ffn_kernel, has_first=has_first, last_cols=last_cols),
        grid=(M // tm, nj),
        in_specs=in_specs,
        out_specs=pl.BlockSpec((tm, D), lambda i, j: (i, 0)),
        out_shape=jax.ShapeDtypeStruct((M, D), F32),
        scratch_shapes=[pltpu.VMEM((tm, D), BF16), pltpu.VMEM((tm, D), F32)],
        compiler_params=_params("parallel", "arbitrary"),
        name="ffn",
    )(*args)


def _ffn_first_kernel(x_ref, npre_ref, g32_ref, u0_ref, u32_ref, wo32_ref, npost_ref,
                      o_ref, wg_ref, wu_ref, wo_ref, hn_ref, acc_ref, uprev_ref, *, F, shift):
    j = pl.program_id(0)
    tn = wg_ref.shape[-1]

    def weights(_):
        col_ok = lax.broadcasted_iota(jnp.int32, (1, tn), 1) + j * tn < F
        row_ok = lax.broadcasted_iota(jnp.int32, (tn, 1), 0) + j * tn < F
        if shift:
            @pl.when(j == 0)
            def _():
                uprev_ref[...] = u0_ref[...]

            u32 = jnp.concatenate([uprev_ref[:, shift:], u32_ref[:, :shift]], axis=1)
            uprev_ref[...] = u32_ref[...]
        else:
            u32 = u32_ref[...]
        wg = jnp.where(col_ok, g32_ref[...], 0.0).astype(BF16)
        wu = jnp.where(col_ok, u32, 0.0).astype(BF16)
        wo = jnp.where(row_ok, wo32_ref[...], 0.0).astype(BF16)
        wg_ref[...] = wg
        wu_ref[...] = wu
        wo_ref[...] = wo
        return wg, wu, wo

    _ffn_step(j, pl.num_programs(0), x_ref, npre_ref, weights, npost_ref, o_ref, hn_ref, acc_ref)


def _ffn_first(x, npre, w_in, w_out, npost, l, tm, tn=256):
    D = x.shape[1]
    F = w_out.shape[1]
    Fp = -(-F // D_FF_TILE) * D_FF_TILE
    nj = Fp // tn
    per = D_FF_TILE // tn
    q, shift = divmod(F, tn)
    last_in = -(-2 * F // tn) - 1
    return pl.pallas_call(
        functools.partial(_ffn_first_kernel, F=F, shift=shift),
        grid=(nj,),
        in_specs=[
            pl.BlockSpec((tm, D), lambda j: (0, 0)),
            pl.BlockSpec((None, 1, D), lambda j: (l, 0, 0)),
            pl.BlockSpec((None, D, tn), lambda j: (l, 0, j)),
            pl.BlockSpec((None, D, tn), lambda j: (l, 0, q)),
            pl.BlockSpec((None, D, tn), lambda j: (l, 0, jnp.minimum(q + j + (1 if shift else 0), last_in))),
            pl.BlockSpec((None, tn, D), lambda j: (l, j, 0)),
            pl.BlockSpec((None, 1, D), lambda j: (l, 0, 0)),
        ],
        out_specs=[
            pl.BlockSpec((tm, D), lambda j: (0, 0)),
            pl.BlockSpec((None, D, tn), lambda j: (j // per, 0, j % per)),
            pl.BlockSpec((None, D, tn), lambda j: (j // per, 0, j % per)),
            pl.BlockSpec((tn, D), lambda j: (j, 0)),
        ],
        out_shape=[
            jax.ShapeDtypeStruct((tm, D), F32),
            jax.ShapeDtypeStruct((Fp // D_FF_TILE, D, D_FF_TILE), BF16),
            jax.ShapeDtypeStruct((Fp // D_FF_TILE, D, D_FF_TILE), BF16),
            jax.ShapeDtypeStruct((Fp, D), BF16),
        ],
        scratch_shapes=[pltpu.VMEM((tm, D), BF16), pltpu.VMEM((tm, D), F32), pltpu.VMEM((D, tn), F32)],
        compiler_params=_params("arbitrary"),
        name="ffn_first",
    )(x, npre, w_in, w_in, w_in, w_out, npost)


def _mix_in_kernel(x_ref, nw_ref, w_ref, wg_ref, *rest, has_first):
    z_ref, g_ref, hn_ref = rest[-3:]
    i, j = pl.program_id(0), pl.program_id(1)

    def compute():
        @pl.when(j == 0)
        def _():
            hn = _rms(x_ref[...], nw_ref[...]).astype(BF16)
            hn_ref[...] = hn
            g_ref[...] = _dot_nt(hn, wg_ref[...])

        z_ref[...] = _dot_nt(hn_ref[...], w_ref[...]).astype(z_ref.dtype)

    if not has_first:
        compute()
        return
    pl.when(i > 0)(compute)

    @pl.when(i == 0)
    def _():
        z_ref[...] = rest[0][...]

    @pl.when((i == 0) & (j == 0))
    def _():
        g_ref[...] = rest[1][...]


def _mix_in(x, nw, w, wg, l, tm, tn, first=None):
    M, D = x.shape
    N, G = w.shape[0], wg.shape[0]
    has_first = first is not None
    tile = (lambda i, j: jnp.where(i == 0, 0, j)) if has_first else (lambda i, j: j)
    args = [x, nw, w, wg]
    in_specs = [
        pl.BlockSpec((tm, D), lambda i, j: (jnp.maximum(i, int(has_first)), 0)),
        pl.BlockSpec((None, 1, D), lambda i, j: (l, 0, 0)),
        pl.BlockSpec((tn, D), lambda i, j: (tile(i, j), 0)),
        pl.BlockSpec((G, D), lambda i, j: (0, 0)),
    ]
    if has_first:
        args += list(first)
        in_specs += [pl.BlockSpec((tm, tn), lambda i, j: (0, jnp.where(i == 0, j, 0))),
                     pl.BlockSpec((tm, G), lambda i, j: (0, 0))]
    return pl.pallas_call(
        functools.partial(_mix_in_kernel, has_first=has_first),
        grid=(M // tm, N // tn),
        in_specs=in_specs,
        out_specs=[pl.BlockSpec((tm, tn), lambda i, j: (i, j)), pl.BlockSpec((tm, G), lambda i, j: (i, 0))],
        out_shape=[jax.ShapeDtypeStruct((M, N), BF16), jax.ShapeDtypeStruct((M, G), F32)],
        scratch_shapes=[pltpu.VMEM((tm, D), BF16)],
        compiler_params=_params("parallel", "arbitrary"),
        name="mix_in",
    )(*args)


def _mix_in_first_kernel(x_ref, nw_ref, a_ref, b_ref, g32_ref, z_ref, gout_ref, w_ref, wg_ref, hn_ref,
                         *, r, aligned, ngate):
    j = pl.program_id(0)

    @pl.when(j == 0)
    def _():
        hn = _rms(x_ref[...], nw_ref[...]).astype(BF16)
        hn_ref[...] = hn
        rowi = lax.broadcasted_iota(jnp.int32, g32_ref.shape, 0)
        wg = jnp.where(rowi < ngate, g32_ref[...], 0.0).astype(BF16)
        wg_ref[...] = wg
        gout_ref[...] = _dot_nt(hn, wg)

    a = a_ref[...]
    w32 = a
    if r:
        is_aligned = functools.reduce(jnp.logical_or, [(j >= lo) & (j < hi) for lo, hi in aligned], j < 0)
        w32 = jnp.where(is_aligned, a, jnp.concatenate([a[r:], b_ref[...]], axis=0))
    w = w32.astype(BF16)
    w_ref[...] = w
    z_ref[...] = _dot_nt(hn_ref[...], w).astype(z_ref.dtype)


def _mix_in_first(x, nw, wt, segs, gate0, ngate, l, tm, tn=512):
    D = x.shape[1]
    N = sum(s[1] for s in segs)
    r = max(s0 % tn for s0, _, _ in segs)
    assert all(s0 % tn in (0, r) and cnt % tn == 0 and d0 % tn == 0 for s0, cnt, d0 in segs)
    assert r % 8 == 0 and gate0 % GATE_PAD == 0
    tiles = [(d0 // tn, (d0 + cnt) // tn, s0 // tn) for s0, cnt, d0 in segs]
    aligned = tuple((t0, t1) for (t0, t1, _), (s0, _, _) in zip(tiles, segs) if s0 % tn == 0)

    def src_block(j):
        blk = 0
        for t0, t1, a0 in tiles:
            blk = jnp.where((j >= t0) & (j < t1), a0 + j - t0, blk)
        return blk

    rb = max(r, 8)
    return pl.pallas_call(
        functools.partial(_mix_in_first_kernel, r=r, aligned=aligned, ngate=ngate),
        grid=(N // tn,),
        in_specs=[
            pl.BlockSpec((tm, D), lambda j: (0, 0)),
            pl.BlockSpec((None, 1, D), lambda j: (l, 0, 0)),
            pl.BlockSpec((None, tn, D), lambda j: (l, src_block(j), 0)),
            pl.BlockSpec((None, rb, D), lambda j: (l, (src_block(j) + 1) * (tn // rb), 0)),
            pl.BlockSpec((None, GATE_PAD, D), lambda j: (l, gate0 // GATE_PAD, 0)),
        ],
        out_specs=[
            pl.BlockSpec((tm, tn), lambda j: (0, j)),
            pl.BlockSpec((tm, GATE_PAD), lambda j: (0, 0)),
            pl.BlockSpec((tn, D), lambda j: (j, 0)),
            pl.BlockSpec((GATE_PAD, D), lambda j: (0, 0)),
        ],
        out_shape=[
            jax.ShapeDtypeStruct((tm, N), BF16),
            jax.ShapeDtypeStruct((tm, GATE_PAD), F32),
            jax.ShapeDtypeStruct((N, D), BF16),
            jax.ShapeDtypeStruct((GATE_PAD, D), BF16),
        ],
        scratch_shapes=[pltpu.VMEM((tm, D), BF16)],
        compiler_params=_params("arbitrary"),
        name="mix_in_first",
    )(x, nw, wt, wt, wt)


def _mlstm_kernel(q_ref, k_ref, v_ref, og_ref, g_ref, gb_ref, nw_ref, C0_ref, n0_ref, m0_ref,
                  hm_ref, C_ref, n_ref, m_ref, *, L, dh):
    H = MLSTM_HEADS

    @pl.when(pl.program_id(1) == 0)
    def _():
        C_ref[...] = C0_ref[...]
        n_ref[...] = n0_ref[...]
        m_ref[...] = m0_ref[...]

    row = lax.broadcasted_iota(jnp.int32, (L, L), 0)
    col = lax.broadcasted_iota(jnp.int32, (L, L), 1)
    causal = row >= col
    eye = row == col

    def as_row(x_col):
        return jnp.sum(jnp.where(eye, x_col, 0.0), axis=0, keepdims=True)

    for bb in range(q_ref.shape[0]):
        ga = g_ref[bb] + gb_ref[...]
        m_prev = m_ref[bb]
        gaT = ga.T if L % LANES == 0 else None
        for h in range(H):
            hs = slice(h * dh, (h + 1) * dh)
            ig_c = ga[:, h:h + 1]
            lf_c = _log_sigmoid(ga[:, H + h:H + h + 1])
            if gaT is not None:
                ig_r = gaT[h:h + 1, :]
                lf_r = _log_sigmoid(gaT[H + h:H + h + 1, :])
            else:
                ig_r = as_row(ig_c)
                lf_r = as_row(lf_c)
            b_c = jnp.sum(jnp.where(causal, lf_r, 0.0), axis=1, keepdims=True)
            b_r = jnp.sum(jnp.where(row <= col, lf_c, 0.0), axis=0, keepdims=True)
            m0 = m_prev[:, h:h + 1]
            src_r = ig_r - b_r
            peak_c = jnp.maximum(m0, jnp.max(jnp.where(causal, src_r, -jnp.inf), axis=1, keepdims=True))
            m_c = b_c + peak_c
            w = jnp.exp(jnp.where(causal, src_r - peak_c, -jnp.inf))
            w_inter = jnp.exp(m0 - peak_c)

            q = q_ref[bb, :, hs]
            k = k_ref[bb, :, hs] * (dh ** -0.5)
            v = v_ref[bb, :, hs]
            C0 = C_ref[bb, h]
            n0 = n_ref[bb, h:h + 1, :]
            wqk = w * _dot_nt(q, k)
            num = _dot(wqk.astype(BF16), v) + w_inter * _dot(q, C0.astype(BF16))
            den = (jnp.sum(wqk, axis=1, keepdims=True)
                   + w_inter * jnp.sum(q.astype(F32) * n0, axis=1, keepdims=True))
            hh = num / jnp.maximum(jnp.abs(den), jnp.exp(-m_c))
            hn = hh * lax.rsqrt(jnp.mean(hh * hh, axis=-1, keepdims=True) + EPS) * nw_ref[:, hs]
            hm_ref[bb, :, hs] = (_sigmoid(og_ref[bb, :, hs].astype(F32)) * hn).astype(BF16)

            m_end = m_c[L - 1:L, :]
            w_end = jnp.exp(b_c[L - 1:L, :] - b_c + ig_c - m_end)
            s_end = w_inter[L - 1:L, :]
            kw = k.astype(F32) * w_end
            C_ref[bb, h] = s_end * C0 + _dot_tn(kw.astype(BF16), v)
            n_ref[bb, h:h + 1, :] = s_end * n0 + jnp.sum(kw, axis=0, keepdims=True)
            m_ref[bb, :, h:h + 1] = m_end


def _mlstm(z, gates, gate_bias, norm_w, C0, n0, m0, l, ls, B, T, col0):
    H = MLSTM_HEADS
    dh = C0.shape[-1]
    W = H * dh
    L = min(MLSTM_CHUNK, T)
    nC = T // L
    M = B * T
    nb = 1
    z3 = z.reshape(B, T, z.shape[-1])
    g3 = gates.reshape(B, T, GATE_PAD)
    zspec = lambda cb: pl.BlockSpec((nb, L, W), lambda b, c: (b, c, cb))
    hm, C1, n1, m1 = pl.pallas_call(
        functools.partial(_mlstm_kernel, L=L, dh=dh),
        grid=(B // nb, nC),
        in_specs=[
            zspec(col0), zspec(col0 + 1), zspec(col0 + 2), zspec(col0 + 3),
            pl.BlockSpec((nb, L, GATE_PAD), lambda b, c: (b, c, 0)),
            pl.BlockSpec((None, 1, GATE_PAD), lambda b, c: (l, 0, 0)),
            pl.BlockSpec((None, 1, W), lambda b, c: (l, 0, 0)),
            pl.BlockSpec((None, nb, H, dh, dh), lambda b, c: (ls, b, 0, 0, 0)),
            pl.BlockSpec((None, nb, H, dh), lambda b, c: (ls, b, 0, 0)),
            pl.BlockSpec((None, nb, 1, H), lambda b, c: (ls, b, 0, 0)),
        ],
        out_specs=[
            pl.BlockSpec((nb, L, W), lambda b, c: (b, c, 0)),
            pl.BlockSpec((nb, H, dh, dh), lambda b, c: (b, 0, 0, 0)),
            pl.BlockSpec((nb, H, dh), lambda b, c: (b, 0, 0)),
            pl.BlockSpec((nb, 1, H), lambda b, c: (b, 0, 0)),
        ],
        out_shape=[
            jax.ShapeDtypeStruct((B, T, W), BF16),
            jax.ShapeDtypeStruct((B, H, dh, dh), F32),
            jax.ShapeDtypeStruct((B, H, dh), F32),
            jax.ShapeDtypeStruct((B, 1, H), F32),
        ],
        compiler_params=_params("parallel", "arbitrary"),
        name="mlstm",
    )(z3, z3, z3, z3, g3, gate_bias, norm_w, C0, n0, m0)
    return hm.reshape(M, W), C1, n1, m1


def _rglru_kernel(rx_ref, rg_ref, cw_ref, cbias_ref, wa_ref, ba_ref, wx_ref, bx_ref, lam_ref, buf0_ref, h0_ref,
                  y_ref, buf_ref, h_ref, cbuf_ref, *, L):
    c = pl.program_id(1)
    W = rx_ref.shape[-1]
    TAIL = CONV_W - 1

    @pl.when(c == 0)
    def _():
        cbuf_ref[0:8, :] = jnp.zeros((8, W), F32)
        cbuf_ref[8 - TAIL:8, :] = buf0_ref[0]
        h_ref[...] = h0_ref[...]

    cbuf_ref[8:8 + L, :] = rx_ref[...].astype(F32)
    xc = cbias_ref[...]
    for j in range(CONV_W):
        xc = xc + cbuf_ref[8 - TAIL + j:8 - TAIL + j + L, :] * cw_ref[j:j + 1, :]

    xcb = xc.astype(BF16)
    r = _sigmoid(_dot(xcb, wa_ref[...]) + ba_ref[...])
    i = _sigmoid(_dot(xcb, wx_ref[...]) + bx_ref[...])
    log_a = (-RG_C * _softplus(-lam_ref[...])) * r
    a = jnp.exp(log_a)
    th = jnp.tanh(log_a)
    u = jnp.sqrt(-2.0 * th / (1.0 - th)) * (i * xc)

    ridx = lax.broadcasted_iota(jnp.int32, (L, W), 0)
    d = 1
    while d < L:
        a_sh = _shift_rows(a, d, 1.0, ridx)
        u_sh = _shift_rows(u, d, 0.0, ridx)
        u = a * u_sh + u
        a = a * a_sh
        d *= 2
    h = a * h_ref[0] + u
    h_ref[0] = h[L - 1:L, :]
    y_ref[...] = (h * jax.nn.gelu(rg_ref[...].astype(F32))).astype(BF16)

    cbuf_ref[0:8, :] = cbuf_ref[L:L + 8, :]

    @pl.when(c == pl.num_programs(1) - 1)
    def _():
        buf_ref[0] = cbuf_ref[8 + L - TAIL:8 + L, :]


def _rglru(z, P, buf0, h0, l, ls, B, T, col0):
    W = h0.shape[-1]
    L = min(RG_CHUNK, T)
    nC = T // L
    M = B * T
    vec = lambda: pl.BlockSpec((None, 1, W), lambda b, c: (l, 0, 0))
    mat = lambda: pl.BlockSpec((None, W, W), lambda b, c: (l, 0, 0))
    return pl.pallas_call(
        functools.partial(_rglru_kernel, L=L),
        grid=(B, nC),
        in_specs=[
            pl.BlockSpec((L, W), lambda b, c: (b * nC + c, col0)),
            pl.BlockSpec((L, W), lambda b, c: (b * nC + c, col0 + 1)),
            pl.BlockSpec((None, CONV_W, W), lambda b, c: (l, 0, 0)),
            vec(), mat(), vec(), mat(), vec(), vec(),
            pl.BlockSpec((None, 1, CONV_W - 1, W), lambda b, c: (ls, b, 0, 0)),
            pl.BlockSpec((None, 1, 1, W), lambda b, c: (ls, b, 0, 0)),
        ],
        out_specs=[
            pl.BlockSpec((L, W), lambda b, c: (b * nC + c, 0)),
            pl.BlockSpec((1, CONV_W - 1, W), lambda b, c: (b, 0, 0)),
            pl.BlockSpec((1, 1, W), lambda b, c: (b, 0, 0)),
        ],
        out_shape=[
            jax.ShapeDtypeStruct((M, W), BF16),
            jax.ShapeDtypeStruct((B, CONV_W - 1, W), F32),
            jax.ShapeDtypeStruct((B, 1, W), F32),
        ],
        scratch_shapes=[pltpu.VMEM((L + 8, W), F32)],
        compiler_params=_params("parallel", "arbitrary"),
        name="rglru",
    )(z, z, P['rg_conv_w'], P['rg_conv_b'], P['rg_wa'], P['rg_ba'], P['rg_wx'], P['rg_bx'], P['rg_lambda'],
      buf0, h0)


def _block_ref_rows(cb, m, ridx):
    L, W = cb.shape
    n = 2 * m
    if m == 1:
        return jnp.where((ridx & 1) == 0, cb, pltpu.roll(cb, 1, 0))
    if m == 2:
        off = ridx & 3
        return jnp.where(off == 0, pltpu.roll(cb, L - 1, 0),
                         jnp.where(off == 1, cb, jnp.where(off == 2, pltpu.roll(cb, 1, 0), pltpu.roll(cb, 2, 0))))
    return jnp.concatenate(
        [jnp.broadcast_to(cb[j * n + m - 1:j * n + m, :], (n, W)) for j in range(L // n)], axis=0)


def _hgrn_kernel(q_ref, f_ref, i_ref, g_ref, lbl_ref, nw_ref, S0_ref, y_ref, S_ref, *, L, layer):
    H = HG_HEADS
    W = q_ref.shape[-1]
    dk = W // H

    @pl.when(pl.program_id(1) == 0)
    def _():
        S_ref[...] = S0_ref[...]

    lg = lbl_ref[...]
    e = jnp.exp(lg - jnp.max(lg, axis=0, keepdims=True))
    p = e / jnp.sum(e, axis=0, keepdims=True)
    lb = jnp.zeros((1, W), F32)
    for r in range(1, layer + 1):
        lb = lb + p[r:r + 1, :]

    f = lb + (1.0 - lb) * jax.nn.sigmoid(f_ref[...].astype(F32))
    cb = jnp.log(f)
    ridx = lax.broadcasted_iota(jnp.int32, (L, W), 0)
    d = 1
    while d < L:
        cb = cb + _shift_rows(cb, d, 0.0, ridx)
        d *= 2
    kk = 1.0 - f
    q = q_ref[...].astype(F32)
    v = i_ref[...].astype(F32)
    vb = v.astype(BF16)

    row = lax.broadcasted_iota(jnp.int32, (L, L), 0)
    col = lax.broadcasted_iota(jnp.int32, (L, L), 1)
    split = jnp.where(row > col, row ^ col, 0)
    a = [jnp.zeros((L, L), F32) for _ in range(H)]
    m = 1
    cb2 = cb * LOG2_E
    while m < L:
        dref = cb2 - _block_ref_rows(cb2, m, ridx)
        dec = jnp.exp2(-jnp.abs(dref))
        qt = (q * dec).astype(BF16)
        ks = (kk * dec).astype(BF16)
        level = (split // m) == 1
        for h in range(H):
            hs = slice(h * dk, (h + 1) * dk)
            a[h] = jnp.where(level, _dot_nt(qt[:, hs], ks[:, hs]), a[h])
        m *= 2

    cb_end = cb[L - 1:L, :]
    qe = (q * jnp.exp(cb)).astype(BF16)
    ke = (kk * jnp.exp(cb_end - cb)).astype(BF16)
    qk = q * kk
    eye = (lax.broadcasted_iota(jnp.int32, (dk, dk), 0) == lax.broadcasted_iota(jnp.int32, (dk, dk), 1))
    outs = []
    for h in range(H):
        hs = slice(h * dk, (h + 1) * dk)
        S = S_ref[0, h]
        o = (_dot(a[h].astype(BF16), vb[:, hs]) + jnp.sum(qk[:, hs], axis=1, keepdims=True) * v[:, hs]
             + _dot(qe[:, hs], S.astype(BF16)))
        dec_col = jnp.sum(jnp.where(eye, jnp.exp(cb_end[:, hs]), 0.0), axis=1, keepdims=True)
        S_ref[0, h] = dec_col * S + _dot_tn(ke[:, hs], vb[:, hs])
        outs.append(o * lax.rsqrt(jnp.mean(o * o, axis=-1, keepdims=True) + EPS))
    gate = g_ref[...].astype(F32)
    y_ref[...] = (jnp.concatenate(outs, axis=1) * nw_ref[...] * (gate * _sigmoid(gate))).astype(BF16)


def _hgrn(z, lb_logits, norm_w, S0, l, ls, B, T, col0):
    H, dk, dv = S0.shape[-3:]
    W = H * dk
    L = min(HG_CHUNK, T)
    nC = T // L
    M = B * T
    depth = lb_logits.shape[0]
    zspec = lambda cb: pl.BlockSpec((L, W), lambda b, c: (b * nC + c, cb))
    return pl.pallas_call(
        functools.partial(_hgrn_kernel, L=L, layer=l),
        grid=(B, nC),
        in_specs=[
            zspec(col0), zspec(col0 + 1), zspec(col0 + 2), zspec(col0 + 3),
            pl.BlockSpec((depth, W), lambda b, c: (0, 0)),
            pl.BlockSpec((None, 1, W), lambda b, c: (l, 0, 0)),
            pl.BlockSpec((None, 1, H, dk, dv), lambda b, c: (ls, b, 0, 0, 0)),
        ],
        out_specs=[
            pl.BlockSpec((L, W), lambda b, c: (b * nC + c, 0)),
            pl.BlockSpec((1, H, dk, dv), lambda b, c: (b, 0, 0, 0)),
        ],
        out_shape=[
            jax.ShapeDtypeStruct((M, W), BF16),
            jax.ShapeDtypeStruct((B, H, dk, dv), F32),
        ],
        compiler_params=_params("parallel", "arbitrary"),
        name="hgrn",
    )(z, z, z, z, lb_logits, norm_w, S0)


def _merge_kernel(a_ref, b_ref, c_ref, ga_ref, gb_ref, gc_ref, wa_ref, wb_ref, wc_ref, wo_ref, npost_ref, x_ref,
                  o_ref):
    sig = lambda r: _sigmoid(r[...].astype(F32))
    merged = (sig(ga_ref) * _dot(a_ref[...], wa_ref[...])
              + sig(gb_ref) * _dot(b_ref[...], wb_ref[...])
              + sig(gc_ref) * _dot(c_ref[...], wc_ref[...]))
    out = _dot(merged.astype(BF16), wo_ref[...])
    o_ref[...] = x_ref[...] + _rms(out, npost_ref[...])


def _merge(x, ya, yb, yc, z, P, l, tm):
    M, D = x.shape
    rows = lambda w: pl.BlockSpec((tm, w), lambda i: (i, 0))
    wspec = lambda k: pl.BlockSpec((None, k, D), lambda i: (l, 0, 0), pipeline_mode=pl.Buffered(1))
    return pl.pallas_call(
        _merge_kernel,
        grid=(M // tm,),
        in_specs=[
            rows(ya.shape[1]), rows(yb.shape[1]), rows(yc.shape[1]),
            pl.BlockSpec((tm, D), lambda i: (i, 0)),
            pl.BlockSpec((tm, D), lambda i: (i, 1)),
            pl.BlockSpec((tm, D), lambda i: (i, 2)),
            wspec(ya.shape[1]), wspec(yb.shape[1]), wspec(yc.shape[1]), wspec(D),
            pl.BlockSpec((None, 1, D), lambda i: (l, 0, 0)),
            rows(D),
        ],
        out_specs=rows(D),
        out_shape=jax.ShapeDtypeStruct((M, D), F32),
        compiler_params=_params("parallel"),
        name="merge",
    )(ya, yb, yc, z, z, z, P['w_br_a'], P['w_br_b'], P['w_br_c'], P['w_out'], P['mix_norm_post'], x)


def _xattn_kernel(x_ref, npre_ref, wq_ref, k_ref, v_ref, wo_ref, npost_ref, o_ref, *kv_scratch):
    D = x_ref.shape[-1]
    dh = D // XA_HEADS
    if kv_scratch:
        kb_ref, vb_ref = kv_scratch

        @pl.when(pl.program_id(1) == 0)
        def _():
            for h in range(XA_HEADS):
                kb_ref[:, h * dh:(h + 1) * dh] = k_ref[:, h, :].astype(BF16)
                vb_ref[:, h * dh:(h + 1) * dh] = v_ref[:, h, :].astype(BF16)
    else:
        kb_ref, vb_ref = k_ref, v_ref

    x = x_ref[...]
    q = _dot(_rms(x, npre_ref[...]).astype(BF16), wq_ref[...]).astype(BF16)
    outs = []
    for h in range(XA_HEADS):
        hs = slice(h * dh, (h + 1) * dh)
        s = _dot_nt(q[:, hs], kb_ref[:, hs]) * (dh ** -0.5)
        e = jnp.exp(s - jnp.max(s, axis=-1, keepdims=True))
        p = e / jnp.sum(e, axis=-1, keepdims=True)
        outs.append(_dot(p.astype(BF16), vb_ref[:, hs]).astype(BF16))
    out = _dot(jnp.concatenate(outs, axis=1), wo_ref[...])
    o_ref[...] = x + _rms(out, npost_ref[...])


def _xattn(x, mem_k, mem_v, P, l, lk, B, T, tq):
    M, D = x.shape
    n_mem = mem_k.shape[2]
    nT = T // tq
    wspec = lambda: pl.BlockSpec((None, D, D), lambda b, t: (l, 0, 0), pipeline_mode=pl.Buffered(1))
    nspec = lambda: pl.BlockSpec((None, 1, D), lambda b, t: (l, 0, 0))
    kv_block = (None, None) + mem_k.shape[2:]
    kv_index = lambda b, t: (lk, b) + (0,) * (mem_k.ndim - 2)
    kvspec = lambda: pl.BlockSpec(kv_block, kv_index)
    scratch = [] if mem_k.dtype == BF16 else [pltpu.VMEM((n_mem, D), BF16), pltpu.VMEM((n_mem, D), BF16)]
    return pl.pallas_call(
        _xattn_kernel,
        grid=(B, nT),
        in_specs=[
            pl.BlockSpec((tq, D), lambda b, t: (b * nT + t, 0)),
            nspec(), wspec(), kvspec(), kvspec(), wspec(), nspec(),
        ],
        out_specs=pl.BlockSpec((tq, D), lambda b, t: (b * nT + t, 0)),
        out_shape=jax.ShapeDtypeStruct((M, D), F32),
        scratch_shapes=scratch,
        compiler_params=_params("parallel", "arbitrary"),
        name="xattn",
    )(x, P['xa_norm_pre'], P['xa_wq'], mem_k, mem_v, P['xa_wo'], P['xa_norm_post'])


def _xattn_short_kernel(x_ref, npre_ref, wq_ref, k_ref, v_ref, wo_ref, npost_ref, o_ref, q_ref, a_ref, *, T):
    b = pl.program_id(0)
    D = x_ref.shape[-1]
    dh = D // XA_HEADS

    @pl.when(b == 0)
    def _():
        q_ref[...] = _dot(_rms(x_ref[...], npre_ref[...]).astype(BF16), wq_ref[...]).astype(BF16)

    r0 = pl.multiple_of(b * T, T)
    q = q_ref[pl.ds(r0, T), :]
    outs = []
    for h in range(XA_HEADS):
        hs = slice(h * dh, (h + 1) * dh)
        kh = k_ref[:, hs] if k_ref.dtype == BF16 else k_ref[:, h, :].astype(BF16)
        vh = v_ref[:, hs] if v_ref.dtype == BF16 else v_ref[:, h, :].astype(BF16)
        s = _dot_nt(q[:, hs], kh) * (dh ** -0.5)
        e = jnp.exp(s - jnp.max(s, axis=-1, keepdims=True))
        p = e / jnp.sum(e, axis=-1, keepdims=True)
        outs.append(_dot(p.astype(BF16), vh).astype(BF16))
    a_ref[pl.ds(r0, T), :] = jnp.concatenate(outs, axis=1)

    @pl.when(b == pl.num_programs(0) - 1)
    def _():
        o_ref[...] = x_ref[...] + _rms(_dot(a_ref[...], wo_ref[...]), npost_ref[...])


def _xattn_short(x, mem_k, mem_v, P, l, lk, B, T):
    M, D = x.shape
    wspec = lambda: pl.BlockSpec((None, D, D), lambda b: (l, 0, 0), pipeline_mode=pl.Buffered(1))
    nspec = lambda: pl.BlockSpec((None, 1, D), lambda b: (l, 0, 0))
    kvspec = lambda: pl.BlockSpec((None, None) + mem_k.shape[2:], lambda b: (lk, b) + (0,) * (mem_k.ndim - 2))
    return pl.pallas_call(
        functools.partial(_xattn_short_kernel, T=T),
        grid=(B,),
        in_specs=[pl.BlockSpec((M, D), lambda b: (0, 0)), nspec(), wspec(), kvspec(), kvspec(), wspec(), nspec()],
        out_specs=pl.BlockSpec((M, D), lambda b: (0, 0)),
        out_shape=jax.ShapeDtypeStruct((M, D), F32),
        scratch_shapes=[pltpu.VMEM((M, D), BF16), pltpu.VMEM((M, D), BF16)],
        compiler_params=_params("arbitrary"),
        name="xattn_short",
    )(x, P['xa_norm_pre'], P['xa_wq'], mem_k, mem_v, P['xa_wo'], P['xa_norm_post'])


def _mem_kv_kernel(m_ref, nw_ref, wk_ref, wv_ref, k_ref, v_ref, kb_ref, vb_ref):
    H, dh = k_ref.shape[-2:]
    hn = _rms(m_ref[...], nw_ref[...]).astype(BF16)
    k = _dot(hn, wk_ref[...])
    v = _dot(hn, wv_ref[...])
    kb_ref[...] = k.astype(BF16)
    vb_ref[...] = v.astype(BF16)
    for h in range(H):
        k_ref[:, h, :] = k[:, h * dh:(h + 1) * dh]
        v_ref[:, h, :] = v[:, h * dh:(h + 1) * dh]


def _mem_kv(mem, P, H):
    B, n_mem, D = mem.shape
    depth = P['xa_wk'].shape[0]
    out = jax.ShapeDtypeStruct((depth, B, n_mem, H, D // H), F32)
    outb = jax.ShapeDtypeStruct((depth, B, n_mem, D), BF16)
    ospec = lambda: pl.BlockSpec((None, None, n_mem, H, D // H), lambda l, b: (l, b, 0, 0, 0))
    obspec = lambda: pl.BlockSpec((None, None, n_mem, D), lambda l, b: (l, b, 0, 0))
    wspec = lambda: pl.BlockSpec((None, D, D), lambda l, b: (l, 0, 0))
    return pl.pallas_call(
        _mem_kv_kernel,
        grid=(depth, B),
        in_specs=[
            pl.BlockSpec((None, n_mem, D), lambda l, b: (b, 0, 0)),
            pl.BlockSpec((None, 1, D), lambda l, b: (l, 0, 0)),
            wspec(), wspec(),
        ],
        out_specs=[ospec(), ospec(), obspec(), obspec()],
        out_shape=[out, out, outb, outb],
        compiler_params=_params("parallel", "arbitrary"),
        name="mem_kv",
    )(mem, P['xa_mem_norm'], P['xa_wk'], P['xa_wv'])


def _prep_params(R):
    depth, D = R['ffn1_norm_pre'].shape
    P = {}
    row = lambda a: a.astype(F32).reshape(depth, 1, a.shape[-1])
    for name in ('ffn1_norm_pre', 'ffn1_norm_post', 'mix_norm_pre', 'mix_norm_post', 'xa_norm_pre', 'xa_mem_norm',
                 'xa_norm_post', 'ffn2_norm_pre', 'ffn2_norm_post', 'mlstm_norm', 'rg_conv_b', 'rg_ba', 'rg_bx',
                 'rg_lambda', 'hg_norm'):
        P[name] = row(R[name])
    P['rg_conv_w'] = R['rg_conv_w'].astype(F32)
    P['hg_lb_logits'] = R['hg_lb_logits'].astype(F32)

    for name in ('ffn1_w_in', 'ffn1_w_out', 'ffn2_w_in', 'ffn2_w_out'):
        P[name] = R[name]

    mw = R['w_br_a'].shape[1]
    rw = R['w_br_b'].shape[1]
    hw = R['w_br_c'].shape[1]
    nh = R['mlstm_bi'].shape[1]
    splits = (mw, mw, mw, mw, nh, nh, rw, rw, hw, hw, hw, hw, D, D, D)
    pts = [0] + [int(v) for v in np.cumsum(splits)]
    runs = ((12, 15), (0, 4), (6, 12))
    segs, d0 = [], 0
    for a, b in runs:
        segs.append((pts[a], pts[b] - pts[a], d0))
        d0 += pts[b] - pts[a]
    P['w_in_t'] = jnp.swapaxes(R['w_in'], 1, 2)
    P['w_in_regroup'] = (tuple(segs), pts[4], 2 * nh)
    P['gate_bias'] = jnp.pad(jnp.concatenate([R['mlstm_bi'], R['mlstm_bf']], axis=-1).astype(F32),
                             ((0, 0), (0, GATE_PAD - 2 * nh))).reshape(depth, 1, GATE_PAD)

    eye = jnp.eye(RG_BLOCKS, dtype=F32)
    dense = lambda w: jnp.einsum('lnde,nm->lndme', w, eye).reshape(depth, rw, rw).astype(BF16)
    P['rg_wa'] = dense(R['rg_wa'])
    P['rg_wx'] = dense(R['rg_wx'])

    for name in ('w_br_a', 'w_br_b', 'w_br_c', 'w_out', 'xa_wq', 'xa_wk', 'xa_wv', 'xa_wo'):
        P[name] = R[name].astype(BF16)
    return P


def _ffn_sublayer(x, P, name, l, bf16_weights):
    npre, npost = P[name + '_norm_pre'], P[name + '_norm_post']
    M = x.shape[0]
    tm = min(FFN_ROWS, M)
    hidden = P[name + '_w_out'].shape[1]
    if (name, l) in bf16_weights:
        return _ffn(x, npre, *bf16_weights[(name, l)], npost, l, tm, hidden)
    y, wg, wu, wo = _ffn_first(x, npre, P[name + '_w_in'], P[name + '_w_out'], npost, l, tm)
    bf16_weights[(name, l)] = (wg, wu, wo)
    return y if M == tm else _ffn(x, npre, wg, wu, wo, npost, l, tm, hidden, first=y)


def _run_trunk(x3, states, state_has_layers, mem, mem_k, mem_v, P, bf16_weights):
    B, T, D = x3.shape
    M = B * T
    depth = P['ffn1_norm_pre'].shape[0]
    x = x3.reshape(M, D)
    tm_proj = min(1024, M)
    tm_merge = min(512, M)
    tq = min(512, T)
    C0, n0, m0, hr0, buf0, S0 = states
    m0 = m0.reshape(m0.shape[0], m0.shape[1], 1, m0.shape[2])
    hr0 = hr0.reshape(hr0.shape[0], hr0.shape[1], 1, hr0.shape[2])
    mw = P['w_br_a'].shape[1]
    rw = P['w_br_b'].shape[1]
    hw = P['w_br_c'].shape[1]
    col_m = 3 * D // mw
    col_r = (3 * D + 4 * mw) // rw
    col_h = (3 * D + 4 * mw + 2 * rw) // hw

    if mem is not None:
        mem_k, mem_v, att_k, att_v = _mem_kv(mem, P, XA_HEADS)
    else:
        att_k, att_v = mem_k, mem_v

    outs = [[] for _ in range(6)]
    for l in range(depth):
        ls = l if state_has_layers else 0
        x = _ffn_sublayer(x, P, 'ffn1', l, bf16_weights)
        if ('mix', l) in bf16_weights:
            z, gates = _mix_in(x, P['mix_norm_pre'], *bf16_weights[('mix', l)], l, tm_proj, MIX_TILE)
        else:
            z, gates, w_mix, w_gate = _mix_in_first(x, P['mix_norm_pre'], P['w_in_t'], *P['w_in_regroup'], l, tm_proj)
            bf16_weights[('mix', l)] = (w_mix, w_gate)
            if M > tm_proj:
                z, gates = _mix_in(x, P['mix_norm_pre'], w_mix, w_gate, l, tm_proj, MIX_TILE, first=(z, gates))
        ya, C1, n1, m1 = _mlstm(z, gates, P['gate_bias'], P['mlstm_norm'], C0, n0, m0, l, ls, B, T, col_m)
        yb, buf1, hr1 = _rglru(z, P, buf0, hr0, l, ls, B, T, col_r)
        yc, S1 = _hgrn(z, P['hg_lb_logits'], P['hg_norm'], S0, l, ls, B, T, col_h)
        x = _merge(x, ya, yb, yc, z, P, l, tm_merge)
        if T % 16 == 0 and M <= XA_SHORT_ROWS:
            x = _xattn_short(x, att_k, att_v, P, l, l, B, T)
        else:
            x = _xattn(x, att_k, att_v, P, l, l, B, T, tq)
        x = _ffn_sublayer(x, P, 'ffn2', l, bf16_weights)
        for lst, s in zip(outs, (C1, n1, m1.reshape(B, -1), hr1.reshape(B, -1), buf1, S1)):
            lst.append(s)
    stacked = tuple(jnp.stack(lst) for lst in outs)
    return x.reshape(B, T, D), stacked, mem_k, mem_v


def kernel(x_prompt, x_sample, mem_prompt, cache_mem_k, cache_mem_v, state_mlstm_C, state_mlstm_n, state_mlstm_m, state_rglru_h, state_rglru_conv, state_hgrn_S, ffn1_norm_pre, ffn1_w_in, ffn1_w_out, ffn1_norm_post, mix_norm_pre, w_in, mlstm_bi, mlstm_bf, mlstm_norm, rg_conv_w, rg_conv_b, rg_wa, rg_ba, rg_wx, rg_bx, rg_lambda, hg_lb_logits, hg_norm, w_br_a, w_br_b, w_br_c, w_out, mix_norm_post, xa_norm_pre, xa_mem_norm, xa_wq, xa_wk, xa_wv, xa_wo, xa_norm_post, ffn2_norm_pre, ffn2_w_in, ffn2_w_out, ffn2_norm_post):
    R = dict(ffn1_norm_pre=ffn1_norm_pre, ffn1_w_in=ffn1_w_in, ffn1_w_out=ffn1_w_out, ffn1_norm_post=ffn1_norm_post,
             mix_norm_pre=mix_norm_pre, w_in=w_in, mlstm_bi=mlstm_bi, mlstm_bf=mlstm_bf, mlstm_norm=mlstm_norm,
             rg_conv_w=rg_conv_w, rg_conv_b=rg_conv_b, rg_wa=rg_wa, rg_ba=rg_ba, rg_wx=rg_wx, rg_bx=rg_bx,
             rg_lambda=rg_lambda, hg_lb_logits=hg_lb_logits, hg_norm=hg_norm,
             w_br_a=w_br_a, w_br_b=w_br_b, w_br_c=w_br_c, w_out=w_out, mix_norm_post=mix_norm_post,
             xa_norm_pre=xa_norm_pre, xa_mem_norm=xa_mem_norm, xa_wq=xa_wq, xa_wk=xa_wk, xa_wv=xa_wv,
             xa_wo=xa_wo, xa_norm_post=xa_norm_post,
             ffn2_norm_pre=ffn2_norm_pre, ffn2_w_in=ffn2_w_in, ffn2_w_out=ffn2_w_out, ffn2_norm_post=ffn2_norm_post)
    P = _prep_params(R)
    bf16_weights = {}
    B = x_prompt.shape[0]
    zeros_like_state = lambda s: jnp.zeros((1, B) + s.shape[2:], F32)
    init = tuple(zeros_like_state(s) for s in (state_mlstm_C, state_mlstm_n, state_mlstm_m, state_rglru_h,
                                               state_rglru_conv, state_hgrn_S))
    y_prompt, p_states, p_mem_k, p_mem_v = _run_trunk(x_prompt, init, False, mem_prompt, None, None, P, bf16_weights)
    s_init = (state_mlstm_C, state_mlstm_n, state_mlstm_m, state_rglru_h, state_rglru_conv, state_hgrn_S)
    y_sample, s_states, _, _ = _run_trunk(x_sample, s_init, True, None, cache_mem_k, cache_mem_v, P, bf16_weights)
    return (y_prompt, y_sample) + p_states + (p_mem_k, p_mem_v) + s_states
```
